```python
import jax, jax.numpy as jnp
from jax import lax
import numpy as np

D_MODEL = 1024
BATCH = 8
SEQ = 4096
DEPTH = 1

MEM_LEN = 256
GDN_HEADS = 4
GDN_DK = 128
GDN_DV = 128
GDN_CONV = 4
GDN_CHUNK = 64
NSA_HEADS = 8
NSA_GROUPS = 2
NSA_DK = 64
NSA_DV = 64
NSA_CMP_BLOCK = 32
NSA_CMP_STRIDE = 16
NSA_SEL_BLOCK = 64
NSA_N_SEL = 16
NSA_WINDOW = 512
NSA_Q_BLOCK = 64
NSA_ROPE_DIM = NSA_DK // 4
MEM_HEADS = 4
MEM_DH = 128
ROPE_THETA = 500000.0
FFN_DIM = 2816
FFN_CONV = 3
EPS = 1e-6

GDN_W = GDN_HEADS * GDN_DV
NSA_W = NSA_HEADS * NSA_DV
MEM_W = MEM_HEADS * MEM_DH
MIX_W = GDN_W + NSA_W + MEM_W
NSA_KV_W = NSA_GROUPS * NSA_DK
IN_SPLITS = (
    3 * GDN_W,
    GDN_HEADS,
    GDN_HEADS,
    GDN_W,
    NSA_HEADS * NSA_DK,
    NSA_KV_W, NSA_KV_W,
    NSA_KV_W, NSA_KV_W,
    NSA_KV_W, NSA_KV_W,
    3 * NSA_HEADS,
    MEM_W,
)
IN_W = sum(IN_SPLITS)

kernel_name = 'hybrid_gdn_nsa_memory_convffn_block'


def rms_norm(x, w):
    xf = x.astype(jnp.float32)
    y = xf * lax.rsqrt(jnp.mean(xf * xf, axis=-1, keepdims=True) + EPS)
    return (y * w.astype(jnp.float32)).astype(x.dtype)


def l2_norm(x):
    xf = x.astype(jnp.float32)
    return xf * lax.rsqrt(jnp.sum(xf * xf, axis=-1, keepdims=True) + EPS)


def causal_dwconv(x, w):
    k, c = w.shape
    return lax.conv_general_dilated(x, w[:, None, :].astype(x.dtype), window_strides=(1,),
                                    padding=[(k - 1, 0)], dimension_numbers=('NWC', 'WIO', 'NWC'),
                                    feature_group_count=c)


def rope_tables(seq):
    pos = jnp.arange(seq, dtype=jnp.float32)
    inv = 1.0 / (ROPE_THETA ** (jnp.arange(0, NSA_ROPE_DIM, 2, dtype=jnp.float32) / NSA_ROPE_DIM))
    ang = pos[:, None] * inv[None, :]
    return jnp.cos(ang), jnp.sin(ang)


def partial_rope(x, cos, sin):
    half = NSA_ROPE_DIM // 2
    c = cos[:, None, :].astype(x.dtype)
    s = sin[:, None, :].astype(x.dtype)
    x1 = x[..., :half]
    x2 = x[..., half:NSA_ROPE_DIM]
    return jnp.concatenate([x1 * c - x2 * s, x2 * c + x1 * s, x[..., NSA_ROPE_DIM:]], axis=-1)


def masked_softmax(s, mask):
    p = jax.nn.softmax(jnp.where(mask, s.astype(jnp.float32), -1e30), axis=-1)
    return jnp.where(mask, p, 0.0)


def gated_delta_rule(q, k, v, a, b, a_log, dt_bias):
    bsz, seq, h, dk = q.shape
    dv = v.shape[-1]
    c = GDN_CHUNK
    n = seq // c
    f32 = jnp.float32
    q = l2_norm(q) * (dk ** -0.5)
    k = l2_norm(k)
    v = v.astype(f32)
    beta = jax.nn.sigmoid(b.astype(f32))
    g = -jnp.exp(a_log.astype(f32)) * jax.nn.softplus(a.astype(f32) + dt_bias.astype(f32))

    def chunk4(t):
        return t.reshape(bsz, n, c, h, -1).transpose(0, 3, 1, 2, 4)

    def chunk3(t):
        return t.reshape(bsz, n, c, h).transpose(0, 3, 1, 2)

    q, k, v = chunk4(q), chunk4(k), chunk4(v)
    beta, g = chunk3(beta), chunk3(g)
    gc = jnp.cumsum(g, axis=-1)
    idx = jnp.arange(c)
    causal = idx[:, None] >= idx[None, :]
    strict = idx[:, None] > idx[None, :]
    diff = gc[..., :, None] - gc[..., None, :]
    decay = jnp.where(causal, jnp.exp(jnp.where(causal, diff, 0.0)), 0.0)
    kb = k * beta[..., None]
    vb = v * beta[..., None]
    lmat = jnp.einsum('bhncd,bhnsd->bhncs', kb, k) * decay * strict
    eye = jnp.eye(c, dtype=f32)
    rhs = jnp.concatenate([vb, kb * jnp.exp(gc)[..., None]], axis=-1)
    sol = lax.linalg.triangular_solve(lmat + eye, rhs, left_side=True, lower=True, unit_diagonal=True)
    u, w = sol[..., :dv], sol[..., dv:]
    aqk = jnp.einsum('bhncd,bhnsd->bhncs', q, k) * decay
    qg = q * jnp.exp(gc)[..., None]
    glast = gc[..., -1]
    kdec = k * jnp.exp(glast[..., None] - gc)[..., None]
    xs = tuple(jnp.moveaxis(t, 2, 0) for t in (u, w, qg, aqk, kdec, glast))

    def step(state, inp):
        u_i, w_i, qg_i, aqk_i, kdec_i, gl_i = inp
        v_new = u_i - jnp.einsum('bhck,bhkv->bhcv', w_i, state)
        o_i = jnp.einsum('bhck,bhkv->bhcv', qg_i, state) + jnp.einsum('bhcs,bhsv->bhcv', aqk_i, v_new)
        state = state * jnp.exp(gl_i)[..., None, None] + jnp.einsum('bhck,bhcv->bhkv', kdec_i, v_new)
        return state, o_i

    state0 = jnp.zeros((bsz, h, dk, dv), f32)
    _, o = lax.scan(step, state0, xs)
    return o.transpose(1, 0, 3, 2, 4).reshape(bsz, seq, h, dv)


def compress_blocks(tok, pos, w1, w2):
    flat = (tok + pos.astype(tok.dtype)).reshape(tok.shape[0], tok.shape[1], tok.shape[2], -1)
    return jax.nn.silu(flat @ w1) @ w2


def nsa_attention(q, kc, vc, ks, vs, kw, vw, gate_logits, cos, sin, q_norm_w, kc_norm_w, ks_norm_w,
                  kw_norm_w, pos_k, pos_v, k_w1, k_w2, v_w1, v_w2):
    bsz, seq = q.shape[:2]
    G = NSA_GROUPS
    R = NSA_HEADS // NSA_GROUPS
    TQ = NSA_Q_BLOCK
    SEL = NSA_SEL_BLOCK
    scale = NSA_DK ** -0.5

    def kv_heads(t):
        return t.reshape(bsz, seq, G, -1)

    q = partial_rope(rms_norm(q.reshape(bsz, seq, NSA_HEADS, NSA_DK), q_norm_w), cos, sin)
    q = q.reshape(bsz, seq, G, R, NSA_DK).transpose(0, 2, 3, 1, 4)
    ks = partial_rope(rms_norm(kv_heads(ks), ks_norm_w), cos, sin).transpose(0, 2, 1, 3)
    kw = partial_rope(rms_norm(kv_heads(kw), kw_norm_w), cos, sin).transpose(0, 2, 1, 3)
    vs = kv_heads(vs).transpose(0, 2, 1, 3)
    vw = kv_heads(vw).transpose(0, 2, 1, 3)

    n_c = (seq - NSA_CMP_BLOCK) // NSA_CMP_STRIDE + 1
    cidx = jnp.arange(n_c)[:, None] * NSA_CMP_STRIDE + jnp.arange(NSA_CMP_BLOCK)[None, :]
    kc_tok = partial_rope(kv_heads(kc), cos, sin).transpose(0, 2, 1, 3)[:, :, cidx]
    vc_tok = kv_heads(vc).transpose(0, 2, 1, 3)[:, :, cidx]
    kcmp = rms_norm(compress_blocks(kc_tok, pos_k, k_w1, k_w2), kc_norm_w)
    vcmp = compress_blocks(vc_tok, pos_v, v_w1, v_w2)
    cmp_end = jnp.arange(n_c) * NSA_CMP_STRIDE + NSA_CMP_BLOCK - 1

    n_s = seq // SEL
    n_sel = min(NSA_N_SEL, n_s)
    ks_blk = ks.reshape(bsz, G, n_s, SEL, NSA_DK)
    vs_blk = vs.reshape(bsz, G, n_s, SEL, NSA_DV)
    ci = jnp.arange(n_c) * NSA_CMP_STRIDE
    sj = jnp.arange(n_s) * SEL
    overlap = (jnp.clip(jnp.minimum(ci[:, None] + NSA_CMP_BLOCK, sj[None, :] + SEL)
                        - jnp.maximum(ci[:, None], sj[None, :]), 0).astype(jnp.float32) / NSA_CMP_STRIDE)
    gather_blocks = jax.vmap(jax.vmap(lambda blocks, ids: blocks[ids]))

    kw_pad = jnp.pad(kw, ((0, 0), (0, 0), (NSA_WINDOW, 0), (0, 0)))
    vw_pad = jnp.pad(vw, ((0, 0), (0, 0), (NSA_WINDOW, 0), (0, 0)))

    gates = jax.nn.sigmoid(gate_logits.astype(jnp.float32)).astype(q.dtype)
    gates = gates.reshape(bsz, seq, G, R, 3).transpose(0, 2, 3, 1, 4)

    def block(i):
        s0 = i * TQ
        t = s0 + jnp.arange(TQ)
        qb = lax.dynamic_slice_in_dim(q, s0, TQ, axis=3)
        gb = lax.dynamic_slice_in_dim(gates, s0, TQ, axis=3)
        cmask = cmp_end[None, :] <= t[:, None]
        sc = jnp.einsum('bgrtd,bgcd->bgrtc', qb, kcmp) * scale
        p_cmp = masked_softmax(sc, cmask)
        o_cmp = jnp.einsum('bgrtc,bgcd->bgrtd', p_cmp.astype(vcmp.dtype), vcmp)
        imp = jnp.einsum('bgrtc,cj->bgtj', p_cmp, overlap)
        blk = jnp.arange(n_s)
        cur = t // SEL
        valid = blk[None, :] <= cur[:, None]
        forced = (blk[None, :] == 0) | (blk[None, :] == cur[:, None]) | (blk[None, :] == cur[:, None] - 1)
        score = jnp.where(valid, jnp.where(forced, 1e6, imp), -1e9)
        _, idx = lax.top_k(score, n_sel)
        kg = gather_blocks(ks_blk, idx).reshape(bsz, G, TQ, n_sel * SEL, NSA_DK)
        vg = gather_blocks(vs_blk, idx).reshape(bsz, G, TQ, n_sel * SEL, NSA_DV)
        kpos = (idx[..., None] * SEL + jnp.arange(SEL)).reshape(bsz, G, TQ, n_sel * SEL)
        smask = (kpos <= t[None, None, :, None])[:, :, None]
        ss = jnp.einsum('bgrtd,bgtkd->bgrtk', qb, kg) * scale
        o_slc = jnp.einsum('bgrtk,bgtkd->bgrtd', masked_softmax(ss, smask).astype(vg.dtype), vg)
        kwb = lax.dynamic_slice_in_dim(kw_pad, s0, NSA_WINDOW + TQ, axis=2)
        vwb = lax.dynamic_slice_in_dim(vw_pad, s0, NSA_WINDOW + TQ, axis=2)
        wpos = s0 - NSA_WINDOW + jnp.arange(NSA_WINDOW + TQ)
        wmask = ((wpos[None, :] >= 0) & (wpos[None, :] <= t[:, None])
                 & (t[:, None] - wpos[None, :] < NSA_WINDOW))
        sw = jnp.einsum('bgrtd,bgkd->bgrtk', qb, kwb) * scale
        o_win = jnp.einsum('bgrtk,bgkd->bgrtd', masked_softmax(sw, wmask).astype(vwb.dtype), vwb)
        return gb[..., 0:1] * o_cmp + gb[..., 1:2] * o_slc + gb[..., 2:3] * o_win

    o = lax.map(block, jnp.arange(seq // TQ))
    return o.transpose(1, 0, 4, 2, 3, 5).reshape(bsz, seq, NSA_W)


def memory_cross_attention(q, mem_n, w_kv, q_norm_w, k_norm_w):
    bsz, seq = q.shape[:2]
    q = rms_norm(q.reshape(bsz, seq, MEM_HEADS, MEM_DH), q_norm_w)
    k, v = jnp.split(mem_n @ w_kv, 2, axis=-1)
    k = rms_norm(k.reshape(bsz, -1, MEM_HEADS, MEM_DH), k_norm_w)
    v = v.reshape(bsz, -1, MEM_HEADS, MEM_DH)
    s = jnp.einsum('bshd,bmhd->bhsm', q, k).astype(jnp.float32) * (MEM_DH ** -0.5)
    p = jax.nn.softmax(s, axis=-1).astype(v.dtype)
    return jnp.einsum('bhsm,bmhd->bshd', p, v).reshape(bsz, seq, MEM_W)


def hybrid_layer(x, mem, cos, sin, attn_norm_w, mem_norm_w, w_in, gdn_conv_w, gdn_a_log, gdn_dt_bias,
                 gdn_out_norm_w, nsa_q_norm_w, nsa_kc_norm_w, nsa_ks_norm_w, nsa_kw_norm_w, nsa_cmp_pos_k,
                 nsa_cmp_pos_v, nsa_cmp_k_w1, nsa_cmp_k_w2, nsa_cmp_v_w1, nsa_cmp_v_w2, mem_w_kv, mem_q_norm_w,
                 mem_k_norm_w, w_out, ffn_norm_w, ffn_w_up, ffn_conv_w, ffn_w_down):
    bsz, seq, _ = x.shape
    xn = rms_norm(x, attn_norm_w)
    proj = xn @ w_in
    (p_qkv, p_a, p_b, p_gate, p_nq, p_kc, p_vc, p_ks, p_vs, p_kw, p_vw, p_ng, p_mq) = jnp.split(
        proj, np.cumsum(IN_SPLITS)[:-1].tolist(), axis=-1)

    qkv = jax.nn.silu(causal_dwconv(p_qkv, gdn_conv_w))
    gq, gk, gv = jnp.split(qkv, 3, axis=-1)
    o_a = gated_delta_rule(gq.reshape(bsz, seq, GDN_HEADS, GDN_DK), gk.reshape(bsz, seq, GDN_HEADS, GDN_DK),
                           gv.reshape(bsz, seq, GDN_HEADS, GDN_DV), p_a, p_b, gdn_a_log, gdn_dt_bias)
    gate_a = jax.nn.silu(p_gate.reshape(bsz, seq, GDN_HEADS, GDN_DV).astype(jnp.float32))
    o_a = (rms_norm(o_a, gdn_out_norm_w) * gate_a).astype(x.dtype).reshape(bsz, seq, GDN_W)

    o_b = nsa_attention(p_nq, p_kc, p_vc, p_ks, p_vs, p_kw, p_vw, p_ng, cos, sin, nsa_q_norm_w, nsa_kc_norm_w,
                        nsa_ks_norm_w, nsa_kw_norm_w, nsa_cmp_pos_k, nsa_cmp_pos_v, nsa_cmp_k_w1, nsa_cmp_k_w2,
                        nsa_cmp_v_w1, nsa_cmp_v_w2)

    o_c = memory_cross_attention(p_mq, rms_norm(mem, mem_norm_w), mem_w_kv, mem_q_norm_w, mem_k_norm_w)

    h = x + jnp.concatenate([o_a, o_b, o_c], axis=-1) @ w_out

    u = causal_dwconv(rms_norm(h, ffn_norm_w) @ ffn_w_up, ffn_conv_w)
    u_gate, u_val = jnp.split(u, 2, axis=-1)
    return h + (jax.nn.silu(u_gate) * u_val) @ ffn_w_down


def setup_inputs(seed: int = 0) -> dict:
    key = jax.random.key(seed)
    keys = iter(jax.random.split(key, 48))
    f32 = jnp.float32
    L = DEPTH

    def nrm(shape, scale):
        return jax.random.normal(next(keys), shape, f32) * scale

    def gain(shape):
        return 1.0 + 0.02 * jax.random.normal(next(keys), shape, f32)

    dt = jnp.exp(jax.random.uniform(next(keys), (L, GDN_HEADS), f32, float(np.log(1e-3)), float(np.log(1e-1))))
    a_log = jnp.log(jax.random.uniform(next(keys), (L, GDN_HEADS), f32, 1.0, 16.0))
    flat_c = NSA_CMP_BLOCK * NSA_DK
    return {
        'x': nrm((BATCH, SEQ, D_MODEL), 1.0),
        'mem': nrm((BATCH, MEM_LEN, D_MODEL), 1.0),
        'attn_norm_w': gain((L, D_MODEL)),
        'mem_norm_w': gain((L, D_MODEL)),
        'w_in': nrm((L, D_MODEL, IN_W), D_MODEL ** -0.5),
        'gdn_conv_w': nrm((L, GDN_CONV, 3 * GDN_W), GDN_CONV ** -0.5),
        'gdn_a_log': a_log,
        'gdn_dt_bias': dt + jnp.log(-jnp.expm1(-dt)),
        'gdn_out_norm_w': gain((L, GDN_DV)),
        'nsa_q_norm_w': gain((L, NSA_DK)),
        'nsa_kc_norm_w': gain((L, NSA_DK)),
        'nsa_ks_norm_w': gain((L, NSA_DK)),
        'nsa_kw_norm_w': gain((L, NSA_DK)),
        'nsa_cmp_pos_k': nrm((L, NSA_CMP_BLOCK, NSA_DK), 0.02),
        'nsa_cmp_pos_v': nrm((L, NSA_CMP_BLOCK, NSA_DV), 0.02),
        'nsa_cmp_k_w1': nrm((L, flat_c, NSA_DK), flat_c ** -0.5),
        'nsa_cmp_k_w2': nrm((L, NSA_DK, NSA_DK), NSA_DK ** -0.5),
        'nsa_cmp_v_w1': nrm((L, NSA_CMP_BLOCK * NSA_DV, NSA_DV), (NSA_CMP_BLOCK * NSA_DV) ** -0.5),
        'nsa_cmp_v_w2': nrm((L, NSA_DV, NSA_DV), NSA_DV ** -0.5),
        'mem_w_kv': nrm((L, D_MODEL, 2 * MEM_W), D_MODEL ** -0.5),
        'mem_q_norm_w': gain((L, MEM_DH)),
        'mem_k_norm_w': gain((L, MEM_DH)),
        'w_out': nrm((L, MIX_W, D_MODEL), MIX_W ** -0.5),
        'ffn_norm_w': gain((L, D_MODEL)),
        'ffn_w_up': nrm((L, D_MODEL, 2 * FFN_DIM), D_MODEL ** -0.5),
        'ffn_conv_w': nrm((L, FFN_CONV, 2 * FFN_DIM), FFN_CONV ** -0.5),
        'ffn_w_down': nrm((L, FFN_DIM, D_MODEL), FFN_DIM ** -0.5),
    }


def reference(x, mem, attn_norm_w, mem_norm_w, w_in, gdn_conv_w, gdn_a_log, gdn_dt_bias, gdn_out_norm_w,
              nsa_q_norm_w, nsa_kc_norm_w, nsa_ks_norm_w, nsa_kw_norm_w, nsa_cmp_pos_k, nsa_cmp_pos_v,
              nsa_cmp_k_w1, nsa_cmp_k_w2, nsa_cmp_v_w1, nsa_cmp_v_w2, mem_w_kv, mem_q_norm_w, mem_k_norm_w,
              w_out, ffn_norm_w, ffn_w_up, ffn_conv_w, ffn_w_down):
    cos, sin = rope_tables(x.shape[1])
    h = x
    for l in range(DEPTH):
        h = hybrid_layer(h, mem, cos, sin, attn_norm_w[l], mem_norm_w[l], w_in[l], gdn_conv_w[l], gdn_a_log[l],
                         gdn_dt_bias[l], gdn_out_norm_w[l], nsa_q_norm_w[l], nsa_kc_norm_w[l], nsa_ks_norm_w[l],
                         nsa_kw_norm_w[l], nsa_cmp_pos_k[l], nsa_cmp_pos_v[l], nsa_cmp_k_w1[l], nsa_cmp_k_w2[l],
                         nsa_cmp_v_w1[l], nsa_cmp_v_w2[l], mem_w_kv[l], mem_q_norm_w[l], mem_k_norm_w[l],
                         w_out[l], ffn_norm_w[l], ffn_w_up[l], ffn_conv_w[l], ffn_w_down[l])
    return h
```

```python
import functools

import jax
import jax.numpy as jnp
import numpy as np
from jax import lax
from jax.experimental import pallas as pl
from jax.experimental.pallas import tpu as pltpu

F32 = jnp.float32
BF16 = jnp.bfloat16

EPS = 1e-6
ROPE_THETA = 500000.0
GDN_HEADS = 4
GDN_D = 128
GDN_CONV = 4
GDN_CHUNK = 64
GDN_SUB = 16
NSA_HEADS = 8
NSA_GROUPS = 2
NSA_REP = NSA_HEADS // NSA_GROUPS
NSA_D = 64
NSA_CMP_BLOCK = 32
NSA_CMP_STRIDE = 16
NSA_SEL_BLOCK = 64
NSA_N_SEL = 16
NSA_WINDOW = 512
NSA_Q_BLOCK = 64
NSA_ROPE_DIM = NSA_D // 4
MEM_HEADS = 4
MEM_D = 128
FFN_CONV = 3

GDN_W = GDN_HEADS * GDN_D
NSA_W = NSA_HEADS * NSA_D
MEM_W = MEM_HEADS * MEM_D
NSA_KV_W = NSA_GROUPS * NSA_D

LANES = 128
HALO = 8
VMEM_LIMIT = 56 * 1024 * 1024
NEG = -1e30


def _cparams(sem):
    return pltpu.CompilerParams(dimension_semantics=sem, vmem_limit_bytes=VMEM_LIMIT)


def _mm(a, b):
    return jnp.dot(a.astype(BF16), b.astype(BF16), preferred_element_type=F32)


def _mm_nt(a, b):
    return lax.dot_general(a.astype(BF16), b.astype(BF16), (((1,), (1,)), ((), ())),
                           preferred_element_type=F32)


def _mm_tn(a, b):
    return lax.dot_general(a.astype(BF16), b.astype(BF16), (((0,), (0,)), ((), ())),
                           preferred_element_type=F32)


def _split3(x):
    hi = x.astype(BF16)
    r = x - hi.astype(F32)
    mid = r.astype(BF16)
    lo = (r - mid.astype(F32)).astype(BF16)
    return hi, mid, lo


def _dot_exact_rhs(x, e):
    hi, mid, lo = _split3(x)
    eb = e.astype(BF16)
    return (jnp.dot(hi, eb, preferred_element_type=F32) + jnp.dot(mid, eb, preferred_element_type=F32)
            + jnp.dot(lo, eb, preferred_element_type=F32))


def _dot_exact_lhs(e, x):
    hi, mid, lo = _split3(x)
    eb = e.astype(BF16)
    return (jnp.dot(eb, hi, preferred_element_type=F32) + jnp.dot(eb, mid, preferred_element_type=F32)
            + jnp.dot(eb, lo, preferred_element_type=F32))


def _dot_exact_lhs_nt(e, x):
    hi, mid, lo = _split3(x)
    eb = e.astype(BF16)
    dn = (((1,), (1,)), ((), ()))
    return (lax.dot_general(eb, hi, dn, preferred_element_type=F32)
            + lax.dot_general(eb, mid, dn, preferred_element_type=F32)
            + lax.dot_general(eb, lo, dn, preferred_element_type=F32))


def _sigmoid(x):
    return 1.0 / (1.0 + jnp.exp(-x))


def _silu(x):
    return x * _sigmoid(x)


def _softplus(x):
    return jnp.maximum(x, 0.0) + jnp.log(1.0 + jnp.exp(-jnp.abs(x)))


def _in_proj_kernel(x_ref, nw_ref, w_ref, *o_refs):
    x = x_ref[...]
    ms = jnp.mean(x * x, axis=-1, keepdims=True)
    xn = (x * lax.rsqrt(ms + EPS) * nw_ref[...]).astype(BF16)
    off = 0
    for o_ref in o_refs:
        wd = o_ref.shape[-1]
        o_ref[...] = jnp.dot(xn, w_ref[:, off:off + wd], preferred_element_type=F32).astype(o_ref.dtype)
        off += wd


def _in_proj(x2, norm_w, w_cat, widths, tm):
    t, d = x2.shape
    wtot = w_cat.shape[1]
    return pl.pallas_call(
        _in_proj_kernel,
        grid=(t // tm,),
        in_specs=[
            pl.BlockSpec((tm, d), lambda i: (i, 0)),
            pl.BlockSpec((1, d), lambda i: (0, 0)),
            pl.BlockSpec((d, wtot), lambda i: (0, 0)),
        ],
        out_specs=[pl.BlockSpec((tm, wd), lambda i: (i, 0)) for wd in widths],
        out_shape=[jax.ShapeDtypeStruct((t, wd), F32) for wd in widths],
        compiler_params=_cparams(("arbitrary",)),
        name="in_proj",
    )(x2, norm_w, w_cat)


def _gdn_kernel(qkv_ref, sm_ref, gate_ref, cw_ref, alog_ref, dtb_ref, onw_ref, ea_ref, eb_ref, ltri_ref,
                e0_ref, o_ref, xb, qn_s, kn_s, v_s, g_s, beta_s, oacc_s, state_s):
    j = pl.program_id(1)
    ts = o_ref.shape[0]
    c = GDN_CHUNK
    hw = GDN_W

    @pl.when(j == 0)
    def _():
        xb[0:HALO, :] = jnp.zeros((HALO, 3 * hw), F32)
        state_s[...] = jnp.zeros_like(state_s)

    xb[HALO:HALO + ts, :] = qkv_ref[...]
    for part in range(3):
        cols = slice(part * hw, (part + 1) * hw)
        acc = cw_ref[GDN_CONV - 1:GDN_CONV, cols] * xb[HALO:HALO + ts, cols]
        for jj in range(GDN_CONV - 1):
            acc = acc + cw_ref[jj:jj + 1, cols] * xb[pl.ds(HALO - (GDN_CONV - 1) + jj, ts), cols]
        act = _silu(acc)
        if part == 2:
            v_s[...] = act
        else:
            dst = qn_s if part == 0 else kn_s
            scale = GDN_D ** -0.5 if part == 0 else 1.0
            for h in range(GDN_HEADS):
                hs = slice(h * GDN_D, (h + 1) * GDN_D)
                xh = act[:, hs]
                ss = jnp.sum(xh * xh, axis=-1, keepdims=True)
                dst[:, hs] = xh * (lax.rsqrt(ss + EPS) * scale)
    xb[0:HALO, :] = xb[ts:ts + HALO, :]

    sm = sm_ref[...]
    a_full = _dot_exact_rhs(sm, ea_ref[...])
    b_full = _dot_exact_rhs(sm, eb_ref[...])
    g_s[...] = -jnp.exp(alog_ref[...]) * _softplus(a_full + dtb_ref[...])
    beta_s[...] = _sigmoid(b_full)

    ri = lax.broadcasted_iota(jnp.int32, (c, c), 0)
    ci = lax.broadcasted_iota(jnp.int32, (c, c), 1)
    causal = ri >= ci
    strict = ri > ci
    blockdiag = (ri // GDN_SUB) == (ci // GDN_SUB)
    ltri = ltri_ref[...]
    e0 = e0_ref[...]

    def chunk_body(ic, carry):
        r0 = pl.multiple_of(ic * c, c)
        rows = pl.ds(r0, c)
        gc = _dot_exact_lhs(ltri, g_s[rows, :])
        glast = gc[c - 1:c, :]
        eg = jnp.exp(gc)
        kfac = jnp.exp(glast - gc)
        egl = jnp.exp(glast)
        beta = beta_s[rows, :]
        kn = kn_s[rows, :]
        qn = qn_s[rows, :]
        kb = kn * beta
        vb = v_s[rows, :] * beta
        kbe = kb * eg
        qg = qn * eg
        kdec = kn * kfac
        for h in range(GDN_HEADS):
            hs = slice(h * GDN_D, (h + 1) * GDN_D)
            gcol = gc[:, h * GDN_D:h * GDN_D + c]
            grow = _dot_exact_lhs_nt(e0, gc[:, hs])
            diff = gcol - grow
            decay = jnp.where(causal, jnp.exp(jnp.where(causal, diff, 0.0)), 0.0)
            lmat = jnp.where(strict, _mm_nt(kb[:, hs], kn[:, hs]) * decay, 0.0)
            aqk = _mm_nt(qn[:, hs], kn[:, hs]) * decay
            p = -lmat
            pd = jnp.where(blockdiag, p, 0.0)
            pn = p - pd
            p2 = _mm(pd, pd)
            p4 = _mm(p2, p2)
            p8 = _mm(p4, p4)
            a1 = pd + p2 + _mm(pd, p2)
            a2 = a1 + p4 + _mm(a1, p4)
            a3 = a2 + p8 + _mm(a2, p8)
            nm = pn + _mm(a3, pn)
            n2 = _mm(nm, nm)
            bm = nm + n2 + _mm(nm, n2)
            tm = bm + a3 + _mm(bm, a3)
            rhs = jnp.concatenate([vb[:, hs], kbe[:, hs]], axis=-1)
            sol = rhs + _mm(tm, rhs)
            u = sol[:, :GDN_D]
            w = sol[:, GDN_D:]
            st = state_s[h]
            v_new = u - _mm(w, st)
            oacc_s[rows, hs] = _mm(qg[:, hs], st) + _mm(aqk, v_new)
            state_s[h] = st * egl[:, hs] + _mm_tn(kdec[:, hs], v_new)
        return carry

    lax.fori_loop(0, ts // c, chunk_body, 0)

    for h in range(GDN_HEADS):
        hs = slice(h * GDN_D, (h + 1) * GDN_D)
        oh = oacc_s[:, hs]
        ms = jnp.mean(oh * oh, axis=-1, keepdims=True)
        o_ref[:, hs] = (oh * lax.rsqrt(ms + EPS) * onw_ref[...] * _silu(gate_ref[:, hs])).astype(o_ref.dtype)


def _gdn(qkv, small, gate, conv_w, a_log, dt_bias, out_norm_w, ts):
    b, s, _ = qkv.shape
    hw = GDN_W
    c = GDN_CHUNK
    rep = lambda v: jnp.repeat(v.astype(F32), GDN_D)[None, :]
    lane_head = np.arange(hw) // GDN_D
    ea = (np.arange(LANES)[:, None] == lane_head[None, :]).astype(np.float32)
    eb = (np.arange(LANES)[:, None] == (lane_head[None, :] + GDN_HEADS)).astype(np.float32)
    ltri = np.tril(np.ones((c, c), np.float32))
    e0 = np.zeros((c, LANES), np.float32)
    e0[:, 0] = 1.0
    full = lambda shape: pl.BlockSpec(shape, lambda ib, ij: (0,) * len(shape))
    return pl.pallas_call(
        _gdn_kernel,
        grid=(b, s // ts),
        in_specs=[
            pl.BlockSpec((None, ts, 3 * hw), lambda ib, ij: (ib, ij, 0)),
            pl.BlockSpec((None, ts, LANES), lambda ib, ij: (ib, ij, 0)),
            pl.BlockSpec((None, ts, hw), lambda ib, ij: (ib, ij, 0)),
            full((GDN_CONV, 3 * hw)), full((1, hw)), full((1, hw)), full((1, GDN_D)),
            full((LANES, hw)), full((LANES, hw)), full((c, c)), full((c, LANES)),
        ],
        out_specs=pl.BlockSpec((None, ts, hw), lambda ib, ij: (ib, ij, 0)),
        out_shape=jax.ShapeDtypeStruct((b, s, hw), BF16),
        scratch_shapes=[
            pltpu.VMEM((ts + HALO, 3 * hw), F32),
            pltpu.VMEM((ts, hw), F32), pltpu.VMEM((ts, hw), F32), pltpu.VMEM((ts, hw), F32),
            pltpu.VMEM((ts, hw), F32), pltpu.VMEM((ts, hw), F32), pltpu.VMEM((ts, hw), F32),
            pltpu.VMEM((GDN_HEADS, GDN_D, GDN_D), F32),
        ],
        compiler_params=_cparams(("arbitrary", "arbitrary")),
        name="gdn",
    )(qkv, small, gate, conv_w.astype(F32), rep(a_log), rep(dt_bias), out_norm_w.astype(F32)[None, :],
      jnp.asarray(ea), jnp.asarray(eb), jnp.asarray(ltri), jnp.asarray(e0))


def _rope_tables(s):
    half = NSA_ROPE_DIM // 2
    pos = jnp.arange(s, dtype=F32)
    inv = 1.0 / (ROPE_THETA ** (jnp.arange(0, NSA_ROPE_DIM, 2, dtype=F32) / NSA_ROPE_DIM))
    ang = pos[:, None] * inv[None, :]
    cos, sin = jnp.cos(ang), jnp.sin(ang)
    one = jnp.ones((s, NSA_D - NSA_ROPE_DIM), F32)
    zero = jnp.zeros((s, NSA_D - NSA_ROPE_DIM), F32)
    zh = jnp.zeros((s, half), F32)
    tc = jnp.concatenate([cos, cos, one], axis=-1)
    ta = jnp.concatenate([-sin, zh, zero], axis=-1)
    tb = jnp.concatenate([zh, sin, zero], axis=-1)
    dup = lambda t: jnp.concatenate([t, t], axis=-1)
    return dup(tc), dup(ta), dup(tb)


def _rope(x, tc, ta, tb):
    half = NSA_ROPE_DIM // 2
    return x * tc + pltpu.roll(x, LANES - half, 1) * ta + pltpu.roll(x, half, 1) * tb


def _group_ms(x, ones_blk):
    return _dot_exact_rhs(x * x, ones_blk) * (1.0 / NSA_D)


def _dup_groups(x):
    r = pltpu.roll(x, NSA_D, 1)
    lane = lax.broadcasted_iota(jnp.int32, x.shape, 1)
    lo = lane < NSA_D
    return jnp.where(lo, x, r), jnp.where(lo, r, x)


def _nsa_prep_kernel(nq_ref, kv_ref, tc_ref, ta_ref, tb_ref, qw_ref, ksw_ref, kww_ref, ones_ref,
                     q_ref, ks_ref, vs_ref, kw_ref, vw_ref):
    tc, ta, tb = tc_ref[...], ta_ref[...], tb_ref[...]
    ones_blk = ones_ref[...]
    scale = NSA_D ** -0.5
    for p in range(NSA_W // LANES):
        cols = slice(p * LANES, (p + 1) * LANES)
        x = nq_ref[:, cols]
        xn = x * lax.rsqrt(_group_ms(x, ones_blk) + EPS) * qw_ref[...]
        q_ref[:, cols] = (_rope(xn, tc, ta, tb) * scale).astype(q_ref.dtype)
    for src, nw_ref, k_out, v_out in ((0, ksw_ref, ks_ref, vs_ref), (2, kww_ref, kw_ref, vw_ref)):
        k = kv_ref[:, src * LANES:(src + 1) * LANES]
        v = kv_ref[:, (src + 1) * LANES:(src + 2) * LANES]
        kn = k * lax.rsqrt(_group_ms(k, ones_blk) + EPS) * nw_ref[...]
        kr = _rope(kn, tc, ta, tb)
        k0, k1 = _dup_groups(kr)
        v0, v1 = _dup_groups(v)
        k_out[0] = k0.astype(k_out.dtype)
        k_out[1] = k1.astype(k_out.dtype)
        v_out[0] = v0.astype(v_out.dtype)
        v_out[1] = v1.astype(v_out.dtype)


def _nsa_prep(nq, kv4, tabs, q_norm_w, ks_norm_w, kw_norm_w, ts):
    b, s, _ = nq.shape
    g = NSA_GROUPS
    tile2 = lambda w: jnp.concatenate([w, w]).astype(F32)[None, :]
    ones_blk = np.kron(np.eye(2, dtype=np.float32), np.ones((NSA_D, NSA_D), np.float32))
    full = lambda shape: pl.BlockSpec(shape, lambda ib, ij: (0,) * len(shape))
    tok = lambda w: pl.BlockSpec((None, ts, w), lambda ib, ij: (ib, ij, 0))
    tab = pl.BlockSpec((ts, LANES), lambda ib, ij: (ij, 0))
    kv_out = pl.BlockSpec((None, g, ts, LANES), lambda ib, ij: (ib, 0, ij, 0))
    kv_shape = jax.ShapeDtypeStruct((b, g, s, LANES), BF16)
    return pl.pallas_call(
        _nsa_prep_kernel,
        grid=(b, s // ts),
        in_specs=[tok(NSA_W), tok(4 * LANES), tab, tab, tab,
                  full((1, LANES)), full((1, LANES)), full((1, LANES)), full((LANES, LANES))],
        out_specs=[tok(NSA_W), kv_out, kv_out, kv_out, kv_out],
        out_shape=[jax.ShapeDtypeStruct((b, s, NSA_W), BF16), kv_shape, kv_shape, kv_shape, kv_shape],
        compiler_params=_cparams(("arbitrary", "arbitrary")),
        name="nsa_prep",
    )(nq, kv4, *tabs, tile2(q_norm_w), tile2(ks_norm_w), tile2(kw_norm_w), jnp.asarray(ones_blk))


def _nsa_compress_kernel(kc_ref, vc_ref, tc_ref, ta_ref, tb_ref, pk_ref, pv_ref, kw1_ref, kw2_ref,
                         vw1_ref, vw2_ref, nw_ref, ones_ref, kc_out, vc_out):
    nrow, width = kc_ref.shape
    half = width
    outs = []
    for is_k in (True, False):
        x = (kc_ref if is_k else vc_ref)[...]
        if is_k:
            x = jnp.concatenate(
                [_rope(x[:, p * LANES:(p + 1) * LANES], tc_ref[:, p * LANES:(p + 1) * LANES],
                       ta_ref[:, p * LANES:(p + 1) * LANES], tb_ref[:, p * LANES:(p + 1) * LANES])
                 for p in range(width // LANES)], axis=-1)
        pos_ref, w1_ref, w2_ref = (pk_ref, kw1_ref, kw2_ref) if is_k else (pv_ref, vw1_ref, vw2_ref)
        first = _mm(x + pos_ref[0:1, :], w1_ref[0:half, :])
        second = _mm(x + pos_ref[1:2, :], w1_ref[half:2 * half, :])
        y = first + pltpu.roll(second, nrow - 1, 0)
        y = _mm(_silu(y), w2_ref[...])
        if is_k:
            y = y * lax.rsqrt(_group_ms(y, ones_ref[...]) + EPS) * nw_ref[...]
        outs.append(y)
    for y, out in zip(outs, (kc_out, vc_out)):
        y0, y1 = _dup_groups(y)
        out[0] = y0.astype(out.dtype)
        out[1] = y1.astype(out.dtype)


def _nsa_compress(kc, vc, tabs, pos_k, pos_v, k_w1, k_w2, v_w1, v_w2, kc_norm_w):
    b, s, _ = kc.shape
    g = NSA_GROUPS
    st = NSA_CMP_STRIDE
    nrow = s // st
    width = st * LANES
    flat = lambda t: t.reshape(t.shape[0], nrow, width)
    eye_g = jnp.eye(g, dtype=F32)

    def w1_blk(w1):
        wl = w1.reshape(NSA_CMP_BLOCK, NSA_D, NSA_D)
        return jnp.einsum("lde,gh->lgdhe", wl, eye_g).reshape(NSA_CMP_BLOCK * LANES, LANES).astype(BF16)

    def w2_blk(w2):
        return jnp.einsum("de,gh->gdhe", w2, eye_g).reshape(LANES, LANES).astype(BF16)

    def pos_rows(p):
        return jnp.concatenate([p, p], axis=-1).reshape(2, width).astype(F32)

    ones_blk = np.kron(np.eye(2, dtype=np.float32), np.ones((NSA_D, NSA_D), np.float32))
    full = lambda shape: pl.BlockSpec(shape, lambda ib: (0,) * len(shape))
    seq = pl.BlockSpec((None, nrow, width), lambda ib: (ib, 0, 0))
    out = pl.BlockSpec((None, g, nrow, LANES), lambda ib: (ib, 0, 0, 0))
    oshape = jax.ShapeDtypeStruct((b, g, nrow, LANES), BF16)
    tabs_r = [t.reshape(nrow, width) for t in tabs]
    return pl.pallas_call(
        _nsa_compress_kernel,
        grid=(b,),
        in_specs=[seq, seq, full((nrow, width)), full((nrow, width)), full((nrow, width)),
                  full((2, width)), full((2, width)),
                  full((2 * width, LANES)), full((LANES, LANES)), full((2 * width, LANES)), full((LANES, LANES)),
                  full((1, LANES)), full((LANES, LANES))],
        out_specs=[out, out],
        out_shape=[oshape, oshape],
        compiler_params=_cparams(("arbitrary",)),
        name="nsa_compress",
    )(flat(kc), flat(vc), *tabs_r, pos_rows(pos_k), pos_rows(pos_v), w1_blk(k_w1), w2_blk(k_w2),
      w1_blk(v_w1), w2_blk(v_w2), jnp.concatenate([kc_norm_w, kc_norm_w]).astype(F32)[None, :],
      jnp.asarray(ones_blk))


def _nsa_attn_kernel(q_ref, sm_ref, kc_ref, vc_ref, ks_ref, vs_ref, kw_ref, vw_ref, ov_ref, ex_ref, eg_ref,
                     o_ref, *, kb, wlen):
    i = pl.program_id(2)
    tq = NSA_Q_BLOCK
    rep = NSA_REP
    rows = rep * tq
    s0 = i * tq
    ncmp = kc_ref.shape[0]

    q = q_ref[...]
    lane_q = lax.broadcasted_iota(jnp.int32, (tq, LANES), 1)
    pieces = []
    for r in range(rep):
        tile = q[:, (r // 2) * LANES:(r // 2 + 1) * LANES]
        keep = (lane_q < NSA_D) if r % 2 == 0 else (lane_q >= NSA_D)
        pieces.append(jnp.where(keep, tile, jnp.zeros_like(tile)))
    qs = jnp.concatenate(pieces, axis=0)

    def tpos(shape):
        return s0 + lax.broadcasted_iota(jnp.int32, shape, 0) % tq

    sc = _mm_nt(qs, kc_ref[...])
    cend = lax.broadcasted_iota(jnp.int32, (rows, ncmp), 1) * NSA_CMP_STRIDE + (NSA_CMP_BLOCK - 1)
    cmask = cend <= tpos((rows, ncmp))
    sc = jnp.where(cmask, sc, NEG)
    m = jnp.max(sc, axis=-1, keepdims=True)
    p = jnp.where(cmask, jnp.exp(sc - m), 0.0)
    l = jnp.sum(p, axis=-1, keepdims=True)
    p = p * (1.0 / jnp.maximum(l, 1e-30))
    o_cmp = _mm(p, vc_ref[...])

    psum = p[0:tq]
    for r in range(1, rep):
        psum = psum + p[r * tq:(r + 1) * tq]
    imp = _dot_exact_rhs(psum, ov_ref[...])
    blk = lax.broadcasted_iota(jnp.int32, (tq, LANES), 1)
    nblk = ks_ref.shape[0] // NSA_SEL_BLOCK
    valid = blk <= i
    forced = (blk == 0) | (blk == i) | (blk == i - 1)
    score = jnp.where(valid, jnp.where(forced, 1e6, imp), -1e9)
    score = jnp.where(blk < nblk, score, -2e9)
    rank = jnp.zeros((tq, LANES), F32)
    for jb in range(nblk):
        col = score[:, jb:jb + 1]
        ahead = (col > score) | ((col == score) & (blk > jb))
        rank = rank + jnp.where(ahead, 1.0, 0.0)
    sel = jnp.where(rank < float(NSA_N_SEL), 1.0, 0.0).astype(BF16)
    sel4 = jnp.concatenate([sel] * rep, axis=0)

    def slc_body(ic, carry):
        m_i, l_i, acc = carry
        k0 = pl.multiple_of(ic * kb, kb)
        s = _mm_nt(qs, ks_ref[pl.ds(k0, kb), :])
        picked = jnp.dot(sel4, ex_ref[:, pl.ds(k0, kb)], preferred_element_type=F32)
        kpos = k0 + lax.broadcasted_iota(jnp.int32, (rows, kb), 1)
        ok = (picked > 0.5) & (kpos <= tpos((rows, kb)))
        s = jnp.where(ok, s, NEG)
        m_new = jnp.maximum(m_i, jnp.max(s, axis=-1, keepdims=True))
        alpha = jnp.exp(m_i - m_new)
        pr = jnp.where(ok, jnp.exp(s - m_new), 0.0)
        l_new = alpha * l_i + jnp.sum(pr, axis=-1, keepdims=True)
        acc_new = alpha * acc + _mm(pr, vs_ref[pl.ds(k0, kb), :])
        return m_new, l_new, acc_new

    nch = (s0 + tq + kb - 1) // kb
    m0 = jnp.full((rows, 1), NEG, F32)
    l0 = jnp.zeros((rows, 1), F32)
    a0 = jnp.zeros((rows, LANES), F32)
    _, l_s, acc_s = lax.fori_loop(0, nch, slc_body, (m0, l0, a0))
    o_slc = acc_s * (1.0 / l_s)

    w0 = pl.multiple_of(jnp.maximum(s0 + tq - wlen, 0), tq)
    sw = _mm_nt(qs, kw_ref[pl.ds(w0, wlen), :])
    wpos = w0 + lax.broadcasted_iota(jnp.int32, (rows, wlen), 1)
    tw = tpos((rows, wlen))
    wok = (wpos <= tw) & (tw - wpos < NSA_WINDOW)
    sw = jnp.where(wok, sw, NEG)
    mw = jnp.max(sw, axis=-1, keepdims=True)
    pw = jnp.where(wok, jnp.exp(sw - mw), 0.0)
    lw = jnp.sum(pw, axis=-1, keepdims=True)
    o_win = _mm(pw, vw_ref[pl.ds(w0, wlen), :]) * (1.0 / lw)

    gates = _dot_exact_rhs(_sigmoid(sm_ref[...]), eg_ref[...])
    heads = []
    for r in range(rep):
        rs = slice(r * tq, (r + 1) * tq)
        gsl = lambda x: gates[:, (r * 3 + x) * LANES:(r * 3 + x + 1) * LANES]
        heads.append(gsl(0) * o_cmp[rs] + gsl(1) * o_slc[rs] + gsl(2) * o_win[rs])
    lo = lane_q < NSA_D
    for pr_ in range(rep // 2):
        o_ref[:, pr_ * LANES:(pr_ + 1) * LANES] = jnp.where(lo, heads[2 * pr_], heads[2 * pr_ + 1]).astype(o_ref.dtype)


def _nsa_attn(q, small, kcmp, vcmp, ks2, vs2, kw2, vw2, gate_col0):
    b, s, _ = q.shape
    g = NSA_GROUPS
    rep = NSA_REP
    tq = NSA_Q_BLOCK
    ncmp = kcmp.shape[2]
    nblk = s // NSA_SEL_BLOCK
    kb = min(512, s)
    wlen = min(NSA_WINDOW + 2 * tq, s)
    ci = np.arange(ncmp) * NSA_CMP_STRIDE
    sj = np.arange(nblk) * NSA_SEL_BLOCK
    ov = np.clip(np.minimum(ci[:, None] + NSA_CMP_BLOCK, sj[None, :] + NSA_SEL_BLOCK)
                 - np.maximum(ci[:, None], sj[None, :]), 0, None).astype(np.float32) / NSA_CMP_STRIDE
    ov_p = np.zeros((ncmp, LANES), np.float32)
    ov_p[:, :nblk] = ov
    ex = np.zeros((LANES, s), np.float32)
    ex[np.arange(s) // NSA_SEL_BLOCK, np.arange(s)] = 1.0
    eg = np.zeros((g, LANES, rep * 3 * LANES), np.float32)
    for ig in range(g):
        for r in range(rep):
            for x in range(3):
                eg[ig, gate_col0 + ig * rep * 3 + r * 3 + x, (r * 3 + x) * LANES:(r * 3 + x + 1) * LANES] = 1.0
    seq = lambda n: pl.BlockSpec((None, None, n, LANES), lambda ib, ig, ii: (ib, ig, 0, 0))
    full = lambda shape: pl.BlockSpec(shape, lambda ib, ig, ii: (0,) * len(shape))
    return pl.pallas_call(
        functools.partial(_nsa_attn_kernel, kb=kb, wlen=wlen),
        grid=(b, g, s // tq),
        in_specs=[
            pl.BlockSpec((None, tq, rep * NSA_D), lambda ib, ig, ii: (ib, ii, ig)),
            pl.BlockSpec((None, tq, LANES), lambda ib, ig, ii: (ib, ii, 0)),
            seq(ncmp), seq(ncmp), seq(s), seq(s), seq(s), seq(s),
            full((ncmp, LANES)), full((LANES, s)),
            pl.BlockSpec((None, LANES, rep * 3 * LANES), lambda ib, ig, ii: (ig, 0, 0)),
        ],
        out_specs=pl.BlockSpec((None, tq, rep * NSA_D), lambda ib, ig, ii: (ib, ii, ig)),
        out_shape=jax.ShapeDtypeStruct((b, s, NSA_W), BF16),
        compiler_params=_cparams(("arbitrary", "arbitrary", "arbitrary")),
        name="nsa_attn",
    )(q, small, kcmp, vcmp, ks2, vs2, kw2, vw2, jnp.asarray(ov_p), jnp.asarray(ex, dtype=BF16),
      jnp.asarray(eg, dtype=BF16))


def _mem_kv_kernel(mem_ref, nw_ref, w_ref, knw_ref, k_ref, v_ref):
    x = mem_ref[...]
    ms = jnp.mean(x * x, axis=-1, keepdims=True)
    xn = x * lax.rsqrt(ms + EPS) * nw_ref[...]
    kv = _mm(xn, w_ref[...])
    v_ref[...] = kv[:, MEM_W:].astype(v_ref.dtype)
    for h in range(MEM_HEADS):
        hs = slice(h * MEM_D, (h + 1) * MEM_D)
        kh = kv[:, hs]
        msk = jnp.mean(kh * kh, axis=-1, keepdims=True)
        k_ref[:, hs] = (kh * lax.rsqrt(msk + EPS) * knw_ref[...]).astype(k_ref.dtype)


def _mem_kv(mem, mem_norm_w, w_kv, k_norm_w):
    b, m, d = mem.shape
    full = lambda shape: pl.BlockSpec(shape, lambda ib: (0,) * len(shape))
    blk = lambda w: pl.BlockSpec((None, m, w), lambda ib: (ib, 0, 0))
    return pl.pallas_call(
        _mem_kv_kernel,
        grid=(b,),
        in_specs=[blk(d), full((1, d)), full((d, 2 * MEM_W)), full((1, MEM_D))],
        out_specs=[blk(MEM_W), blk(MEM_W)],
        out_shape=[jax.ShapeDtypeStruct((b, m, MEM_W), BF16)] * 2,
        compiler_params=_cparams(("arbitrary",)),
        name="mem_kv",
    )(mem, mem_norm_w.astype(F32)[None, :], w_kv.astype(BF16), k_norm_w.astype(F32)[None, :])


def _mem_attn_kernel(q_ref, k_ref, v_ref, qnw_ref, o_ref):
    scale = MEM_D ** -0.5
    for h in range(MEM_HEADS):
        hs = slice(h * MEM_D, (h + 1) * MEM_D)
        qh = q_ref[:, hs]
        ms = jnp.mean(qh * qh, axis=-1, keepdims=True)
        qn = qh * lax.rsqrt(ms + EPS) * qnw_ref[...]
        s = _mm_nt(qn, k_ref[:, hs]) * scale
        m = jnp.max(s, axis=-1, keepdims=True)
        p = jnp.exp(s - m)
        l = jnp.sum(p, axis=-1, keepdims=True)
        o_ref[:, hs] = (_mm(p, v_ref[:, hs]) * (1.0 / l)).astype(o_ref.dtype)


def _mem_attn(mq, k, v, q_norm_w, ts):
    b, s, _ = mq.shape
    m = k.shape[1]
    return pl.pallas_call(
        _mem_attn_kernel,
        grid=(b, s // ts),
        in_specs=[
            pl.BlockSpec((None, ts, MEM_W), lambda ib, ij: (ib, ij, 0)),
            pl.BlockSpec((None, m, MEM_W), lambda ib, ij: (ib, 0, 0)),
            pl.BlockSpec((None, m, MEM_W), lambda ib, ij: (ib, 0, 0)),
            pl.BlockSpec((1, MEM_D), lambda ib, ij: (0, 0)),
        ],
        out_specs=pl.BlockSpec((None, ts, MEM_W), lambda ib, ij: (ib, ij, 0)),
        out_shape=jax.ShapeDtypeStruct((b, s, MEM_W), BF16),
        compiler_params=_cparams(("arbitrary", "arbitrary")),
        name="mem_attn",
    )(mq, k, v, q_norm_w.astype(F32)[None, :])


def _out_proj_kernel(x_ref, oa_ref, ob_ref, oc_ref, w_ref, h_ref):
    acc = x_ref[...]
    off = 0
    for o_ref in (oa_ref, ob_ref, oc_ref):
        wd = o_ref.shape[-1]
        acc = acc + jnp.dot(o_ref[...].astype(BF16), w_ref[off:off + wd, :], preferred_element_type=F32)
        off += wd
    h_ref[...] = acc


def _out_proj(x2, oa, ob, oc, w_out, tm):
    t, d = x2.shape
    row = lambda w: pl.BlockSpec((tm, w), lambda i: (i, 0))
    return pl.pallas_call(
        _out_proj_kernel,
        grid=(t // tm,),
        in_specs=[row(d), row(oa.shape[1]), row(ob.shape[1]), row(oc.shape[1]),
                  pl.BlockSpec(w_out.shape, lambda i: (0, 0))],
        out_specs=row(d),
        out_shape=jax.ShapeDtypeStruct((t, d), F32),
        compiler_params=_cparams(("arbitrary",)),
        name="out_proj",
    )(x2, oa, ob, oc, w_out)


def _ffn_kernel(h_ref, halo_ref, nw_ref, wup_ref, cw_ref, wdn_ref, o_ref, hn_s, u_s, acc_s, *, fc):
    j = pl.program_id(1)
    ts = h_ref.shape[0]
    f = wdn_ref.shape[0]

    def norm(x):
        ms = jnp.mean(x * x, axis=-1, keepdims=True)
        return (x * lax.rsqrt(ms + EPS) * nw_ref[...]).astype(BF16)

    halo = jnp.where(j > 0, halo_ref[...], 0.0)
    hn_s[0:HALO, :] = norm(halo).astype(hn_s.dtype)
    hn_s[HALO:HALO + ts, :] = norm(h_ref[...]).astype(hn_s.dtype)
    acc_s[...] = h_ref[...]
    for ic in range(f // fc):
        hn = hn_s[...].astype(BF16)
        for part in range(2):
            cols = slice(part * f + ic * fc, part * f + (ic + 1) * fc)
            u_s[part] = jnp.dot(hn, wup_ref[:, cols], preferred_element_type=F32)
        conv = []
        for part in range(2):
            cols = slice(part * f + ic * fc, part * f + (ic + 1) * fc)
            acc = cw_ref[FFN_CONV - 1:FFN_CONV, cols] * u_s[part, HALO:HALO + ts, :]
            for jj in range(FFN_CONV - 1):
                acc = acc + cw_ref[jj:jj + 1, cols] * u_s[part, pl.ds(HALO - (FFN_CONV - 1) + jj, ts), :]
            conv.append(acc)
        act = (_silu(conv[0]) * conv[1]).astype(BF16)
        acc_s[...] += jnp.dot(act, wdn_ref[ic * fc:(ic + 1) * fc, :], preferred_element_type=F32)
    o_ref[...] = acc_s[...]


def _ffn(h, norm_w, w_up, conv_w, w_down, ts, fc):
    b, s, d = h.shape
    f = w_down.shape[0]
    full = lambda shape: pl.BlockSpec(shape, lambda ib, ij: (0,) * len(shape))
    return pl.pallas_call(
        functools.partial(_ffn_kernel, fc=fc),
        grid=(b, s // ts),
        in_specs=[
            pl.BlockSpec((None, ts, d), lambda ib, ij: (ib, ij, 0)),
            pl.BlockSpec((None, HALO, d), lambda ib, ij: (ib, jnp.maximum(ij * (ts // HALO) - 1, 0), 0)),
            full((1, d)), full((d, 2 * f)), full((FFN_CONV, 2 * f)), full((f, d)),
        ],
        out_specs=pl.BlockSpec((None, ts, d), lambda ib, ij: (ib, ij, 0)),
        out_shape=jax.ShapeDtypeStruct((b, s, d), F32),
        scratch_shapes=[
            pltpu.VMEM((ts + HALO, d), F32),
            pltpu.VMEM((2, ts + HALO, fc), F32),
            pltpu.VMEM((ts, d), F32),
        ],
        compiler_params=_cparams(("arbitrary", "arbitrary")),
        name="ffn",
    )(h, h, norm_w.astype(F32)[None, :], w_up, conv_w.astype(F32), w_down)


def _split_w_in(w_in):
    sizes = (3 * GDN_W, GDN_HEADS, GDN_HEADS, GDN_W, NSA_W, NSA_KV_W, NSA_KV_W, NSA_KV_W, NSA_KV_W,
             NSA_KV_W, NSA_KV_W, 3 * NSA_HEADS, MEM_W)
    offs = np.concatenate([[0], np.cumsum(sizes)])
    (qkv, a, bb, gate, nq, kc, vc, ks, vs, kw, vw, ng, mq) = [w_in[:, offs[i]:offs[i + 1]] for i in range(len(sizes))]
    n_small = 2 * GDN_HEADS + 3 * NSA_HEADS
    small = jnp.concatenate([a, bb, ng, jnp.zeros((w_in.shape[0], LANES - n_small), w_in.dtype)], axis=1)
    widths = (3 * GDN_W, GDN_W, NSA_W, NSA_KV_W, NSA_KV_W, 4 * NSA_KV_W, MEM_W, LANES)
    w_cat = jnp.concatenate([qkv, gate, nq, kc, vc, ks, vs, kw, vw, mq, small], axis=1).astype(BF16)
    return w_cat, widths


def _layer(x, mem, attn_norm_w, mem_norm_w, w_in, gdn_conv_w, gdn_a_log, gdn_dt_bias, gdn_out_norm_w,
           nsa_q_norm_w, nsa_kc_norm_w, nsa_ks_norm_w, nsa_kw_norm_w, nsa_cmp_pos_k, nsa_cmp_pos_v,
           nsa_cmp_k_w1, nsa_cmp_k_w2, nsa_cmp_v_w1, nsa_cmp_v_w2, mem_w_kv, mem_q_norm_w, mem_k_norm_w,
           w_out, ffn_norm_w, ffn_w_up, ffn_conv_w, ffn_w_down):
    b, s, d = x.shape
    t = b * s
    ts = min(512, s)
    x2 = x.reshape(t, d)

    w_cat, widths = _split_w_in(w_in)
    qkv, gate, nq, kc, vc, kv4, mq, small = _in_proj(x2, attn_norm_w.astype(F32)[None, :], w_cat, widths, ts)
    r3 = lambda a: a.reshape(b, s, a.shape[-1])

    o_a = _gdn(r3(qkv), r3(small), r3(gate), gdn_conv_w, gdn_a_log, gdn_dt_bias, gdn_out_norm_w, ts)

    tabs = _rope_tables(s)
    q_r, ks2, vs2, kw2, vw2 = _nsa_prep(r3(nq), r3(kv4), tabs, nsa_q_norm_w, nsa_ks_norm_w, nsa_kw_norm_w, ts)
    kcmp, vcmp = _nsa_compress(r3(kc), r3(vc), tabs, nsa_cmp_pos_k, nsa_cmp_pos_v, nsa_cmp_k_w1, nsa_cmp_k_w2,
                               nsa_cmp_v_w1, nsa_cmp_v_w2, nsa_kc_norm_w)
    o_b = _nsa_attn(q_r, r3(small), kcmp, vcmp, ks2, vs2, kw2, vw2, 2 * GDN_HEADS)

    mk, mv = _mem_kv(mem, mem_norm_w, mem_w_kv, mem_k_norm_w)
    o_c = _mem_attn(r3(mq), mk, mv, mem_q_norm_w, ts)

    h = _out_proj(x2, o_a.reshape(t, GDN_W), o_b.reshape(t, NSA_W), o_c.reshape(t, MEM_W), w_out.astype(BF16), ts)
    out = _ffn(h.reshape(b, s, d), ffn_norm_w, ffn_w_up.astype(BF16), ffn_conv_w, ffn_w_down.astype(BF16), ts, 256)
    return out


def kernel(x, mem, attn_norm_w, mem_norm_w, w_in, gdn_conv_w, gdn_a_log, gdn_dt_bias, gdn_out_norm_w, nsa_q_norm_w, nsa_kc_norm_w, nsa_ks_norm_w, nsa_kw_norm_w, nsa_cmp_pos_k, nsa_cmp_pos_v, nsa_cmp_k_w1, nsa_cmp_k_w2, nsa_cmp_v_w1, nsa_cmp_v_w2, mem_w_kv, mem_q_norm_w, mem_k_norm_w, w_out, ffn_norm_w, ffn_w_up, ffn_conv_w, ffn_w_down):
    h = x
    for l in range(w_in.shape[0]):
        h = _layer(h, mem, attn_norm_w[l], mem_norm_w[l], w_in[l], gdn_conv_w[l], gdn_a_log[l], gdn_dt_bias[l],
                   gdn_out_norm_w[l], nsa_q_norm_w[l], nsa_kc_norm_w[l], nsa_ks_norm_w[l], nsa_kw_norm_w[l],
                   nsa_cmp_pos_k[l], nsa_cmp_pos_v[l], nsa_cmp_k_w1[l], nsa_cmp_k_w2[l], nsa_cmp_v_w1[l],
                   nsa_cmp_v_w2[l], mem_w_kv[l], mem_q_norm_w[l], mem_k_norm_w[l], w_out[l], ffn_norm_w[l],
                   ffn_w_up[l], ffn_conv_w[l], ffn_w_down[l])
    return h
```

```python
import functools

import jax
import jax.numpy as jnp
import numpy as np
from jax import lax
from jax.experimental import pallas as pl
from jax.experimental.pallas import tpu as pltpu

F32 = jnp.float32
BF16 = jnp.bfloat16

EPS = 1e-6
ROPE_THETA = 500000.0
GDN_HEADS = 4
GDN_D = 128
GDN_CONV = 4
GDN_CHUNK = 64
GDN_SUB = 16
GDN_GROUP = 4
NSA_HEADS = 8
NSA_GROUPS = 2
NSA_REP = NSA_HEADS // NSA_GROUPS
NSA_D = 64
NSA_CMP_BLOCK = 32
NSA_CMP_STRIDE = 16
NSA_SEL_BLOCK = 64
NSA_N_SEL = 16
NSA_WINDOW = 512
NSA_Q_BLOCK = 64
NSA_ROPE_DIM = NSA_D // 4
MEM_HEADS = 4
MEM_D = 128
FFN_CONV = 3

GDN_W = GDN_HEADS * GDN_D
NSA_W = NSA_HEADS * NSA_D
MEM_W = MEM_HEADS * MEM_D
NSA_KV_W = NSA_GROUPS * NSA_D

LANES = 128
HALO = 8
VMEM_LIMIT = 56 * 1024 * 1024
NEG = -1e30
LOG2E = 1.4426950408889634


def _cparams(sem):
    return pltpu.CompilerParams(dimension_semantics=sem, vmem_limit_bytes=VMEM_LIMIT)


def _mm(a, b):
    return jnp.dot(a.astype(BF16), b.astype(BF16), preferred_element_type=F32)


def _mm_nt(a, b):
    return lax.dot_general(a.astype(BF16), b.astype(BF16), (((1,), (1,)), ((), ())),
                           preferred_element_type=F32)


def _mm_tn(a, b):
    return lax.dot_general(a.astype(BF16), b.astype(BF16), (((0,), (0,)), ((), ())),
                           preferred_element_type=F32)


def _split3(x):
    hi = x.astype(BF16)
    r = x - hi.astype(F32)
    mid = r.astype(BF16)
    lo = (r - mid.astype(F32)).astype(BF16)
    return hi, mid, lo


def _dot_exact_rhs(x, e):
    hi, mid, lo = _split3(x)
    eb = e.astype(BF16)
    return (jnp.dot(hi, eb, preferred_element_type=F32) + jnp.dot(mid, eb, preferred_element_type=F32)
            + jnp.dot(lo, eb, preferred_element_type=F32))


def _dot_exact_lhs(e, x):
    hi, mid, lo = _split3(x)
    eb = e.astype(BF16)
    return (jnp.dot(eb, hi, preferred_element_type=F32) + jnp.dot(eb, mid, preferred_element_type=F32)
            + jnp.dot(eb, lo, preferred_element_type=F32))


def _dot_exact_lhs_nt(e, x):
    hi, mid, lo = _split3(x)
    eb = e.astype(BF16)
    dn = (((1,), (1,)), ((), ()))
    return (lax.dot_general(eb, hi, dn, preferred_element_type=F32)
            + lax.dot_general(eb, mid, dn, preferred_element_type=F32)
            + lax.dot_general(eb, lo, dn, preferred_element_type=F32))


def _sigmoid(x):
    return 1.0 / (1.0 + jnp.exp(-x))


def _silu(x):
    return x * _sigmoid(x)


def _softplus(x):
    return jnp.maximum(x, 0.0) + jnp.log(1.0 + jnp.exp(-jnp.abs(x)))


def _in_proj_kernel(x_ref, nw_ref, w_ref, *o_refs):
    x = x_ref[...]
    ms = jnp.mean(x * x, axis=-1, keepdims=True)
    xn = (x * lax.rsqrt(ms + EPS) * nw_ref[...]).astype(BF16)
    off = 0
    for o_ref in o_refs:
        wd = o_ref.shape[-1]
        o_ref[...] = jnp.dot(xn, w_ref[:, off:off + wd], preferred_element_type=F32).astype(o_ref.dtype)
        off += wd


def _in_proj(x2, norm_w, w_cat, widths, tm):
    t, d = x2.shape
    wtot = w_cat.shape[1]
    return pl.pallas_call(
        _in_proj_kernel,
        grid=(t // tm,),
        in_specs=[
            pl.BlockSpec((tm, d), lambda i: (i, 0)),
            pl.BlockSpec((1, d), lambda i: (0, 0)),
            pl.BlockSpec((d, wtot), lambda i: (0, 0)),
        ],
        out_specs=[pl.BlockSpec((tm, wd), lambda i: (i, 0)) for wd in widths],
        out_shape=[jax.ShapeDtypeStruct((t, wd), F32) for wd in widths],
        compiler_params=_cparams(("arbitrary",)),
        name="in_proj",
    )(x2, norm_w, w_cat)


def _gdn_kernel(qkv_ref, sm_ref, gate_ref, cw_ref, alog_ref, dtb_ref, onw_ref, ea_ref, eb_ref, ltri_ref,
                o_ref, xb, qn_s, kn_s, v_s, g_s, beta_s, u_s, w_s, aqk_s, egl_s, oacc_s, state_s):
    j = pl.program_id(1)
    ts = o_ref.shape[0]
    c = GDN_CHUNK
    hw = GDN_W

    @pl.when(j == 0)
    def _():
        xb[0:HALO, :] = jnp.zeros((HALO, 3 * hw), F32)
        state_s[...] = jnp.zeros_like(state_s)

    xb[HALO:HALO + ts, :] = qkv_ref[...]
    for part in range(3):
        cols = slice(part * hw, (part + 1) * hw)
        acc = cw_ref[GDN_CONV - 1:GDN_CONV, cols] * xb[HALO:HALO + ts, cols]
        for jj in range(GDN_CONV - 1):
            acc = acc + cw_ref[jj:jj + 1, cols] * xb[pl.ds(HALO - (GDN_CONV - 1) + jj, ts), cols]
        act = _silu(acc)
        if part == 2:
            v_s[...] = act
        else:
            dst = qn_s if part == 0 else kn_s
            scale = GDN_D ** -0.5 if part == 0 else 1.0
            for h in range(GDN_HEADS):
                hs = slice(h * GDN_D, (h + 1) * GDN_D)
                xh = act[:, hs]
                ss = jnp.sum(xh * xh, axis=-1, keepdims=True)
                dst[:, hs] = xh * (lax.rsqrt(ss + EPS) * scale)
    xb[0:HALO, :] = xb[ts:ts + HALO, :]

    sm = sm_ref[...]
    a_full = _dot_exact_rhs(sm, ea_ref[...])
    b_full = _dot_exact_rhs(sm, eb_ref[...])
    g_s[...] = -jnp.exp(alog_ref[...]) * _softplus(a_full + dtb_ref[...])
    beta_s[...] = _sigmoid(b_full)

    ri = lax.broadcasted_iota(jnp.int32, (c, c), 0)
    ci = lax.broadcasted_iota(jnp.int32, (c, c), 1)
    causal = ri >= ci
    strict = ri > ci
    blockdiag = (ri // GDN_SUB) == (ci // GDN_SUB)
    ltri = ltri_ref[...]
    heads = [slice(h * GDN_D, (h + 1) * GDN_D) for h in range(GDN_HEADS)]

    def precompute(ig, carry):
        base = pl.multiple_of(ig * (GDN_GROUP * c), GDN_GROUP * c)
        kb_l, kn_l, qn_l, rhs_l, decay_l, where_l = [], [], [], [], [], []
        for cc in range(GDN_GROUP):
            rows = pl.ds(base + cc * c, c)
            gc = _dot_exact_lhs(ltri, g_s[rows, :])
            glast = gc[c - 1:c, :]
            eg = jnp.exp(gc)
            beta = beta_s[rows, :]
            kn = kn_s[rows, :]
            qn = qn_s[rows, :]
            kb = kn * beta
            vb = v_s[rows, :] * beta
            kbe = kb * eg
            qn_s[rows, :] = qn * eg
            kn_s[rows, :] = kn * jnp.exp(glast - gc)
            egl_s[pl.ds(pl.multiple_of((ig * GDN_GROUP + cc) * HALO, HALO), HALO), :] = jnp.broadcast_to(
                jnp.exp(glast), (HALO, hw))
            for h, hs in enumerate(heads):
                gcol = gc[:, h * GDN_D:h * GDN_D + c]
                grow = gc[:, hs].T[0:1, 0:c]
                diff = gcol - grow
                decay_l.append(jnp.where(causal, jnp.exp(jnp.where(causal, diff, 0.0)), 0.0))
                kb_l.append(kb[:, hs].astype(BF16))
                kn_l.append(kn[:, hs].astype(BF16))
                qn_l.append(qn[:, hs].astype(BF16))
                rhs_l.append(jnp.concatenate([vb[:, hs], kbe[:, hs]], axis=-1))
                where_l.append((rows, h, hs))
        n = len(where_l)
        kk = [_mm_nt(kb_l[i], kn_l[i]) for i in range(n)]
        qk = [_mm_nt(qn_l[i], kn_l[i]) for i in range(n)]
        for i, (rows, h, hs) in enumerate(where_l):
            aqk_s[rows, h * c:(h + 1) * c] = qk[i] * decay_l[i]
        p = [-jnp.where(strict, kk[i] * decay_l[i], 0.0) for i in range(n)]
        pd = [jnp.where(blockdiag, x, 0.0) for x in p]
        pn = [p[i] - pd[i] for i in range(n)]
        p2 = [_mm(x, x) for x in pd]
        p4 = [_mm(x, x) for x in p2]
        p8 = [_mm(x, x) for x in p4]
        a1 = [pd[i] + p2[i] + _mm(pd[i], p2[i]) for i in range(n)]
        a2 = [a1[i] + p4[i] + _mm(a1[i], p4[i]) for i in range(n)]
        a3 = [a2[i] + p8[i] + _mm(a2[i], p8[i]) for i in range(n)]
        nm = [pn[i] + _mm(a3[i], pn[i]) for i in range(n)]
        n2 = [_mm(x, x) for x in nm]
        bm = [nm[i] + n2[i] + _mm(nm[i], n2[i]) for i in range(n)]
        tm = [bm[i] + a3[i] + _mm(bm[i], a3[i]) for i in range(n)]
        for i, (rows, h, hs) in enumerate(where_l):
            sol = rhs_l[i] + _mm(tm[i], rhs_l[i])
            u_s[rows, hs] = sol[:, :GDN_D]
            w_s[rows, hs] = sol[:, GDN_D:]
        return carry

    lax.fori_loop(0, ts // (GDN_GROUP * c), precompute, 0)

    def scan_body(ic, carry):
        rows = pl.ds(pl.multiple_of(ic * c, c), c)
        egl = egl_s[pl.ds(pl.multiple_of(ic * HALO, HALO), 1), :]
        st = [state_s[h] for h in range(GDN_HEADS)]
        stb = [x.astype(BF16) for x in st]
        ws = [_mm(w_s[rows, hs], stb[h]) for h, hs in enumerate(heads)]
        qs = [_mm(qn_s[rows, hs], stb[h]) for h, hs in enumerate(heads)]
        v_new = [u_s[rows, hs] - ws[h] for h, hs in enumerate(heads)]
        for h, hs in enumerate(heads):
            oacc_s[rows, hs] = qs[h] + _mm(aqk_s[rows, h * c:(h + 1) * c], v_new[h])
            state_s[h] = st[h] * egl[:, hs] + _mm_tn(kn_s[rows, hs], v_new[h])
        return carry

    lax.fori_loop(0, ts // c, scan_body, 0)

    for h in range(GDN_HEADS):
        hs = slice(h * GDN_D, (h + 1) * GDN_D)
        oh = oacc_s[:, hs]
        ms = jnp.mean(oh * oh, axis=-1, keepdims=True)
        o_ref[:, hs] = (oh * lax.rsqrt(ms + EPS) * onw_ref[...] * _silu(gate_ref[:, hs])).astype(o_ref.dtype)


def _gdn(qkv, small, gate, conv_w, a_log, dt_bias, out_norm_w, ts):
    b, s, _ = qkv.shape
    hw = GDN_W
    c = GDN_CHUNK
    rep = lambda v: jnp.repeat(v.astype(F32), GDN_D)[None, :]
    lane_head = np.arange(hw) // GDN_D
    ea = (np.arange(LANES)[:, None] == lane_head[None, :]).astype(np.float32)
    eb = (np.arange(LANES)[:, None] == (lane_head[None, :] + GDN_HEADS)).astype(np.float32)
    ltri = np.tril(np.ones((c, c), np.float32))
    full = lambda shape: pl.BlockSpec(shape, lambda ib, ij: (0,) * len(shape))
    return pl.pallas_call(
        _gdn_kernel,
        grid=(b, s // ts),
        in_specs=[
            pl.BlockSpec((None, ts, 3 * hw), lambda ib, ij: (ib, ij, 0)),
            pl.BlockSpec((None, ts, LANES), lambda ib, ij: (ib, ij, 0)),
            pl.BlockSpec((None, ts, hw), lambda ib, ij: (ib, ij, 0)),
            full((GDN_CONV, 3 * hw)), full((1, hw)), full((1, hw)), full((1, GDN_D)),
            full((LANES, hw)), full((LANES, hw)), full((c, c)),
        ],
        out_specs=pl.BlockSpec((None, ts, hw), lambda ib, ij: (ib, ij, 0)),
        out_shape=jax.ShapeDtypeStruct((b, s, hw), BF16),
        scratch_shapes=[
            pltpu.VMEM((ts + HALO, 3 * hw), F32),
            pltpu.VMEM((ts, hw), F32), pltpu.VMEM((ts, hw), F32), pltpu.VMEM((ts, hw), F32),
            pltpu.VMEM((ts, hw), F32), pltpu.VMEM((ts, hw), F32),
            pltpu.VMEM((ts, hw), F32), pltpu.VMEM((ts, hw), F32), pltpu.VMEM((ts, GDN_HEADS * c), F32),
            pltpu.VMEM((ts // c * HALO, hw), F32),
            pltpu.VMEM((ts, hw), F32),
            pltpu.VMEM((GDN_HEADS, GDN_D, GDN_D), F32),
        ],
        compiler_params=_cparams(("arbitrary", "arbitrary")),
        name="gdn",
    )(qkv, small, gate, conv_w.astype(F32), rep(a_log), rep(dt_bias), out_norm_w.astype(F32)[None, :],
      jnp.asarray(ea), jnp.asarray(eb), jnp.asarray(ltri))


def _rope_tables(s):
    half = NSA_ROPE_DIM // 2
    pos = jnp.arange(s, dtype=F32)
    inv = 1.0 / (ROPE_THETA ** (jnp.arange(0, NSA_ROPE_DIM, 2, dtype=F32) / NSA_ROPE_DIM))
    ang = pos[:, None] * inv[None, :]
    cos, sin = jnp.cos(ang), jnp.sin(ang)
    one = jnp.ones((s, NSA_D - NSA_ROPE_DIM), F32)
    zero = jnp.zeros((s, NSA_D - NSA_ROPE_DIM), F32)
    zh = jnp.zeros((s, half), F32)
    tc = jnp.concatenate([cos, cos, one], axis=-1)
    ta = jnp.concatenate([-sin, zh, zero], axis=-1)
    tb = jnp.concatenate([zh, sin, zero], axis=-1)
    dup = lambda t: jnp.concatenate([t, t], axis=-1)
    return dup(tc), dup(ta), dup(tb)


def _rope(x, tc, ta, tb):
    half = NSA_ROPE_DIM // 2
    return x * tc + pltpu.roll(x, LANES - half, 1) * ta + pltpu.roll(x, half, 1) * tb


def _group_ms(x, ones_blk):
    return _dot_exact_rhs(x * x, ones_blk) * (1.0 / NSA_D)


def _dup_groups(x):
    r = pltpu.roll(x, NSA_D, 1)
    lane = lax.broadcasted_iota(jnp.int32, x.shape, 1)
    lo = lane < NSA_D
    return jnp.where(lo, x, r), jnp.where(lo, r, x)


def _nsa_prep_kernel(nq_ref, kv_ref, sm_ref, tc_ref, ta_ref, tb_ref, qw_ref, ksw_ref, kww_ref, ones_ref,
                     q_ref, ks_ref, vs_ref, kw_ref, vw_ref, gs_ref, *, gate_col0):
    j = pl.program_id(1)
    ts = nq_ref.shape[0]
    tc, ta, tb = tc_ref[...], ta_ref[...], tb_ref[...]
    ones_blk = ones_ref[...]
    scale = NSA_D ** -0.5 * LOG2E
    for p in range(NSA_W // LANES):
        cols = slice(p * LANES, (p + 1) * LANES)
        x = nq_ref[:, cols]
        xn = x * lax.rsqrt(_group_ms(x, ones_blk) + EPS) * qw_ref[...]
        q_ref[:, cols] = (_rope(xn, tc, ta, tb) * scale).astype(q_ref.dtype)
    for src, nw_ref, k_out, v_out in ((0, ksw_ref, ks_ref, vs_ref), (2, kww_ref, kw_ref, vw_ref)):
        k = kv_ref[:, src * LANES:(src + 1) * LANES]
        v = kv_ref[:, (src + 1) * LANES:(src + 2) * LANES]
        kn = k * lax.rsqrt(_group_ms(k, ones_blk) + EPS) * nw_ref[...]
        kr = _rope(kn, tc, ta, tb)
        for ig, (kg, vg) in enumerate(zip(_dup_groups(kr), _dup_groups(v))):
            k_out[ig, :, 0:LANES] = kg.astype(k_out.dtype)
            v_out[ig] = vg.astype(v_out.dtype)
    pos = j * ts + lax.broadcasted_iota(jnp.int32, (ts, LANES), 0)
    lane = lax.broadcasted_iota(jnp.int32, (ts, LANES), 1)
    onehot = jnp.where(pos // NSA_SEL_BLOCK == lane, 1.0, 0.0).astype(ks_ref.dtype)
    sig = _sigmoid(sm_ref[...])
    for ig in range(NSA_GROUPS):
        ks_ref[ig, :, LANES:2 * LANES] = onehot
        gs_ref[ig] = pltpu.roll(sig, LANES - (gate_col0 + ig * NSA_REP * 3), 1)


def _nsa_prep(nq, kv4, small, tabs, q_norm_w, ks_norm_w, kw_norm_w, gate_col0, ts):
    b, s, _ = nq.shape
    g = NSA_GROUPS
    tile2 = lambda w: jnp.concatenate([w, w]).astype(F32)[None, :]
    ones_blk = np.kron(np.eye(2, dtype=np.float32), np.ones((NSA_D, NSA_D), np.float32))
    full = lambda shape: pl.BlockSpec(shape, lambda ib, ij: (0,) * len(shape))
    tok = lambda w: pl.BlockSpec((None, ts, w), lambda ib, ij: (ib, ij, 0))
    tab = pl.BlockSpec((ts, LANES), lambda ib, ij: (ij, 0))
    kv_out = lambda w: pl.BlockSpec((None, g, ts, w), lambda ib, ij: (ib, 0, ij, 0))
    kv_shape = lambda w, dt: jax.ShapeDtypeStruct((b, g, s, w), dt)
    return pl.pallas_call(
        functools.partial(_nsa_prep_kernel, gate_col0=gate_col0),
        grid=(b, s // ts),
        in_specs=[tok(NSA_W), tok(4 * LANES), tok(LANES), tab, tab, tab,
                  full((1, LANES)), full((1, LANES)), full((1, LANES)), full((LANES, LANES))],
        out_specs=[tok(NSA_W), kv_out(2 * LANES), kv_out(LANES), kv_out(LANES), kv_out(LANES), kv_out(LANES)],
        out_shape=[jax.ShapeDtypeStruct((b, s, NSA_W), BF16), kv_shape(2 * LANES, BF16), kv_shape(LANES, BF16),
                   kv_shape(LANES, BF16), kv_shape(LANES, BF16), kv_shape(LANES, F32)],
        compiler_params=_cparams(("arbitrary", "arbitrary")),
        name="nsa_prep",
    )(nq, kv4, small, *tabs, tile2(q_norm_w), tile2(ks_norm_w), tile2(kw_norm_w), jnp.asarray(ones_blk))


def _nsa_compress_kernel(kc_ref, vc_ref, tc_ref, ta_ref, tb_ref, pk_ref, pv_ref, kw1_ref, kw2_ref,
                         vw1_ref, vw2_ref, nw_ref, ones_ref, kc_out, vc_out):
    nrow, width = kc_ref.shape
    half = width
    outs = []
    for is_k in (True, False):
        x = (kc_ref if is_k else vc_ref)[...]
        if is_k:
            x = jnp.concatenate(
                [_rope(x[:, p * LANES:(p + 1) * LANES], tc_ref[:, p * LANES:(p + 1) * LANES],
                       ta_ref[:, p * LANES:(p + 1) * LANES], tb_ref[:, p * LANES:(p + 1) * LANES])
                 for p in range(width // LANES)], axis=-1)
        pos_ref, w1_ref, w2_ref = (pk_ref, kw1_ref, kw2_ref) if is_k else (pv_ref, vw1_ref, vw2_ref)
        first = _mm(x + pos_ref[0:1, :], w1_ref[0:half, :])
        second = _mm(x + pos_ref[1:2, :], w1_ref[half:2 * half, :])
        y = first + pltpu.roll(second, nrow - 1, 0)
        y = _mm(_silu(y), w2_ref[...])
        if is_k:
            y = y * lax.rsqrt(_group_ms(y, ones_ref[...]) + EPS) * nw_ref[...]
        outs.append(y)
    for y, out in zip(outs, (kc_out, vc_out)):
        y0, y1 = _dup_groups(y)
        out[0] = y0.astype(out.dtype)
        out[1] = y1.astype(out.dtype)


def _nsa_compress(kc, vc, tabs, pos_k, pos_v, k_w1, k_w2, v_w1, v_w2, kc_norm_w):
    b, s, _ = kc.shape
    g = NSA_GROUPS
    st = NSA_CMP_STRIDE
    nrow = s // st
    width = st * LANES
    flat = lambda t: t.reshape(t.shape[0], nrow, width)
    eye_g = jnp.eye(g, dtype=F32)

    def w1_blk(w1):
        wl = w1.reshape(NSA_CMP_BLOCK, NSA_D, NSA_D)
        return jnp.einsum("lde,gh->lgdhe", wl, eye_g).reshape(NSA_CMP_BLOCK * LANES, LANES).astype(BF16)

    def w2_blk(w2):
        return jnp.einsum("de,gh->gdhe", w2, eye_g).reshape(LANES, LANES).astype(BF16)

    def pos_rows(p):
        return jnp.concatenate([p, p], axis=-1).reshape(2, width).astype(F32)

    ones_blk = np.kron(np.eye(2, dtype=np.float32), np.ones((NSA_D, NSA_D), np.float32))
    full = lambda shape: pl.BlockSpec(shape, lambda ib: (0,) * len(shape))
    seq = pl.BlockSpec((None, nrow, width), lambda ib: (ib, 0, 0))
    out = pl.BlockSpec((None, g, nrow, LANES), lambda ib: (ib, 0, 0, 0))
    oshape = jax.ShapeDtypeStruct((b, g, nrow, LANES), BF16)
    tabs_r = [t.reshape(nrow, width) for t in tabs]
    return pl.pallas_call(
        _nsa_compress_kernel,
        grid=(b,),
        in_specs=[seq, seq, full((nrow, width)), full((nrow, width)), full((nrow, width)),
                  full((2, width)), full((2, width)),
                  full((2 * width, LANES)), full((LANES, LANES)), full((2 * width, LANES)), full((LANES, LANES)),
                  full((1, LANES)), full((LANES, LANES))],
        out_specs=[out, out],
        out_shape=[oshape, oshape],
        compiler_params=_cparams(("arbitrary",)),
        name="nsa_compress",
    )(flat(kc), flat(vc), *tabs_r, pos_rows(pos_k), pos_rows(pos_v), w1_blk(k_w1), w2_blk(k_w2),
      w1_blk(v_w1), w2_blk(v_w2), jnp.concatenate([kc_norm_w, kc_norm_w]).astype(F32)[None, :],
      jnp.asarray(ones_blk))


def _nsa_attn_kernel(q_ref, gs_ref, kc_ref, vc_ref, ks_ref, vs_ref, kw_ref, vw_ref, ov_ref, tri_ref, wb_ref,
                     o_ref, *, kb, wlen):
    i = pl.program_id(2)
    tq = NSA_Q_BLOCK
    rep = NSA_REP
    rows = rep * tq
    s0 = i * tq
    ncmp = kc_ref.shape[0]
    nblk = vs_ref.shape[0] // NSA_SEL_BLOCK

    q = q_ref[...]
    lane_q = lax.broadcasted_iota(jnp.int32, (tq, LANES), 1)
    pieces = []
    for r in range(rep):
        tile = q[:, (r // 2) * LANES:(r // 2 + 1) * LANES]
        keep = (lane_q < NSA_D) if r % 2 == 0 else (lane_q >= NSA_D)
        pieces.append(jnp.where(keep, tile, jnp.zeros_like(tile)))
    qs = jnp.concatenate(pieces, axis=0)

    sc = _mm_nt(qs, kc_ref[...])
    cend = (lax.broadcasted_iota(jnp.int32, (rows, ncmp), 1) * NSA_CMP_STRIDE + (NSA_CMP_BLOCK - 1)) - s0
    cmask = cend <= (lax.broadcasted_iota(jnp.int32, (rows, ncmp), 0) & (tq - 1))
    sc = jnp.where(cmask, sc, NEG)
    m = jnp.max(sc, axis=-1, keepdims=True)
    p = jnp.where(cmask, jnp.exp2(sc - m), 0.0)
    l = jnp.sum(p, axis=-1, keepdims=True)
    p = p * (1.0 / jnp.maximum(l, 1e-30))
    o_cmp = _mm(p, vc_ref[...])

    psum = p[0:tq]
    for r in range(1, rep):
        psum = psum + p[r * tq:(r + 1) * tq]
    imp = _dot_exact_rhs(psum, ov_ref[...])
    blk = lane_q
    valid = blk <= i

    def ranked():
        forced = (blk == 0) | (blk == i) | (blk == i - 1)
        key = jnp.where(valid, jnp.where(forced, 0x7F000000, pltpu.bitcast(imp, jnp.int32)), -1)
        key = jnp.where(blk < nblk, key, -2)
        key_m1 = key - 1
        rank = jnp.zeros((tq, LANES), jnp.int32)
        for jb in range(nblk):
            ahead = key[:, jb:jb + 1] > jnp.where(blk > jb, key_m1, key)
            rank = rank + jnp.where(ahead, 1, 0)
        return jnp.where(rank < NSA_N_SEL, 1.0, 0.0)

    sel = lax.cond(i >= NSA_N_SEL, ranked, lambda: jnp.where(valid, 1.0, 0.0))
    selneg = jnp.where((sel > 0.5) & (blk < i), 0.0, NEG).astype(BF16)
    q2 = jnp.concatenate([qs, jnp.concatenate([selneg] * rep, axis=0)], axis=1)

    sd = _mm_nt(qs, ks_ref[pl.ds(pl.multiple_of(s0, tq), tq), 0:LANES]) + tri_ref[...]
    m0 = jnp.max(sd, axis=-1, keepdims=True)
    pd = jnp.exp2(sd - m0)
    l0 = jnp.sum(pd, axis=-1, keepdims=True)
    a0 = _mm(pd, vs_ref[pl.ds(pl.multiple_of(s0, tq), tq), :])

    def slc_body(ic, carry):
        m_i, l_i, acc = carry
        k0 = pl.multiple_of(ic * kb, kb)
        s = _mm_nt(q2, ks_ref[pl.ds(k0, kb), :])
        m_new = jnp.maximum(m_i, jnp.max(s, axis=-1, keepdims=True))
        alpha = jnp.exp2(m_i - m_new)
        pr = jnp.exp2(s - m_new)
        l_new = alpha * l_i + jnp.sum(pr, axis=-1, keepdims=True)
        acc_new = alpha * acc + _mm(pr, vs_ref[pl.ds(k0, kb), :])
        return m_new, l_new, acc_new

    _, l_s, acc_s = lax.fori_loop(0, (s0 + kb - 1) // kb, slc_body, (m0, l0, a0))

    wstart = pl.multiple_of(s0, tq)
    col = lax.broadcasted_iota(jnp.int32, (1, wlen), 1)
    before_start = jnp.where(col >= (wlen - tq) - s0, 0.0, NEG)
    sw = _mm_nt(qs, kw_ref[pl.ds(wstart, wlen), :]) + wb_ref[...] + before_start
    mw = jnp.max(sw, axis=-1, keepdims=True)
    pw = jnp.exp2(sw - mw)
    lw = jnp.sum(pw, axis=-1, keepdims=True)
    acc_w = _mm(pw, vw_ref[pl.ds(wstart, wlen), :])

    gs = gs_ref[...]
    gcol = lambda x: jnp.concatenate([gs[:, r * 3 + x:r * 3 + x + 1] for r in range(rep)], axis=0)
    o = gcol(0) * o_cmp + (gcol(1) / l_s) * acc_s + (gcol(2) / lw) * acc_w
    lo = lane_q < NSA_D
    for pr_ in range(rep // 2):
        even = o[(2 * pr_) * tq:(2 * pr_ + 1) * tq]
        odd = o[(2 * pr_ + 1) * tq:(2 * pr_ + 2) * tq]
        o_ref[:, pr_ * LANES:(pr_ + 1) * LANES] = jnp.where(lo, even, odd).astype(o_ref.dtype)


def _nsa_attn(q, gsig, kcmp, vcmp, ks3, vs2, kw2, vw2):
    b, s, _ = q.shape
    g = NSA_GROUPS
    rep = NSA_REP
    tq = NSA_Q_BLOCK
    rows = rep * tq
    ncmp = kcmp.shape[2]
    nblk = s // NSA_SEL_BLOCK
    kb = min(512, s)
    wlen = NSA_WINDOW + 2 * tq
    wpad = wlen - tq
    ci = np.arange(ncmp) * NSA_CMP_STRIDE
    sj = np.arange(nblk) * NSA_SEL_BLOCK
    ov = np.clip(np.minimum(ci[:, None] + NSA_CMP_BLOCK, sj[None, :] + NSA_SEL_BLOCK)
                 - np.maximum(ci[:, None], sj[None, :]), 0, None).astype(np.float32) / NSA_CMP_STRIDE
    ov_p = np.zeros((ncmp, LANES), np.float32)
    ov_p[:, :nblk] = ov
    tloc = (np.arange(rows) % tq)[:, None]
    tri = np.where(np.arange(tq)[None, :] <= tloc, 0.0, NEG).astype(np.float32)
    wcol = np.arange(wlen)[None, :]
    wb = np.where((wcol <= wpad + tloc) & (wcol > wpad - NSA_WINDOW + tloc), 0.0, NEG).astype(np.float32)
    pad = lambda a: jnp.pad(a, ((0, 0), (0, 0), (wpad, 0), (0, 0)))
    seq = lambda n, w: pl.BlockSpec((None, None, n, w), lambda ib, ig, ii: (ib, ig, 0, 0))
    full = lambda shape: pl.BlockSpec(shape, lambda ib, ig, ii: (0,) * len(shape))
    return pl.pallas_call(
        functools.partial(_nsa_attn_kernel, kb=kb, wlen=wlen),
        grid=(b, g, s // tq),
        in_specs=[
            pl.BlockSpec((None, tq, rep * NSA_D), lambda ib, ig, ii: (ib, ii, ig)),
            pl.BlockSpec((None, None, tq, LANES), lambda ib, ig, ii: (ib, ig, ii, 0)),
            seq(ncmp, LANES), seq(ncmp, LANES), seq(s, 2 * LANES), seq(s, LANES),
            seq(s + wpad, LANES), seq(s + wpad, LANES),
            full((ncmp, LANES)), full((rows, tq)), full((rows, wlen)),
        ],
        out_specs=pl.BlockSpec((None, tq, rep * NSA_D), lambda ib, ig, ii: (ib, ii, ig)),
        out_shape=jax.ShapeDtypeStruct((b, s, NSA_W), BF16),
        compiler_params=_cparams(("arbitrary", "arbitrary", "arbitrary")),
        name="nsa_attn",
    )(q, gsig, kcmp, vcmp, ks3, vs2, pad(kw2), pad(vw2), jnp.asarray(ov_p), jnp.asarray(tri), jnp.asarray(wb))


def _mem_kv_kernel(mem_ref, nw_ref, w_ref, knw_ref, k_ref, v_ref):
    x = mem_ref[...]
    ms = jnp.mean(x * x, axis=-1, keepdims=True)
    xn = x * lax.rsqrt(ms + EPS) * nw_ref[...]
    kv = _mm(xn, w_ref[...])
    v_ref[...] = kv[:, MEM_W:].astype(v_ref.dtype)
    for h in range(MEM_HEADS):
        hs = slice(h * MEM_D, (h + 1) * MEM_D)
        kh = kv[:, hs]
        msk = jnp.mean(kh * kh, axis=-1, keepdims=True)
        k_ref[:, hs] = (kh * lax.rsqrt(msk + EPS) * knw_ref[...]).astype(k_ref.dtype)


def _mem_kv(mem, mem_norm_w, w_kv, k_norm_w):
    b, m, d = mem.shape
    full = lambda shape: pl.BlockSpec(shape, lambda ib: (0,) * len(shape))
    blk = lambda w: pl.BlockSpec((None, m, w), lambda ib: (ib, 0, 0))
    return pl.pallas_call(
        _mem_kv_kernel,
        grid=(b,),
        in_specs=[blk(d), full((1, d)), full((d, 2 * MEM_W)), full((1, MEM_D))],
        out_specs=[blk(MEM_W), blk(MEM_W)],
        out_shape=[jax.ShapeDtypeStruct((b, m, MEM_W), BF16)] * 2,
        compiler_params=_cparams(("arbitrary",)),
        name="mem_kv",
    )(mem, mem_norm_w.astype(F32)[None, :], w_kv.astype(BF16), k_norm_w.astype(F32)[None, :])


def _mem_attn_kernel(q_ref, k_ref, v_ref, qnw_ref, o_ref):
    scale = MEM_D ** -0.5
    for h in range(MEM_HEADS):
        hs = slice(h * MEM_D, (h + 1) * MEM_D)
        qh = q_ref[:, hs]
        ms = jnp.mean(qh * qh, axis=-1, keepdims=True)
        qn = qh * lax.rsqrt(ms + EPS) * qnw_ref[...]
        s = _mm_nt(qn, k_ref[:, hs]) * scale
        m = jnp.max(s, axis=-1, keepdims=True)
        p = jnp.exp(s - m)
        l = jnp.sum(p, axis=-1, keepdims=True)
        o_ref[:, hs] = (_mm(p, v_ref[:, hs]) * (1.0 / l)).astype(o_ref.dtype)


def _mem_attn(mq, k, v, q_norm_w, ts):
    b, s, _ = mq.shape
    m = k.shape[1]
    return pl.pallas_call(
        _mem_attn_kernel,
        grid=(b, s // ts),
        in_specs=[
            pl.BlockSpec((None, ts, MEM_W), lambda ib, ij: (ib, ij, 0)),
            pl.BlockSpec((None, m, MEM_W), lambda ib, ij: (ib, 0, 0)),
            pl.BlockSpec((None, m, MEM_W), lambda ib, ij: (ib, 0, 0)),
            pl.BlockSpec((1, MEM_D), lambda ib, ij: (0, 0)),
        ],
        out_specs=pl.BlockSpec((None, ts, MEM_W), lambda ib, ij: (ib, ij, 0)),
        out_shape=jax.ShapeDtypeStruct((b, s, MEM_W), BF16),
        compiler_params=_cparams(("arbitrary", "arbitrary")),
        name="mem_attn",
    )(mq, k, v, q_norm_w.astype(F32)[None, :])


def _out_proj_kernel(x_ref, oa_ref, ob_ref, oc_ref, w_ref, h_ref):
    acc = x_ref[...]
    off = 0
    for o_ref in (oa_ref, ob_ref, oc_ref):
        wd = o_ref.shape[-1]
        acc = acc + jnp.dot(o_ref[...].astype(BF16), w_ref[off:off + wd, :], preferred_element_type=F32)
        off += wd
    h_ref[...] = acc


def _out_proj(x2, oa, ob, oc, w_out, tm):
    t, d = x2.shape
    row = lambda w: pl.BlockSpec((tm, w), lambda i: (i, 0))
    return pl.pallas_call(
        _out_proj_kernel,
        grid=(t // tm,),
        in_specs=[row(d), row(oa.shape[1]), row(ob.shape[1]), row(oc.shape[1]),
                  pl.BlockSpec(w_out.shape, lambda i: (0, 0))],
        out_specs=row(d),
        out_shape=jax.ShapeDtypeStruct((t, d), F32),
        compiler_params=_cparams(("arbitrary",)),
        name="out_proj",
    )(x2, oa, ob, oc, w_out)


def _ffn_kernel(h_ref, halo_ref, nw_ref, wup_ref, cw_ref, wdn_ref, o_ref, hn_s, u_s, acc_s, *, fc):
    j = pl.program_id(1)
    ts = h_ref.shape[0]
    f = wdn_ref.shape[0]

    def norm(x):
        ms = jnp.mean(x * x, axis=-1, keepdims=True)
        return (x * lax.rsqrt(ms + EPS) * nw_ref[...]).astype(BF16)

    halo = jnp.where(j > 0, halo_ref[...], 0.0)
    hn_s[0:HALO, :] = norm(halo).astype(hn_s.dtype)
    hn_s[HALO:HALO + ts, :] = norm(h_ref[...]).astype(hn_s.dtype)
    acc_s[...] = h_ref[...]
    for ic in range(f // fc):
        hn = hn_s[...].astype(BF16)
        for part in range(2):
            cols = slice(part * f + ic * fc, part * f + (ic + 1) * fc)
            u_s[part] = jnp.dot(hn, wup_ref[:, cols], preferred_element_type=F32)
        conv = []
        for part in range(2):
            cols = slice(part * f + ic * fc, part * f + (ic + 1) * fc)
            acc = cw_ref[FFN_CONV - 1:FFN_CONV, cols] * u_s[part, HALO:HALO + ts, :]
            for jj in range(FFN_CONV - 1):
                acc = acc + cw_ref[jj:jj + 1, cols] * u_s[part, pl.ds(HALO - (FFN_CONV - 1) + jj, ts), :]
            conv.append(acc)
        act = (_silu(conv[0]) * conv[1]).astype(BF16)
        acc_s[...] += jnp.dot(act, wdn_ref[ic * fc:(ic + 1) * fc, :], preferred_element_type=F32)
    o_ref[...] = acc_s[...]


def _ffn(h, norm_w, w_up, conv_w, w_down, ts, fc):
    b, s, d = h.shape
    f = w_down.shape[0]
    full = lambda shape: pl.BlockSpec(shape, lambda ib, ij: (0,) * len(shape))
    return pl.pallas_call(
        functools.partial(_ffn_kernel, fc=fc),
        grid=(b, s // ts),
        in_specs=[
            pl.BlockSpec((None, ts, d), lambda ib, ij: (ib, ij, 0)),
            pl.BlockSpec((None, HALO, d), lambda ib, ij: (ib, jnp.maximum(ij * (ts // HALO) - 1, 0), 0)),
            full((1, d)), full((d, 2 * f)), full((FFN_CONV, 2 * f)), full((f, d)),
        ],
        out_specs=pl.BlockSpec((None, ts, d), lambda ib, ij: (ib, ij, 0)),
        out_shape=jax.ShapeDtypeStruct((b, s, d), F32),
        scratch_shapes=[
            pltpu.VMEM((ts + HALO, d), F32),
            pltpu.VMEM((2, ts + HALO, fc), F32),
            pltpu.VMEM((ts, d), F32),
        ],
        compiler_params=_cparams(("arbitrary", "arbitrary")),
        name="ffn",
    )(h, h, norm_w.astype(F32)[None, :], w_up, conv_w.astype(F32), w_down)


def _split_w_in(w_in):
    sizes = (3 * GDN_W, GDN_HEADS, GDN_HEADS, GDN_W, NSA_W, NSA_KV_W, NSA_KV_W, NSA_KV_W, NSA_KV_W,
             NSA_KV_W, NSA_KV_W, 3 * NSA_HEADS, MEM_W)
    offs = np.concatenate([[0], np.cumsum(sizes)])
    (qkv, a, bb, gate, nq, kc, vc, ks, vs, kw, vw, ng, mq) = [w_in[:, offs[i]:offs[i + 1]] for i in range(len(sizes))]
    n_small = 2 * GDN_HEADS + 3 * NSA_HEADS
    small = jnp.concatenate([a, bb, ng, jnp.zeros((w_in.shape[0], LANES - n_small), w_in.dtype)], axis=1)
    widths = (3 * GDN_W, GDN_W, NSA_W, NSA_KV_W, NSA_KV_W, 4 * NSA_KV_W, MEM_W, LANES)
    w_cat = jnp.concatenate([qkv, gate, nq, kc, vc, ks, vs, kw, vw, mq, small], axis=1).astype(BF16)
    return w_cat, widths


def _layer(x, mem, attn_norm_w, mem_norm_w, w_in, gdn_conv_w, gdn_a_log, gdn_dt_bias, gdn_out_norm_w,
           nsa_q_norm_w, nsa_kc_norm_w, nsa_ks_norm_w, nsa_kw_norm_w, nsa_cmp_pos_k, nsa_cmp_pos_v,
           nsa_cmp_k_w1, nsa_cmp_k_w2, nsa_cmp_v_w1, nsa_cmp_v_w2, mem_w_kv, mem_q_norm_w, mem_k_norm_w,
           w_out, ffn_norm_w, ffn_w_up, ffn_conv_w, ffn_w_down):
    b, s, d = x.shape
    t = b * s
    ts = min(512, s)
    x2 = x.reshape(t, d)

    w_cat, widths = _split_w_in(w_in)
    qkv, gate, nq, kc, vc, kv4, mq, small = _in_proj(x2, attn_norm_w.astype(F32)[None, :], w_cat, widths, ts)
    r3 = lambda a: a.reshape(b, s, a.shape[-1])

    o_a = _gdn(r3(qkv), r3(small), r3(gate), gdn_conv_w, gdn_a_log, gdn_dt_bias, gdn_out_norm_w, ts)

    tabs = _rope_tables(s)
    q_r, ks3, vs2, kw2, vw2, gsig = _nsa_prep(r3(nq), r3(kv4), r3(small), tabs, nsa_q_norm_w, nsa_ks_norm_w,
                                              nsa_kw_norm_w, 2 * GDN_HEADS, ts)
    kcmp, vcmp = _nsa_compress(r3(kc), r3(vc), tabs, nsa_cmp_pos_k, nsa_cmp_pos_v, nsa_cmp_k_w1, nsa_cmp_k_w2,
                               nsa_cmp_v_w1, nsa_cmp_v_w2, nsa_kc_norm_w)
    o_b = _nsa_attn(q_r, gsig, kcmp, vcmp, ks3, vs2, kw2, vw2)

    mk, mv = _mem_kv(mem, mem_norm_w, mem_w_kv, mem_k_norm_w)
    o_c = _mem_attn(r3(mq), mk, mv, mem_q_norm_w, ts)

    h = _out_proj(x2, o_a.reshape(t, GDN_W), o_b.reshape(t, NSA_W), o_c.reshape(t, MEM_W), w_out.astype(BF16), ts)
    out = _ffn(h.reshape(b, s, d), ffn_norm_w, ffn_w_up.astype(BF16), ffn_conv_w, ffn_w_down.astype(BF16), ts, 256)
    return out


def kernel(x, mem, attn_norm_w, mem_norm_w, w_in, gdn_conv_w, gdn_a_log, gdn_dt_bias, gdn_out_norm_w, nsa_q_norm_w, nsa_kc_norm_w, nsa_ks_norm_w, nsa_kw_norm_w, nsa_cmp_pos_k, nsa_cmp_pos_v, nsa_cmp_k_w1, nsa_cmp_k_w2, nsa_cmp_v_w1, nsa_cmp_v_w2, mem_w_kv, mem_q_norm_w, mem_k_norm_w, w_out, ffn_norm_w, ffn_w_up, ffn_conv_w, ffn_w_down):
    h = x
    for l in range(w_in.shape[0]):
        h = _layer(h, mem, attn_norm_w[l], mem_norm_w[l], w_in[l], gdn_conv_w[l], gdn_a_log[l], gdn_dt_bias[l],
                   gdn_out_norm_w[l], nsa_q_norm_w[l], nsa_kc_norm_w[l], nsa_ks_norm_w[l], nsa_kw_norm_w[l],
                   nsa_cmp_pos_k[l], nsa_cmp_pos_v[l], nsa_cmp_k_w1[l], nsa_cmp_k_w2[l], nsa_cmp_v_w1[l],
                   nsa_cmp_v_w2[l], mem_w_kv[l], mem_q_norm_w[l], mem_k_norm_w[l], w_out[l], ffn_norm_w[l],
                   ffn_w_up[l], ffn_conv_w[l], ffn_w_down[l])
    return h
```

```python
import functools

import jax
import jax.numpy as jnp
import numpy as np
from jax import lax
from jax.experimental import pallas as pl
from jax.experimental.pallas import tpu as pltpu

F32 = jnp.float32
BF16 = jnp.bfloat16

EPS = 1e-6
ROPE_THETA = 500000.0
GDN_HEADS = 4
GDN_D = 128
GDN_CONV = 4
GDN_CHUNK = 64
GDN_SUB = 16
GDN_GROUP = 4
NSA_HEADS = 8
NSA_GROUPS = 2
NSA_REP = NSA_HEADS // NSA_GROUPS
NSA_D = 64
NSA_CMP_BLOCK = 32
NSA_CMP_STRIDE = 16
NSA_SEL_BLOCK = 64
NSA_N_SEL = 16
NSA_WINDOW = 512
NSA_Q_BLOCK = 64
NSA_ROPE_DIM = NSA_D // 4
MEM_HEADS = 4
MEM_D = 128
FFN_CONV = 3

GDN_W = GDN_HEADS * GDN_D
NSA_W = NSA_HEADS * NSA_D
MEM_W = MEM_HEADS * MEM_D
NSA_KV_W = NSA_GROUPS * NSA_D

LANES = 128
HALO = 8
HALO_BF16 = 16
VMEM_LIMIT = 56 * 1024 * 1024
NEG = -1e30
LOG2E = 1.4426950408889634


def _cparams(sem):
    return pltpu.CompilerParams(dimension_semantics=sem, vmem_limit_bytes=VMEM_LIMIT)


def _mm(a, b):
    return jnp.dot(a.astype(BF16), b.astype(BF16), preferred_element_type=F32)


def _mm_nt(a, b):
    return lax.dot_general(a.astype(BF16), b.astype(BF16), (((1,), (1,)), ((), ())),
                           preferred_element_type=F32)


def _mm_tn(a, b):
    return lax.dot_general(a.astype(BF16), b.astype(BF16), (((0,), (0,)), ((), ())),
                           preferred_element_type=F32)


def _split3(x):
    hi = x.astype(BF16)
    r = x - hi.astype(F32)
    mid = r.astype(BF16)
    lo = (r - mid.astype(F32)).astype(BF16)
    return hi, mid, lo


def _dot_exact_rhs(x, e):
    hi, mid, lo = _split3(x)
    eb = e.astype(BF16)
    return (jnp.dot(hi, eb, preferred_element_type=F32) + jnp.dot(mid, eb, preferred_element_type=F32)
            + jnp.dot(lo, eb, preferred_element_type=F32))


def _dot_exact_lhs(e, x):
    hi, mid, lo = _split3(x)
    eb = e.astype(BF16)
    return (jnp.dot(eb, hi, preferred_element_type=F32) + jnp.dot(eb, mid, preferred_element_type=F32)
            + jnp.dot(eb, lo, preferred_element_type=F32))


def _dot_exact_lhs_nt(e, x):
    hi, mid, lo = _split3(x)
    eb = e.astype(BF16)
    dn = (((1,), (1,)), ((), ()))
    return (lax.dot_general(eb, hi, dn, preferred_element_type=F32)
            + lax.dot_general(eb, mid, dn, preferred_element_type=F32)
            + lax.dot_general(eb, lo, dn, preferred_element_type=F32))


def _sigmoid(x):
    return 1.0 / (1.0 + jnp.exp(-x))


def _silu(x):
    return x * _sigmoid(x)


def _softplus(x):
    return jnp.maximum(x, 0.0) + jnp.log(1.0 + jnp.exp(-jnp.abs(x)))


def _in_proj_kernel(x_ref, nw_ref, w_ref, *o_refs):
    x = x_ref[...]
    ms = jnp.mean(x * x, axis=-1, keepdims=True)
    xn = (x * lax.rsqrt(ms + EPS) * nw_ref[...]).astype(BF16)
    off = 0
    for o_ref in o_refs:
        wd = o_ref.shape[-1]
        o_ref[...] = jnp.dot(xn, w_ref[:, off:off + wd], preferred_element_type=F32).astype(o_ref.dtype)
        off += wd


def _in_proj(x2, norm_w, w_cat, widths, tm):
    t, d = x2.shape
    wtot = w_cat.shape[1]
    return pl.pallas_call(
        _in_proj_kernel,
        grid=(t // tm,),
        in_specs=[
            pl.BlockSpec((tm, d), lambda i: (i, 0)),
            pl.BlockSpec((1, d), lambda i: (0, 0)),
            pl.BlockSpec((d, wtot), lambda i: (0, 0)),
        ],
        out_specs=[pl.BlockSpec((tm, wd), lambda i: (i, 0)) for wd in widths],
        out_shape=[jax.ShapeDtypeStruct((t, wd), F32) for wd in widths],
        compiler_params=_cparams(("arbitrary",)),
        name="in_proj",
    )(x2, norm_w, w_cat)


def _gdn_kernel(qkv_ref, sm_ref, gate_ref, cw_ref, alog_ref, dtb_ref, onw_ref, ea_ref, eb_ref, ltri_ref,
                o_ref, xb, qn_s, kn_s, v_s, g_s, beta_s, u_s, w_s, aqk_s, egl_s, oacc_s, state_s):
    j = pl.program_id(1)
    ts = o_ref.shape[0]
    c = GDN_CHUNK
    hw = GDN_W

    @pl.when(j == 0)
    def _():
        xb[0:HALO, :] = jnp.zeros((HALO, 3 * hw), F32)
        state_s[...] = jnp.zeros_like(state_s)

    xb[HALO:HALO + ts, :] = qkv_ref[...]
    for part in range(3):
        cols = slice(part * hw, (part + 1) * hw)
        acc = cw_ref[GDN_CONV - 1:GDN_CONV, cols] * xb[HALO:HALO + ts, cols]
        for jj in range(GDN_CONV - 1):
            acc = acc + cw_ref[jj:jj + 1, cols] * xb[pl.ds(HALO - (GDN_CONV - 1) + jj, ts), cols]
        act = _silu(acc)
        if part == 2:
            v_s[...] = act
        else:
            dst = qn_s if part == 0 else kn_s
            scale = GDN_D ** -0.5 if part == 0 else 1.0
            for h in range(GDN_HEADS):
                hs = slice(h * GDN_D, (h + 1) * GDN_D)
                xh = act[:, hs]
                ss = jnp.sum(xh * xh, axis=-1, keepdims=True)
                dst[:, hs] = xh * (lax.rsqrt(ss + EPS) * scale)
    xb[0:HALO, :] = xb[ts:ts + HALO, :]

    sm = sm_ref[...]
    a_full = _dot_exact_rhs(sm, ea_ref[...])
    b_full = _dot_exact_rhs(sm, eb_ref[...])
    g_s[...] = -jnp.exp(alog_ref[...]) * _softplus(a_full + dtb_ref[...])
    beta_s[...] = _sigmoid(b_full)

    ri = lax.broadcasted_iota(jnp.int32, (c, c), 0)
    ci = lax.broadcasted_iota(jnp.int32, (c, c), 1)
    causal = ri >= ci
    strict = ri > ci
    blockdiag = (ri // GDN_SUB) == (ci // GDN_SUB)
    ltri = ltri_ref[...]
    heads = [slice(h * GDN_D, (h + 1) * GDN_D) for h in range(GDN_HEADS)]

    def precompute(ig, carry):
        base = pl.multiple_of(ig * (GDN_GROUP * c), GDN_GROUP * c)
        kb_l, kn_l, qn_l, rhs_l, decay_l, where_l = [], [], [], [], [], []
        for cc in range(GDN_GROUP):
            rows = pl.ds(base + cc * c, c)
            gc = _dot_exact_lhs(ltri, g_s[rows, :])
            glast = gc[c - 1:c, :]
            eg = jnp.exp(gc)
            beta = beta_s[rows, :]
            kn = kn_s[rows, :]
            qn = qn_s[rows, :]
            kb = kn * beta
            vb = v_s[rows, :] * beta
            kbe = kb * eg
            qn_s[rows, :] = qn * eg
            kn_s[rows, :] = kn * jnp.exp(glast - gc)
            egl_s[pl.ds(pl.multiple_of((ig * GDN_GROUP + cc) * HALO, HALO), HALO), :] = jnp.broadcast_to(
                jnp.exp(glast), (HALO, hw))
            for h, hs in enumerate(heads):
                gcol = gc[:, h * GDN_D:h * GDN_D + c]
                grow = gc[:, hs].T[0:1, 0:c]
                diff = gcol - grow
                decay_l.append(jnp.where(causal, jnp.exp(jnp.where(causal, diff, 0.0)), 0.0))
                kb_l.append(kb[:, hs].astype(BF16))
                kn_l.append(kn[:, hs].astype(BF16))
                qn_l.append(qn[:, hs].astype(BF16))
                rhs_l.append(jnp.concatenate([vb[:, hs], kbe[:, hs]], axis=-1))
                where_l.append((rows, h, hs))
        n = len(where_l)
        kk = [_mm_nt(kb_l[i], kn_l[i]) for i in range(n)]
        qk = [_mm_nt(qn_l[i], kn_l[i]) for i in range(n)]
        for i, (rows, h, hs) in enumerate(where_l):
            aqk_s[rows, h * c:(h + 1) * c] = qk[i] * decay_l[i]
        p = [-jnp.where(strict, kk[i] * decay_l[i], 0.0) for i in range(n)]
        pd = [jnp.where(blockdiag, x, 0.0) for x in p]
        pn = [p[i] - pd[i] for i in range(n)]
        p2 = [_mm(x, x) for x in pd]
        p4 = [_mm(x, x) for x in p2]
        p8 = [_mm(x, x) for x in p4]
        a1 = [pd[i] + p2[i] + _mm(pd[i], p2[i]) for i in range(n)]
        a2 = [a1[i] + p4[i] + _mm(a1[i], p4[i]) for i in range(n)]
        a3 = [a2[i] + p8[i] + _mm(a2[i], p8[i]) for i in range(n)]
        nm = [pn[i] + _mm(a3[i], pn[i]) for i in range(n)]
        n2 = [_mm(x, x) for x in nm]
        bm = [nm[i] + n2[i] + _mm(nm[i], n2[i]) for i in range(n)]
        tm = [bm[i] + a3[i] + _mm(bm[i], a3[i]) for i in range(n)]
        for i, (rows, h, hs) in enumerate(where_l):
            sol = rhs_l[i] + _mm(tm[i], rhs_l[i])
            u_s[rows, hs] = sol[:, :GDN_D]
            w_s[rows, hs] = sol[:, GDN_D:]
        return carry

    lax.fori_loop(0, ts // (GDN_GROUP * c), precompute, 0)

    def scan_body(ic, carry):
        rows = pl.ds(pl.multiple_of(ic * c, c), c)
        egl = egl_s[pl.ds(pl.multiple_of(ic * HALO, HALO), 1), :]
        st = [state_s[h] for h in range(GDN_HEADS)]
        stb = [x.astype(BF16) for x in st]
        ws = [_mm(w_s[rows, hs], stb[h]) for h, hs in enumerate(heads)]
        qs = [_mm(qn_s[rows, hs], stb[h]) for h, hs in enumerate(heads)]
        v_new = [u_s[rows, hs] - ws[h] for h, hs in enumerate(heads)]
        for h, hs in enumerate(heads):
            oacc_s[rows, hs] = qs[h] + _mm(aqk_s[rows, h * c:(h + 1) * c], v_new[h])
            state_s[h] = st[h] * egl[:, hs] + _mm_tn(kn_s[rows, hs], v_new[h])
        return carry

    lax.fori_loop(0, ts // c, scan_body, 0)

    for h in range(GDN_HEADS):
        hs = slice(h * GDN_D, (h + 1) * GDN_D)
        oh = oacc_s[:, hs]
        ms = jnp.mean(oh * oh, axis=-1, keepdims=True)
        o_ref[:, hs] = (oh * lax.rsqrt(ms + EPS) * onw_ref[...] * _silu(gate_ref[:, hs])).astype(o_ref.dtype)


def _gdn(qkv, small, gate, conv_w, a_log, dt_bias, out_norm_w, ts):
    b, s, _ = qkv.shape
    hw = GDN_W
    c = GDN_CHUNK
    rep = lambda v: jnp.repeat(v.astype(F32), GDN_D)[None, :]
    lane_head = np.arange(hw) // GDN_D
    ea = (np.arange(LANES)[:, None] == lane_head[None, :]).astype(np.float32)
    eb = (np.arange(LANES)[:, None] == (lane_head[None, :] + GDN_HEADS)).astype(np.float32)
    ltri = np.tril(np.ones((c, c), np.float32))
    full = lambda shape: pl.BlockSpec(shape, lambda ib, ij: (0,) * len(shape))
    return pl.pallas_call(
        _gdn_kernel,
        grid=(b, s // ts),
        in_specs=[
            pl.BlockSpec((None, ts, 3 * hw), lambda ib, ij: (ib, ij, 0)),
            pl.BlockSpec((None, ts, LANES), lambda ib, ij: (ib, ij, 0)),
            pl.BlockSpec((None, ts, hw), lambda ib, ij: (ib, ij, 0)),
            full((GDN_CONV, 3 * hw)), full((1, hw)), full((1, hw)), full((1, GDN_D)),
            full((LANES, hw)), full((LANES, hw)), full((c, c)),
        ],
        out_specs=pl.BlockSpec((None, ts, hw), lambda ib, ij: (ib, ij, 0)),
        out_shape=jax.ShapeDtypeStruct((b, s, hw), BF16),
        scratch_shapes=[
            pltpu.VMEM((ts + HALO, 3 * hw), F32),
            pltpu.VMEM((ts, hw), F32), pltpu.VMEM((ts, hw), F32), pltpu.VMEM((ts, hw), F32),
            pltpu.VMEM((ts, hw), F32), pltpu.VMEM((ts, hw), F32),
            pltpu.VMEM((ts, hw), F32), pltpu.VMEM((ts, hw), F32), pltpu.VMEM((ts, GDN_HEADS * c), F32),
            pltpu.VMEM((ts // c * HALO, hw), F32),
            pltpu.VMEM((ts, hw), F32),
            pltpu.VMEM((GDN_HEADS, GDN_D, GDN_D), F32),
        ],
        compiler_params=_cparams(("arbitrary", "arbitrary")),
        name="gdn",
    )(qkv, small, gate, conv_w.astype(F32), rep(a_log), rep(dt_bias), out_norm_w.astype(F32)[None, :],
      jnp.asarray(ea), jnp.asarray(eb), jnp.asarray(ltri))


def _rope_tables(s):
    half = NSA_ROPE_DIM // 2
    pos = jnp.arange(s, dtype=F32)
    inv = 1.0 / (ROPE_THETA ** (jnp.arange(0, NSA_ROPE_DIM, 2, dtype=F32) / NSA_ROPE_DIM))
    ang = pos[:, None] * inv[None, :]
    cos, sin = jnp.cos(ang), jnp.sin(ang)
    one = jnp.ones((s, NSA_D - NSA_ROPE_DIM), F32)
    zero = jnp.zeros((s, NSA_D - NSA_ROPE_DIM), F32)
    zh = jnp.zeros((s, half), F32)
    tc = jnp.concatenate([cos, cos, one], axis=-1)
    ta = jnp.concatenate([-sin, zh, zero], axis=-1)
    tb = jnp.concatenate([zh, sin, zero], axis=-1)
    dup = lambda t: jnp.concatenate([t, t], axis=-1)
    return dup(tc), dup(ta), dup(tb)


def _rope(x, tc, ta, tb):
    half = NSA_ROPE_DIM // 2
    return x * tc + pltpu.roll(x, LANES - half, 1) * ta + pltpu.roll(x, half, 1) * tb


def _group_ms(x, ones_blk):
    return _dot_exact_rhs(x * x, ones_blk) * (1.0 / NSA_D)


def _dup_groups(x):
    r = pltpu.roll(x, NSA_D, 1)
    lane = lax.broadcasted_iota(jnp.int32, x.shape, 1)
    lo = lane < NSA_D
    return jnp.where(lo, x, r), jnp.where(lo, r, x)


def _nsa_prep_kernel(nq_ref, kv_ref, sm_ref, tc_ref, ta_ref, tb_ref, qw_ref, ksw_ref, kww_ref, ones_ref,
                     q_ref, ks_ref, vs_ref, kw_ref, vw_ref, gs_ref, *, gate_col0):
    j = pl.program_id(1)
    ts = nq_ref.shape[0]
    tc, ta, tb = tc_ref[...], ta_ref[...], tb_ref[...]
    ones_blk = ones_ref[...]
    scale = NSA_D ** -0.5 * LOG2E
    for p in range(NSA_W // LANES):
        cols = slice(p * LANES, (p + 1) * LANES)
        x = nq_ref[:, cols]
        xn = x * lax.rsqrt(_group_ms(x, ones_blk) + EPS) * qw_ref[...]
        q_ref[:, cols] = (_rope(xn, tc, ta, tb) * scale).astype(q_ref.dtype)
    for src, nw_ref, k_out, v_out in ((0, ksw_ref, ks_ref, vs_ref), (2, kww_ref, kw_ref, vw_ref)):
        k = kv_ref[:, src * LANES:(src + 1) * LANES]
        v = kv_ref[:, (src + 1) * LANES:(src + 2) * LANES]
        kn = k * lax.rsqrt(_group_ms(k, ones_blk) + EPS) * nw_ref[...]
        kr = _rope(kn, tc, ta, tb)
        for ig, (kg, vg) in enumerate(zip(_dup_groups(kr), _dup_groups(v))):
            k_out[ig, :, 0:LANES] = kg.astype(k_out.dtype)
            v_out[ig] = vg.astype(v_out.dtype)
    pos = j * ts + lax.broadcasted_iota(jnp.int32, (ts, LANES), 0)
    lane = lax.broadcasted_iota(jnp.int32, (ts, LANES), 1)
    onehot = jnp.where(pos // NSA_SEL_BLOCK == lane, 1.0, 0.0).astype(ks_ref.dtype)
    sig = _sigmoid(sm_ref[...])
    for ig in range(NSA_GROUPS):
        ks_ref[ig, :, LANES:2 * LANES] = onehot
        gs_ref[ig] = pltpu.roll(sig, LANES - (gate_col0 + ig * NSA_REP * 3), 1)


def _nsa_prep(nq, kv4, small, tabs, q_norm_w, ks_norm_w, kw_norm_w, gate_col0, ts):
    b, s, _ = nq.shape
    g = NSA_GROUPS
    tile2 = lambda w: jnp.concatenate([w, w]).astype(F32)[None, :]
    ones_blk = np.kron(np.eye(2, dtype=np.float32), np.ones((NSA_D, NSA_D), np.float32))
    full = lambda shape: pl.BlockSpec(shape, lambda ib, ij: (0,) * len(shape))
    tok = lambda w: pl.BlockSpec((None, ts, w), lambda ib, ij: (ib, ij, 0))
    tab = pl.BlockSpec((ts, LANES), lambda ib, ij: (ij, 0))
    kv_out = lambda w: pl.BlockSpec((None, g, ts, w), lambda ib, ij: (ib, 0, ij, 0))
    kv_shape = lambda w, dt: jax.ShapeDtypeStruct((b, g, s, w), dt)
    return pl.pallas_call(
        functools.partial(_nsa_prep_kernel, gate_col0=gate_col0),
        grid=(b, s // ts),
        in_specs=[tok(NSA_W), tok(4 * LANES), tok(LANES), tab, tab, tab,
                  full((1, LANES)), full((1, LANES)), full((1, LANES)), full((LANES, LANES))],
        out_specs=[tok(NSA_W), kv_out(2 * LANES), kv_out(LANES), kv_out(LANES), kv_out(LANES), kv_out(LANES)],
        out_shape=[jax.ShapeDtypeStruct((b, s, NSA_W), BF16), kv_shape(2 * LANES, BF16), kv_shape(LANES, BF16),
                   kv_shape(LANES, BF16), kv_shape(LANES, BF16), kv_shape(LANES, F32)],
        compiler_params=_cparams(("arbitrary", "arbitrary")),
        name="nsa_prep",
    )(nq, kv4, small, *tabs, tile2(q_norm_w), tile2(ks_norm_w), tile2(kw_norm_w), jnp.asarray(ones_blk))


def _nsa_compress_kernel(kc_ref, vc_ref, tc_ref, ta_ref, tb_ref, pk_ref, pv_ref, kw1_ref, kw2_ref,
                         vw1_ref, vw2_ref, nw_ref, ones_ref, kc_out, vc_out):
    nrow, width = kc_ref.shape
    half = width
    outs = []
    for is_k in (True, False):
        x = (kc_ref if is_k else vc_ref)[...]
        if is_k:
            x = jnp.concatenate(
                [_rope(x[:, p * LANES:(p + 1) * LANES], tc_ref[:, p * LANES:(p + 1) * LANES],
                       ta_ref[:, p * LANES:(p + 1) * LANES], tb_ref[:, p * LANES:(p + 1) * LANES])
                 for p in range(width // LANES)], axis=-1)
        pos_ref, w1_ref, w2_ref = (pk_ref, kw1_ref, kw2_ref) if is_k else (pv_ref, vw1_ref, vw2_ref)
        first = _mm(x + pos_ref[0:1, :], w1_ref[0:half, :])
        second = _mm(x + pos_ref[1:2, :], w1_ref[half:2 * half, :])
        y = first + pltpu.roll(second, nrow - 1, 0)
        y = _mm(_silu(y), w2_ref[...])
        if is_k:
            y = y * lax.rsqrt(_group_ms(y, ones_ref[...]) + EPS) * nw_ref[...]
        outs.append(y)
    for y, out in zip(outs, (kc_out, vc_out)):
        y0, y1 = _dup_groups(y)
        out[0] = y0.astype(out.dtype)
        out[1] = y1.astype(out.dtype)


def _nsa_compress(kc, vc, tabs, pos_k, pos_v, k_w1, k_w2, v_w1, v_w2, kc_norm_w):
    b, s, _ = kc.shape
    g = NSA_GROUPS
    st = NSA_CMP_STRIDE
    nrow = s // st
    width = st * LANES
    flat = lambda t: t.reshape(t.shape[0], nrow, width)
    eye_g = jnp.eye(g, dtype=F32)

    def w1_blk(w1):
        wl = w1.reshape(NSA_CMP_BLOCK, NSA_D, NSA_D)
        return jnp.einsum("lde,gh->lgdhe", wl, eye_g).reshape(NSA_CMP_BLOCK * LANES, LANES).astype(BF16)

    def w2_blk(w2):
        return jnp.einsum("de,gh->gdhe", w2, eye_g).reshape(LANES, LANES).astype(BF16)

    def pos_rows(p):
        return jnp.concatenate([p, p], axis=-1).reshape(2, width).astype(F32)

    ones_blk = np.kron(np.eye(2, dtype=np.float32), np.ones((NSA_D, NSA_D), np.float32))
    full = lambda shape: pl.BlockSpec(shape, lambda ib: (0,) * len(shape))
    seq = pl.BlockSpec((None, nrow, width), lambda ib: (ib, 0, 0))
    out = pl.BlockSpec((None, g, nrow, LANES), lambda ib: (ib, 0, 0, 0))
    oshape = jax.ShapeDtypeStruct((b, g, nrow, LANES), BF16)
    tabs_r = [t.reshape(nrow, width) for t in tabs]
    return pl.pallas_call(
        _nsa_compress_kernel,
        grid=(b,),
        in_specs=[seq, seq, full((nrow, width)), full((nrow, width)), full((nrow, width)),
                  full((2, width)), full((2, width)),
                  full((2 * width, LANES)), full((LANES, LANES)), full((2 * width, LANES)), full((LANES, LANES)),
                  full((1, LANES)), full((LANES, LANES))],
        out_specs=[out, out],
        out_shape=[oshape, oshape],
        compiler_params=_cparams(("arbitrary",)),
        name="nsa_compress",
    )(flat(kc), flat(vc), *tabs_r, pos_rows(pos_k), pos_rows(pos_v), w1_blk(k_w1), w2_blk(k_w2),
      w1_blk(v_w1), w2_blk(v_w2), jnp.concatenate([kc_norm_w, kc_norm_w]).astype(F32)[None, :],
      jnp.asarray(ones_blk))


def _nsa_attn_kernel(q_ref, gs_ref, kc_ref, vc_ref, ks_ref, vs_ref, kw_ref, vw_ref, ov_ref, tri_ref, wb_ref,
                     o_ref, *, kb, wlen):
    i = pl.program_id(1)
    tq = NSA_Q_BLOCK
    rep = NSA_REP
    rows = rep * tq
    groups = range(NSA_GROUPS)
    s0 = i * tq
    ncmp = kc_ref.shape[1]
    nblk = vs_ref.shape[1] // NSA_SEL_BLOCK
    blk_rows = tq

    lane_q = lax.broadcasted_iota(jnp.int32, (tq, LANES), 1)
    qs = []
    for g in groups:
        pieces = []
        for r in range(rep):
            pair = g * (rep // 2) + r // 2
            tile = q_ref[:, pair * LANES:(pair + 1) * LANES]
            keep = (lane_q < NSA_D) if r % 2 == 0 else (lane_q >= NSA_D)
            pieces.append(jnp.where(keep, tile, jnp.zeros_like(tile)))
        qs.append(jnp.concatenate(pieces, axis=0))

    cend = (lax.broadcasted_iota(jnp.int32, (rows, ncmp), 1) * NSA_CMP_STRIDE + (NSA_CMP_BLOCK - 1)) - s0
    cmask = cend <= (lax.broadcasted_iota(jnp.int32, (rows, ncmp), 0) & (tq - 1))
    sc = [jnp.where(cmask, _mm_nt(qs[g], kc_ref[g]), NEG) for g in groups]
    mc = [jnp.max(sc[g], axis=-1, keepdims=True) for g in groups]
    pc = [jnp.where(cmask, jnp.exp2(sc[g] - mc[g]), 0.0) for g in groups]
    lc = [jnp.sum(pc[g], axis=-1, keepdims=True) for g in groups]
    pc = [pc[g] * (1.0 / jnp.maximum(lc[g], 1e-30)) for g in groups]
    o_cmp = [_mm(pc[g], vc_ref[g]) for g in groups]

    wrows = pl.ds(pl.multiple_of(s0, tq), wlen)
    col = lax.broadcasted_iota(jnp.int32, (1, wlen), 1)
    wbias = wb_ref[...] + jnp.where(col >= (wlen - tq) - s0, 0.0, NEG)
    sw = [_mm_nt(qs[g], kw_ref[g, wrows, :]) + wbias for g in groups]
    mw = [jnp.max(sw[g], axis=-1, keepdims=True) for g in groups]
    pw = [jnp.exp2(sw[g] - mw[g]) for g in groups]
    lw = [jnp.sum(pw[g], axis=-1, keepdims=True) for g in groups]
    acc_w = [_mm(pw[g], vw_ref[g, wrows, :]) for g in groups]

    drows = pl.ds(pl.multiple_of(s0, tq), tq)
    sd = [_mm_nt(qs[g], ks_ref[g, drows, 0:LANES]) + tri_ref[...] for g in groups]
    m0 = [jnp.max(sd[g], axis=-1, keepdims=True) for g in groups]
    pd = [jnp.exp2(sd[g] - m0[g]) for g in groups]
    l0 = [jnp.sum(pd[g], axis=-1, keepdims=True) for g in groups]
    a0 = [_mm(pd[g], vs_ref[g, drows, :]) for g in groups]

    imp = []
    for g in groups:
        psum = pc[g][0:tq]
        for r in range(1, rep):
            psum = psum + pc[g][r * tq:(r + 1) * tq]
        imp.append(_dot_exact_rhs(psum, ov_ref[...]))
    blk = lane_q
    valid = blk <= i

    def ranked():
        forced = (blk == 0) | (blk == i) | (blk == i - 1)
        keys = []
        for g in groups:
            key = jnp.where(valid, jnp.where(forced, 0x7F000000, pltpu.bitcast(imp[g], jnp.int32)), -1)
            keys.append(jnp.where(blk < nblk, key, -2))
        key_t = jnp.concatenate(keys, axis=0).T[0:blk_rows, :]
        key_m1 = key_t - 1
        row = lax.broadcasted_iota(jnp.int32, (blk_rows, NSA_GROUPS * tq), 0)
        rank = jnp.zeros((blk_rows, NSA_GROUPS * tq), jnp.int32)
        for jb in range(nblk):
            ahead = key_t[jb:jb + 1, :] > jnp.where(row > jb, key_m1, key_t)
            rank = rank + jnp.where(ahead, 1, 0)
        sel_t = jnp.where(rank < NSA_N_SEL, 1.0, 0.0)
        if blk_rows < LANES:
            sel_t = jnp.concatenate([sel_t, jnp.zeros((LANES - blk_rows, NSA_GROUPS * tq), F32)], axis=0)
        sel_all = sel_t.T
        return tuple(sel_all[g * tq:(g + 1) * tq] for g in groups)

    sel = lax.cond(i >= NSA_N_SEL, ranked, lambda: tuple(jnp.where(valid, 1.0, 0.0) for _ in groups))
    q2 = []
    for g in groups:
        selneg = jnp.where((sel[g] > 0.5) & (blk < i), 0.0, NEG).astype(BF16)
        q2.append(jnp.concatenate([qs[g], jnp.concatenate([selneg] * rep, axis=0)], axis=1))

    def slc_body(ic, carry):
        m_i, l_i, acc = carry
        krows = pl.ds(pl.multiple_of(ic * kb, kb), kb)
        s = [_mm_nt(q2[g], ks_ref[g, krows, :]) for g in groups]
        m_new = [jnp.maximum(m_i[g], jnp.max(s[g], axis=-1, keepdims=True)) for g in groups]
        alpha = [jnp.exp2(m_i[g] - m_new[g]) for g in groups]
        pr = [jnp.exp2(s[g] - m_new[g]) for g in groups]
        l_new = [alpha[g] * l_i[g] + jnp.sum(pr[g], axis=-1, keepdims=True) for g in groups]
        acc_new = [alpha[g] * acc[g] + _mm(pr[g], vs_ref[g, krows, :]) for g in groups]
        return tuple(m_new), tuple(l_new), tuple(acc_new)

    _, l_s, acc_s = lax.fori_loop(0, (s0 + kb - 1) // kb, slc_body, (tuple(m0), tuple(l0), tuple(a0)))

    lo = lane_q < NSA_D
    for g in groups:
        gs = gs_ref[g]
        gcol = lambda x: jnp.concatenate([gs[:, r * 3 + x:r * 3 + x + 1] for r in range(rep)], axis=0)
        o = gcol(0) * o_cmp[g] + (gcol(1) / l_s[g]) * acc_s[g] + (gcol(2) / lw[g]) * acc_w[g]
        for pr_ in range(rep // 2):
            even = o[(2 * pr_) * tq:(2 * pr_ + 1) * tq]
            odd = o[(2 * pr_ + 1) * tq:(2 * pr_ + 2) * tq]
            pair = g * (rep // 2) + pr_
            o_ref[:, pair * LANES:(pair + 1) * LANES] = jnp.where(lo, even, odd).astype(o_ref.dtype)


def _nsa_attn(q, gsig, kcmp, vcmp, ks3, vs2, kw2, vw2):
    b, s, _ = q.shape
    g = NSA_GROUPS
    rep = NSA_REP
    tq = NSA_Q_BLOCK
    rows = rep * tq
    ncmp = kcmp.shape[2]
    nblk = s // NSA_SEL_BLOCK
    assert nblk <= tq and nblk <= NSA_D
    kb = min(1024, s)
    wlen = NSA_WINDOW + 2 * tq
    wpad = wlen - tq
    ci = np.arange(ncmp) * NSA_CMP_STRIDE
    sj = np.arange(nblk) * NSA_SEL_BLOCK
    ov = np.clip(np.minimum(ci[:, None] + NSA_CMP_BLOCK, sj[None, :] + NSA_SEL_BLOCK)
                 - np.maximum(ci[:, None], sj[None, :]), 0, None).astype(np.float32) / NSA_CMP_STRIDE
    ov_p = np.zeros((ncmp, LANES), np.float32)
    ov_p[:, :nblk] = ov
    tloc = (np.arange(rows) % tq)[:, None]
    tri = np.where(np.arange(tq)[None, :] <= tloc, 0.0, NEG).astype(np.float32)
    wcol = np.arange(wlen)[None, :]
    wb = np.where((wcol <= wpad + tloc) & (wcol > wpad - NSA_WINDOW + tloc), 0.0, NEG).astype(np.float32)
    pad = lambda a: jnp.pad(a, ((0, 0), (0, 0), (wpad, 0), (0, 0)))
    seq = lambda n, w: pl.BlockSpec((None, g, n, w), lambda ib, ii: (ib, 0, 0, 0))
    full = lambda shape: pl.BlockSpec(shape, lambda ib, ii: (0,) * len(shape))
    return pl.pallas_call(
        functools.partial(_nsa_attn_kernel, kb=kb, wlen=wlen),
        grid=(b, s // tq),
        in_specs=[
            pl.BlockSpec((None, tq, NSA_W), lambda ib, ii: (ib, ii, 0)),
            pl.BlockSpec((None, g, tq, LANES), lambda ib, ii: (ib, 0, ii, 0)),
            seq(ncmp, LANES), seq(ncmp, LANES), seq(s, 2 * LANES), seq(s, LANES),
            seq(s + wpad, LANES), seq(s + wpad, LANES),
            full((ncmp, LANES)), full((rows, tq)), full((rows, wlen)),
        ],
        out_specs=pl.BlockSpec((None, tq, NSA_W), lambda ib, ii: (ib, ii, 0)),
        out_shape=jax.ShapeDtypeStruct((b, s, NSA_W), BF16),
        compiler_params=_cparams(("arbitrary", "arbitrary")),
        name="nsa_attn",
    )(q, gsig, kcmp, vcmp, ks3, vs2, pad(kw2), pad(vw2), jnp.asarray(ov_p), jnp.asarray(tri), jnp.asarray(wb))


def _mem_kv_kernel(mem_ref, nw_ref, w_ref, knw_ref, k_ref, v_ref):
    x = mem_ref[...]
    ms = jnp.mean(x * x, axis=-1, keepdims=True)
    xn = x * lax.rsqrt(ms + EPS) * nw_ref[...]
    kv = _mm(xn, w_ref[...])
    v_ref[...] = kv[:, MEM_W:].astype(v_ref.dtype)
    for h in range(MEM_HEADS):
        hs = slice(h * MEM_D, (h + 1) * MEM_D)
        kh = kv[:, hs]
        msk = jnp.mean(kh * kh, axis=-1, keepdims=True)
        k_ref[:, hs] = (kh * lax.rsqrt(msk + EPS) * knw_ref[...]).astype(k_ref.dtype)


def _mem_kv(mem, mem_norm_w, w_kv, k_norm_w):
    b, m, d = mem.shape
    full = lambda shape: pl.BlockSpec(shape, lambda ib: (0,) * len(shape))
    blk = lambda w: pl.BlockSpec((None, m, w), lambda ib: (ib, 0, 0))
    return pl.pallas_call(
        _mem_kv_kernel,
        grid=(b,),
        in_specs=[blk(d), full((1, d)), full((d, 2 * MEM_W)), full((1, MEM_D))],
        out_specs=[blk(MEM_W), blk(MEM_W)],
        out_shape=[jax.ShapeDtypeStruct((b, m, MEM_W), BF16)] * 2,
        compiler_params=_cparams(("arbitrary",)),
        name="mem_kv",
    )(mem, mem_norm_w.astype(F32)[None, :], w_kv.astype(BF16), k_norm_w.astype(F32)[None, :])


def _mem_attn_kernel(q_ref, k_ref, v_ref, qnw_ref, o_ref):
    scale = MEM_D ** -0.5
    for h in range(MEM_HEADS):
        hs = slice(h * MEM_D, (h + 1) * MEM_D)
        qh = q_ref[:, hs]
        ms = jnp.mean(qh * qh, axis=-1, keepdims=True)
        qn = qh * lax.rsqrt(ms + EPS) * qnw_ref[...]
        s = _mm_nt(qn, k_ref[:, hs]) * scale
        m = jnp.max(s, axis=-1, keepdims=True)
        p = jnp.exp(s - m)
        l = jnp.sum(p, axis=-1, keepdims=True)
        o_ref[:, hs] = (_mm(p, v_ref[:, hs]) * (1.0 / l)).astype(o_ref.dtype)


def _mem_attn(mq, k, v, q_norm_w, ts):
    b, s, _ = mq.shape
    m = k.shape[1]
    return pl.pallas_call(
        _mem_attn_kernel,
        grid=(b, s // ts),
        in_specs=[
            pl.BlockSpec((None, ts, MEM_W), lambda ib, ij: (ib, ij, 0)),
            pl.BlockSpec((None, m, MEM_W), lambda ib, ij: (ib, 0, 0)),
            pl.BlockSpec((None, m, MEM_W), lambda ib, ij: (ib, 0, 0)),
            pl.BlockSpec((1, MEM_D), lambda ib, ij: (0, 0)),
        ],
        out_specs=pl.BlockSpec((None, ts, MEM_W), lambda ib, ij: (ib, ij, 0)),
        out_shape=jax.ShapeDtypeStruct((b, s, MEM_W), BF16),
        compiler_params=_cparams(("arbitrary", "arbitrary")),
        name="mem_attn",
    )(mq, k, v, q_norm_w.astype(F32)[None, :])


def _out_proj_kernel(x_ref, oa_ref, ob_ref, oc_ref, w_ref, h_ref):
    acc = x_ref[...]
    off = 0
    for o_ref in (oa_ref, ob_ref, oc_ref):
        wd = o_ref.shape[-1]
        acc = acc + jnp.dot(o_ref[...].astype(BF16), w_ref[off:off + wd, :], preferred_element_type=F32)
        off += wd
    h_ref[...] = acc


def _out_proj(x2, oa, ob, oc, w_out, tm):
    t, d = x2.shape
    row = lambda w: pl.BlockSpec((tm, w), lambda i: (i, 0))
    return pl.pallas_call(
        _out_proj_kernel,
        grid=(t // tm,),
        in_specs=[row(d), row(oa.shape[1]), row(ob.shape[1]), row(oc.shape[1]),
                  pl.BlockSpec(w_out.shape, lambda i: (0, 0))],
        out_specs=row(d),
        out_shape=jax.ShapeDtypeStruct((t, d), F32),
        compiler_params=_cparams(("arbitrary",)),
        name="out_proj",
    )(x2, oa, ob, oc, w_out)


def _ffn_kernel(h_ref, halo_ref, nw_ref, wup_ref, cw_ref, wdn_ref, o_ref, hn_s, u_s, act_s, *, fc):
    j = pl.program_id(1)
    ts = h_ref.shape[0]
    f = wdn_ref.shape[0]
    hl = halo_ref.shape[0]

    def norm(x):
        ms = jnp.mean(x * x, axis=-1, keepdims=True)
        return (x * lax.rsqrt(ms + EPS) * nw_ref[...]).astype(BF16)

    halo = jnp.where(j > 0, halo_ref[...], 0.0)
    hn_s[0:hl, :] = norm(halo)
    hn_s[hl:hl + ts, :] = norm(h_ref[...])
    for ic in range(f // fc):
        hn = hn_s[...]
        slot = ic % 2
        for part in range(2):
            cols = slice(part * f + ic * fc, part * f + (ic + 1) * fc)
            u_s[slot, part] = jnp.dot(hn, wup_ref[:, cols], preferred_element_type=F32)
        conv = []
        for part in range(2):
            cols = slice(part * f + ic * fc, part * f + (ic + 1) * fc)
            acc = cw_ref[FFN_CONV - 1:FFN_CONV, cols] * u_s[slot, part, hl:hl + ts, :]
            for jj in range(FFN_CONV - 1):
                acc = acc + cw_ref[jj:jj + 1, cols] * u_s[slot, part, pl.ds(hl - (FFN_CONV - 1) + jj, ts), :]
            conv.append(acc)
        act_s[:, ic * fc:(ic + 1) * fc] = (_silu(conv[0]) * conv[1]).astype(BF16)
    o_ref[...] = h_ref[...] + jnp.dot(act_s[...], wdn_ref[...], preferred_element_type=F32)


def _ffn(h, norm_w, w_up, conv_w, w_down, ts, fc):
    b, s, d = h.shape
    f = w_down.shape[0]
    full = lambda shape: pl.BlockSpec(shape, lambda ib, ij: (0,) * len(shape))
    return pl.pallas_call(
        functools.partial(_ffn_kernel, fc=fc),
        grid=(b, s // ts),
        in_specs=[
            pl.BlockSpec((None, ts, d), lambda ib, ij: (ib, ij, 0)),
            pl.BlockSpec((None, HALO_BF16, d), lambda ib, ij: (ib, jnp.maximum(ij * (ts // HALO_BF16) - 1, 0), 0)),
            full((1, d)), full((d, 2 * f)), full((FFN_CONV, 2 * f)), full((f, d)),
        ],
        out_specs=pl.BlockSpec((None, ts, d), lambda ib, ij: (ib, ij, 0)),
        out_shape=jax.ShapeDtypeStruct((b, s, d), F32),
        scratch_shapes=[
            pltpu.VMEM((ts + HALO_BF16, d), BF16),
            pltpu.VMEM((2, 2, ts + HALO_BF16, fc), F32),
            pltpu.VMEM((ts, f), BF16),
        ],
        compiler_params=_cparams(("arbitrary", "arbitrary")),
        name="ffn",
    )(h, h, norm_w.astype(F32)[None, :], w_up, conv_w.astype(F32), w_down)


def _split_w_in(w_in):
    sizes = (3 * GDN_W, GDN_HEADS, GDN_HEADS, GDN_W, NSA_W, NSA_KV_W, NSA_KV_W, NSA_KV_W, NSA_KV_W,
             NSA_KV_W, NSA_KV_W, 3 * NSA_HEADS, MEM_W)
    offs = np.concatenate([[0], np.cumsum(sizes)])
    (qkv, a, bb, gate, nq, kc, vc, ks, vs, kw, vw, ng, mq) = [w_in[:, offs[i]:offs[i + 1]] for i in range(len(sizes))]
    n_small = 2 * GDN_HEADS + 3 * NSA_HEADS
    small = jnp.concatenate([a, bb, ng, jnp.zeros((w_in.shape[0], LANES - n_small), w_in.dtype)], axis=1)
    widths = (3 * GDN_W, GDN_W, NSA_W, NSA_KV_W, NSA_KV_W, 4 * NSA_KV_W, MEM_W, LANES)
    w_cat = jnp.concatenate([qkv, gate, nq, kc, vc, ks, vs, kw, vw, mq, small], axis=1).astype(BF16)
    return w_cat, widths


def _layer(x, mem, attn_norm_w, mem_norm_w, w_in, gdn_conv_w, gdn_a_log, gdn_dt_bias, gdn_out_norm_w,
           nsa_q_norm_w, nsa_kc_norm_w, nsa_ks_norm_w, nsa_kw_norm_w, nsa_cmp_pos_k, nsa_cmp_pos_v,
           nsa_cmp_k_w1, nsa_cmp_k_w2, nsa_cmp_v_w1, nsa_cmp_v_w2, mem_w_kv, mem_q_norm_w, mem_k_norm_w,
           w_out, ffn_norm_w, ffn_w_up, ffn_conv_w, ffn_w_down):
    b, s, d = x.shape
    t = b * s
    ts = min(512, s)
    x2 = x.reshape(t, d)

    w_cat, widths = _split_w_in(w_in)
    qkv, gate, nq, kc, vc, kv4, mq, small = _in_proj(x2, attn_norm_w.astype(F32)[None, :], w_cat, widths, ts)
    r3 = lambda a: a.reshape(b, s, a.shape[-1])

    o_a = _gdn(r3(qkv), r3(small), r3(gate), gdn_conv_w, gdn_a_log, gdn_dt_bias, gdn_out_norm_w, ts)

    tabs = _rope_tables(s)
    q_r, ks3, vs2, kw2, vw2, gsig = _nsa_prep(r3(nq), r3(kv4), r3(small), tabs, nsa_q_norm_w, nsa_ks_norm_w,
                                              nsa_kw_norm_w, 2 * GDN_HEADS, ts)
    kcmp, vcmp = _nsa_compress(r3(kc), r3(vc), tabs, nsa_cmp_pos_k, nsa_cmp_pos_v, nsa_cmp_k_w1, nsa_cmp_k_w2,
                               nsa_cmp_v_w1, nsa_cmp_v_w2, nsa_kc_norm_w)
    o_b = _nsa_attn(q_r, gsig, kcmp, vcmp, ks3, vs2, kw2, vw2)

    mk, mv = _mem_kv(mem, mem_norm_w, mem_w_kv, mem_k_norm_w)
    o_c = _mem_attn(r3(mq), mk, mv, mem_q_norm_w, ts)

    h = _out_proj(x2, o_a.reshape(t, GDN_W), o_b.reshape(t, NSA_W), o_c.reshape(t, MEM_W), w_out.astype(BF16), ts)
    out = _ffn(h.reshape(b, s, d), ffn_norm_w, ffn_w_up.astype(BF16), ffn_conv_w, ffn_w_down.astype(BF16), ts, 256)
    return out


def kernel(x, mem, attn_norm_w, mem_norm_w, w_in, gdn_conv_w, gdn_a_log, gdn_dt_bias, gdn_out_norm_w, nsa_q_norm_w, nsa_kc_norm_w, nsa_ks_norm_w, nsa_kw_norm_w, nsa_cmp_pos_k, nsa_cmp_pos_v, nsa_cmp_k_w1, nsa_cmp_k_w2, nsa_cmp_v_w1, nsa_cmp_v_w2, mem_w_kv, mem_q_norm_w, mem_k_norm_w, w_out, ffn_norm_w, ffn_w_up, ffn_conv_w, ffn_w_down):
    h = x
    for l in range(w_in.shape[0]):
        h = _layer(h, mem, attn_norm_w[l], mem_norm_w[l], w_in[l], gdn_conv_w[l], gdn_a_log[l], gdn_dt_bias[l],
                   gdn_out_norm_w[l], nsa_q_norm_w[l], nsa_kc_norm_w[l], nsa_ks_norm_w[l], nsa_kw_norm_w[l],
                   nsa_cmp_pos_k[l], nsa_cmp_pos_v[l], nsa_cmp_k_w1[l], nsa_cmp_k_w2[l], nsa_cmp_v_w1[l],
                   nsa_cmp_v_w2[l], mem_w_kv[l], mem_q_norm_w[l], mem_k_norm_w[l], w_out[l], ffn_norm_w[l],
                   ffn_w_up[l], ffn_conv_w[l], ffn_w_down[l])
    return h
```

```python
import functools

import jax
import jax.numpy as jnp
import numpy as np
from jax import lax
from jax.experimental import pallas as pl
from jax.experimental.pallas import tpu as pltpu

F32 = jnp.float32
BF16 = jnp.bfloat16

EPS = 1e-6
ROPE_THETA = 500000.0
GDN_HEADS = 4
GDN_D = 128
GDN_CONV = 4
GDN_CHUNK = 64
GDN_SUB = 16
GDN_GROUP = 4
NSA_HEADS = 8
NSA_GROUPS = 2
NSA_REP = NSA_HEADS // NSA_GROUPS
NSA_D = 64
NSA_CMP_BLOCK = 32
NSA_CMP_STRIDE = 16
NSA_SEL_BLOCK = 64
NSA_N_SEL = 16
NSA_WINDOW = 512
NSA_Q_BLOCK = 64
NSA_ROPE_DIM = NSA_D // 4
NSA_SUB_KEYS = 256
NSA_VROWS = NSA_D + 16
MEM_HEADS = 4
MEM_D = 128
FFN_CONV = 3

GDN_W = GDN_HEADS * GDN_D
NSA_W = NSA_HEADS * NSA_D
MEM_W = MEM_HEADS * MEM_D
NSA_KV_W = NSA_GROUPS * NSA_D

LANES = 128
HALO = 8
HALO_BF16 = 16
VMEM_LIMIT = 56 * 1024 * 1024
NEG = -1e30
LOG2E = 1.4426950408889634


def _cparams(sem):
    return pltpu.CompilerParams(dimension_semantics=sem, vmem_limit_bytes=VMEM_LIMIT)


def _mm(a, b):
    return jnp.dot(a.astype(BF16), b.astype(BF16), preferred_element_type=F32)


def _mm_nt(a, b):
    return lax.dot_general(a.astype(BF16), b.astype(BF16), (((1,), (1,)), ((), ())),
                           preferred_element_type=F32)


def _mm_tn(a, b):
    return lax.dot_general(a.astype(BF16), b.astype(BF16), (((0,), (0,)), ((), ())),
                           preferred_element_type=F32)


def _split3(x):
    hi = x.astype(BF16)
    r = x - hi.astype(F32)
    mid = r.astype(BF16)
    lo = (r - mid.astype(F32)).astype(BF16)
    return hi, mid, lo


def _dot_exact_rhs(x, e):
    hi, mid, lo = _split3(x)
    eb = e.astype(BF16)
    return (jnp.dot(hi, eb, preferred_element_type=F32) + jnp.dot(mid, eb, preferred_element_type=F32)
            + jnp.dot(lo, eb, preferred_element_type=F32))


def _dot_exact_lhs(e, x):
    hi, mid, lo = _split3(x)
    eb = e.astype(BF16)
    return (jnp.dot(eb, hi, preferred_element_type=F32) + jnp.dot(eb, mid, preferred_element_type=F32)
            + jnp.dot(eb, lo, preferred_element_type=F32))


def _sigmoid(x):
    return 1.0 / (1.0 + jnp.exp(-x))


def _silu(x):
    return x * _sigmoid(x)


def _softplus(x):
    return jnp.maximum(x, 0.0) + jnp.log(1.0 + jnp.exp(-jnp.abs(x)))


def _in_proj_kernel(x_ref, nw_ref, w_ref, *o_refs):
    x = x_ref[...]
    ms = jnp.mean(x * x, axis=-1, keepdims=True)
    xn = (x * lax.rsqrt(ms + EPS) * nw_ref[...]).astype(BF16)
    off = 0
    for o_ref in o_refs:
        wd = o_ref.shape[-1]
        o_ref[...] = jnp.dot(xn, w_ref[:, off:off + wd], preferred_element_type=F32).astype(o_ref.dtype)
        off += wd


def _in_proj(x2, norm_w, w_cat, widths, tm):
    t, d = x2.shape
    wtot = w_cat.shape[1]
    return pl.pallas_call(
        _in_proj_kernel,
        grid=(t // tm,),
        in_specs=[
            pl.BlockSpec((tm, d), lambda i: (i, 0)),
            pl.BlockSpec((1, d), lambda i: (0, 0)),
            pl.BlockSpec((d, wtot), lambda i: (0, 0)),
        ],
        out_specs=[pl.BlockSpec((tm, wd), lambda i: (i, 0)) for wd in widths],
        out_shape=[jax.ShapeDtypeStruct((t, wd), F32) for wd in widths],
        compiler_params=_cparams(("arbitrary",)),
        name="in_proj",
    )(x2, norm_w, w_cat)


def _gdn_kernel(qkv_ref, sm_ref, gate_ref, cw_ref, alog_ref, dtb_ref, onw_ref, ea_ref, eb_ref, ltri_ref,
                o_ref, xb, qn_s, kn_s, v_s, g_s, beta_s, u_s, w_s, aqk_s, egl_s, oacc_s, state_s):
    j = pl.program_id(1)
    ts = o_ref.shape[0]
    c = GDN_CHUNK
    hw = GDN_W

    @pl.when(j == 0)
    def _():
        xb[0:HALO, :] = jnp.zeros((HALO, 3 * hw), F32)
        state_s[...] = jnp.zeros_like(state_s)

    xb[HALO:HALO + ts, :] = qkv_ref[...]
    for part in range(3):
        cols = slice(part * hw, (part + 1) * hw)
        acc = cw_ref[GDN_CONV - 1:GDN_CONV, cols] * xb[HALO:HALO + ts, cols]
        for jj in range(GDN_CONV - 1):
            acc = acc + cw_ref[jj:jj + 1, cols] * xb[pl.ds(HALO - (GDN_CONV - 1) + jj, ts), cols]
        act = _silu(acc)
        if part == 2:
            v_s[...] = act
        else:
            dst = qn_s if part == 0 else kn_s
            scale = GDN_D ** -0.5 if part == 0 else 1.0
            for h in range(GDN_HEADS):
                hs = slice(h * GDN_D, (h + 1) * GDN_D)
                xh = act[:, hs]
                ss = jnp.sum(xh * xh, axis=-1, keepdims=True)
                dst[:, hs] = xh * (lax.rsqrt(ss + EPS) * scale)
    xb[0:HALO, :] = xb[ts:ts + HALO, :]

    sm = sm_ref[...]
    a_full = _dot_exact_rhs(sm, ea_ref[...])
    b_full = _dot_exact_rhs(sm, eb_ref[...])
    g_s[...] = -jnp.exp(alog_ref[...]) * _softplus(a_full + dtb_ref[...])
    beta_s[...] = _sigmoid(b_full)

    ri = lax.broadcasted_iota(jnp.int32, (c, c), 0)
    ci = lax.broadcasted_iota(jnp.int32, (c, c), 1)
    causal = ri >= ci
    strict = ri > ci
    blockdiag = (ri // GDN_SUB) == (ci // GDN_SUB)
    ltri = ltri_ref[...]
    heads = [slice(h * GDN_D, (h + 1) * GDN_D) for h in range(GDN_HEADS)]

    def precompute(ig, carry):
        base = pl.multiple_of(ig * (GDN_GROUP * c), GDN_GROUP * c)
        kb_l, kn_l, qn_l, rhs_l, decay_l, where_l = [], [], [], [], [], []
        for cc in range(GDN_GROUP):
            rows = pl.ds(base + cc * c, c)
            gc = _dot_exact_lhs(ltri, g_s[rows, :])
            glast = gc[c - 1:c, :]
            eg = jnp.exp(gc)
            beta = beta_s[rows, :]
            kn = kn_s[rows, :]
            qn = qn_s[rows, :]
            kb = kn * beta
            vb = v_s[rows, :] * beta
            kbe = kb * eg
            qn_s[rows, :] = qn * eg
            kn_s[rows, :] = kn * jnp.exp(glast - gc)
            egl_s[pl.ds(pl.multiple_of((ig * GDN_GROUP + cc) * HALO, HALO), HALO), :] = jnp.broadcast_to(
                jnp.exp(glast), (HALO, hw))
            for h, hs in enumerate(heads):
                gcol = gc[:, h * GDN_D:h * GDN_D + c]
                grow = gc[:, hs].T[0:1, 0:c]
                diff = gcol - grow
                decay_l.append(jnp.where(causal, jnp.exp(jnp.where(causal, diff, 0.0)), 0.0))
                kb_l.append(kb[:, hs].astype(BF16))
                kn_l.append(kn[:, hs].astype(BF16))
                qn_l.append(qn[:, hs].astype(BF16))
                rhs_l.append(jnp.concatenate([vb[:, hs], kbe[:, hs]], axis=-1))
                where_l.append((rows, h, hs))
        n = len(where_l)
        kk = [_mm_nt(kb_l[i], kn_l[i]) for i in range(n)]
        qk = [_mm_nt(qn_l[i], kn_l[i]) for i in range(n)]
        for i, (rows, h, hs) in enumerate(where_l):
            aqk_s[rows, h * c:(h + 1) * c] = qk[i] * decay_l[i]
        p = [-jnp.where(strict, kk[i] * decay_l[i], 0.0) for i in range(n)]
        pd = [jnp.where(blockdiag, x, 0.0) for x in p]
        pn = [p[i] - pd[i] for i in range(n)]
        p2 = [_mm(x, x) for x in pd]
        p4 = [_mm(x, x) for x in p2]
        p8 = [_mm(x, x) for x in p4]
        a1 = [pd[i] + p2[i] + _mm(pd[i], p2[i]) for i in range(n)]
        a2 = [a1[i] + p4[i] + _mm(a1[i], p4[i]) for i in range(n)]
        a3 = [a2[i] + p8[i] + _mm(a2[i], p8[i]) for i in range(n)]
        nm = [pn[i] + _mm(a3[i], pn[i]) for i in range(n)]
        n2 = [_mm(x, x) for x in nm]
        bm = [nm[i] + n2[i] + _mm(nm[i], n2[i]) for i in range(n)]
        tm = [bm[i] + a3[i] + _mm(bm[i], a3[i]) for i in range(n)]
        for i, (rows, h, hs) in enumerate(where_l):
            sol = rhs_l[i] + _mm(tm[i], rhs_l[i])
            u_s[rows, hs] = sol[:, :GDN_D]
            w_s[rows, hs] = sol[:, GDN_D:]
        return carry

    lax.fori_loop(0, ts // (GDN_GROUP * c), precompute, 0)

    def scan_body(ic, carry):
        rows = pl.ds(pl.multiple_of(ic * c, c), c)
        egl = egl_s[pl.ds(pl.multiple_of(ic * HALO, HALO), 1), :]
        st = [state_s[h] for h in range(GDN_HEADS)]
        stb = [x.astype(BF16) for x in st]
        ws = [_mm(w_s[rows, hs], stb[h]) for h, hs in enumerate(heads)]
        qs = [_mm(qn_s[rows, hs], stb[h]) for h, hs in enumerate(heads)]
        v_new = [u_s[rows, hs] - ws[h] for h, hs in enumerate(heads)]
        for h, hs in enumerate(heads):
            oacc_s[rows, hs] = qs[h] + _mm(aqk_s[rows, h * c:(h + 1) * c], v_new[h])
            state_s[h] = st[h] * egl[:, hs] + _mm_tn(kn_s[rows, hs], v_new[h])
        return carry

    lax.fori_loop(0, ts // c, scan_body, 0)

    for h in range(GDN_HEADS):
        hs = slice(h * GDN_D, (h + 1) * GDN_D)
        oh = oacc_s[:, hs]
        ms = jnp.mean(oh * oh, axis=-1, keepdims=True)
        o_ref[:, hs] = (oh * lax.rsqrt(ms + EPS) * onw_ref[...] * _silu(gate_ref[:, hs])).astype(o_ref.dtype)


def _gdn(qkv, small, gate, conv_w, a_log, dt_bias, out_norm_w, ts):
    b, s, _ = qkv.shape
    hw = GDN_W
    c = GDN_CHUNK
    rep = lambda v: jnp.repeat(v.astype(F32), GDN_D)[None, :]
    lane_head = np.arange(hw) // GDN_D
    ea = (np.arange(LANES)[:, None] == lane_head[None, :]).astype(np.float32)
    eb = (np.arange(LANES)[:, None] == (lane_head[None, :] + GDN_HEADS)).astype(np.float32)
    ltri = np.tril(np.ones((c, c), np.float32))
    full = lambda shape: pl.BlockSpec(shape, lambda ib, ij: (0,) * len(shape))
    return pl.pallas_call(
        _gdn_kernel,
        grid=(b, s // ts),
        in_specs=[
            pl.BlockSpec((None, ts, 3 * hw), lambda ib, ij: (ib, ij, 0)),
            pl.BlockSpec((None, ts, LANES), lambda ib, ij: (ib, ij, 0)),
            pl.BlockSpec((None, ts, hw), lambda ib, ij: (ib, ij, 0)),
            full((GDN_CONV, 3 * hw)), full((1, hw)), full((1, hw)), full((1, GDN_D)),
            full((LANES, hw)), full((LANES, hw)), full((c, c)),
        ],
        out_specs=pl.BlockSpec((None, ts, hw), lambda ib, ij: (ib, ij, 0)),
        out_shape=jax.ShapeDtypeStruct((b, s, hw), BF16),
        scratch_shapes=[
            pltpu.VMEM((ts + HALO, 3 * hw), F32),
            pltpu.VMEM((ts, hw), F32), pltpu.VMEM((ts, hw), F32), pltpu.VMEM((ts, hw), F32),
            pltpu.VMEM((ts, hw), F32), pltpu.VMEM((ts, hw), F32),
            pltpu.VMEM((ts, hw), F32), pltpu.VMEM((ts, hw), F32), pltpu.VMEM((ts, GDN_HEADS * c), F32),
            pltpu.VMEM((ts // c * HALO, hw), F32),
            pltpu.VMEM((ts, hw), F32),
            pltpu.VMEM((GDN_HEADS, GDN_D, GDN_D), F32),
        ],
        compiler_params=_cparams(("arbitrary", "arbitrary")),
        name="gdn",
    )(qkv, small, gate, conv_w.astype(F32), rep(a_log), rep(dt_bias), out_norm_w.astype(F32)[None, :],
      jnp.asarray(ea), jnp.asarray(eb), jnp.asarray(ltri))


def _rope_tables(s):
    half = NSA_ROPE_DIM // 2
    pos = jnp.arange(s, dtype=F32)
    inv = 1.0 / (ROPE_THETA ** (jnp.arange(0, NSA_ROPE_DIM, 2, dtype=F32) / NSA_ROPE_DIM))
    ang = pos[:, None] * inv[None, :]
    cos, sin = jnp.cos(ang), jnp.sin(ang)
    one = jnp.ones((s, NSA_D - NSA_ROPE_DIM), F32)
    zero = jnp.zeros((s, NSA_D - NSA_ROPE_DIM), F32)
    zh = jnp.zeros((s, half), F32)
    tc = jnp.concatenate([cos, cos, one], axis=-1)
    ta = jnp.concatenate([-sin, zh, zero], axis=-1)
    tb = jnp.concatenate([zh, sin, zero], axis=-1)
    dup = lambda t: jnp.concatenate([t, t], axis=-1)
    return dup(tc), dup(ta), dup(tb)


def _rope(x, tc, ta, tb):
    half = NSA_ROPE_DIM // 2
    return x * tc + pltpu.roll(x, LANES - half, 1) * ta + pltpu.roll(x, half, 1) * tb


def _group_ms(x, ones_blk):
    return _dot_exact_rhs(x * x, ones_blk) * (1.0 / NSA_D)


def _dup_groups(x):
    r = pltpu.roll(x, NSA_D, 1)
    lane = lax.broadcasted_iota(jnp.int32, x.shape, 1)
    lo = lane < NSA_D
    return jnp.where(lo, x, r), jnp.where(lo, r, x)


def _vt_block(vt):
    n = vt.shape[1]
    tail = jnp.where(lax.broadcasted_iota(jnp.int32, (NSA_VROWS - NSA_D, n), 0) == 0, 1.0, 0.0)
    return jnp.concatenate([vt, tail], axis=0).astype(BF16)


def _nsa_prep_kernel(nq_ref, kv_ref, sm_ref, tc_ref, ta_ref, tb_ref, qw_ref, ksw_ref, kww_ref, ones_ref,
                     q_ref, ks_ref, vs_ref, kw_ref, vw_ref, gs_ref, *, gate_col0):
    j = pl.program_id(1)
    ts = nq_ref.shape[0]
    tc, ta, tb = tc_ref[...], ta_ref[...], tb_ref[...]
    ones_blk = ones_ref[...]
    scale = NSA_D ** -0.5 * LOG2E
    for p in range(NSA_W // LANES):
        cols = slice(p * LANES, (p + 1) * LANES)
        x = nq_ref[:, cols]
        xn = x * lax.rsqrt(_group_ms(x, ones_blk) + EPS) * qw_ref[...]
        q_ref[:, cols] = (_rope(xn, tc, ta, tb) * scale).astype(q_ref.dtype)
    for src, nw_ref, k_out, v_out in ((0, ksw_ref, ks_ref, vs_ref), (2, kww_ref, kw_ref, vw_ref)):
        k = kv_ref[:, src * LANES:(src + 1) * LANES]
        v = kv_ref[:, (src + 1) * LANES:(src + 2) * LANES]
        kn = k * lax.rsqrt(_group_ms(k, ones_blk) + EPS) * nw_ref[...]
        kr = _rope(kn, tc, ta, tb)
        vt = v.T
        for ig, kg in enumerate(_dup_groups(kr)):
            k_out[ig, :, 0:LANES] = kg.astype(k_out.dtype)
            v_out[ig] = _vt_block(vt[ig * NSA_D:(ig + 1) * NSA_D])
    pos = j * ts + lax.broadcasted_iota(jnp.int32, (ts, LANES), 0)
    lane = lax.broadcasted_iota(jnp.int32, (ts, LANES), 1)
    onehot = jnp.where(pos // NSA_SEL_BLOCK == lane, 1.0, 0.0).astype(ks_ref.dtype)
    sig = _sigmoid(sm_ref[...])
    for ig in range(NSA_GROUPS):
        ks_ref[ig, :, LANES:2 * LANES] = onehot
        kw_ref[ig, :, LANES:2 * LANES] = jnp.zeros((ts, LANES), kw_ref.dtype)
        gs_ref[ig] = pltpu.roll(sig, LANES - (gate_col0 + ig * NSA_REP * 3), 1)


def _nsa_prep(nq, kv4, small, tabs, q_norm_w, ks_norm_w, kw_norm_w, gate_col0, ts):
    b, s, _ = nq.shape
    g = NSA_GROUPS
    tile2 = lambda w: jnp.concatenate([w, w]).astype(F32)[None, :]
    ones_blk = np.kron(np.eye(2, dtype=np.float32), np.ones((NSA_D, NSA_D), np.float32))
    full = lambda shape: pl.BlockSpec(shape, lambda ib, ij: (0,) * len(shape))
    tok = lambda w: pl.BlockSpec((None, ts, w), lambda ib, ij: (ib, ij, 0))
    tab = pl.BlockSpec((ts, LANES), lambda ib, ij: (ij, 0))
    kv_out = lambda w: pl.BlockSpec((None, g, ts, w), lambda ib, ij: (ib, 0, ij, 0))
    kv_shape = lambda w, dt: jax.ShapeDtypeStruct((b, g, s, w), dt)
    vt_out = pl.BlockSpec((None, g, NSA_VROWS, ts), lambda ib, ij: (ib, 0, 0, ij))
    vt_shape = jax.ShapeDtypeStruct((b, g, NSA_VROWS, s), BF16)
    return pl.pallas_call(
        functools.partial(_nsa_prep_kernel, gate_col0=gate_col0),
        grid=(b, s // ts),
        in_specs=[tok(NSA_W), tok(4 * LANES), tok(LANES), tab, tab, tab,
                  full((1, LANES)), full((1, LANES)), full((1, LANES)), full((LANES, LANES))],
        out_specs=[tok(NSA_W), kv_out(2 * LANES), vt_out, kv_out(2 * LANES), vt_out, kv_out(LANES)],
        out_shape=[jax.ShapeDtypeStruct((b, s, NSA_W), BF16), kv_shape(2 * LANES, BF16), vt_shape,
                   kv_shape(2 * LANES, BF16), vt_shape, kv_shape(LANES, F32)],
        compiler_params=_cparams(("arbitrary", "arbitrary")),
        name="nsa_prep",
    )(nq, kv4, small, *tabs, tile2(q_norm_w), tile2(ks_norm_w), tile2(kw_norm_w), jnp.asarray(ones_blk))


def _nsa_compress_kernel(kc_ref, vc_ref, tc_ref, ta_ref, tb_ref, pk_ref, pv_ref, kw1_ref, kw2_ref,
                         vw1_ref, vw2_ref, nw_ref, ones_ref, kc_out, vc_out):
    nrow, width = kc_ref.shape
    half = width
    outs = []
    for is_k in (True, False):
        x = (kc_ref if is_k else vc_ref)[...]
        if is_k:
            x = jnp.concatenate(
                [_rope(x[:, p * LANES:(p + 1) * LANES], tc_ref[:, p * LANES:(p + 1) * LANES],
                       ta_ref[:, p * LANES:(p + 1) * LANES], tb_ref[:, p * LANES:(p + 1) * LANES])
                 for p in range(width // LANES)], axis=-1)
        pos_ref, w1_ref, w2_ref = (pk_ref, kw1_ref, kw2_ref) if is_k else (pv_ref, vw1_ref, vw2_ref)
        first = _mm(x + pos_ref[0:1, :], w1_ref[0:half, :])
        second = _mm(x + pos_ref[1:2, :], w1_ref[half:2 * half, :])
        y = first + pltpu.roll(second, nrow - 1, 0)
        if is_k:
            y = _mm(_silu(y), w2_ref[...])
            y = y * lax.rsqrt(_group_ms(y, ones_ref[...]) + EPS) * nw_ref[...]
        else:
            y = _mm_nt(w2_ref[...], _silu(y))
        outs.append(y)
    for ig, kg in enumerate(_dup_groups(outs[0])):
        kc_out[ig] = kg.astype(kc_out.dtype)
    for ig in range(NSA_GROUPS):
        vc_out[ig] = _vt_block(outs[1][ig * NSA_D:(ig + 1) * NSA_D])


def _nsa_compress(kc, vc, tabs, pos_k, pos_v, k_w1, k_w2, v_w1, v_w2, kc_norm_w):
    b, s, _ = kc.shape
    g = NSA_GROUPS
    st = NSA_CMP_STRIDE
    nrow = s // st
    width = st * LANES
    flat = lambda t: t.reshape(t.shape[0], nrow, width)
    eye_g = jnp.eye(g, dtype=F32)

    def w1_blk(w1):
        wl = w1.reshape(NSA_CMP_BLOCK, NSA_D, NSA_D)
        return jnp.einsum("lde,gh->lgdhe", wl, eye_g).reshape(NSA_CMP_BLOCK * LANES, LANES).astype(BF16)

    def w2_blk(w2):
        return jnp.einsum("de,gh->gdhe", w2, eye_g).reshape(LANES, LANES).astype(BF16)

    def pos_rows(p):
        return jnp.concatenate([p, p], axis=-1).reshape(2, width).astype(F32)

    ones_blk = np.kron(np.eye(2, dtype=np.float32), np.ones((NSA_D, NSA_D), np.float32))
    full = lambda shape: pl.BlockSpec(shape, lambda ib: (0,) * len(shape))
    seq = pl.BlockSpec((None, nrow, width), lambda ib: (ib, 0, 0))
    out = pl.BlockSpec((None, g, nrow, LANES), lambda ib: (ib, 0, 0, 0))
    oshape = jax.ShapeDtypeStruct((b, g, nrow, LANES), BF16)
    tabs_r = [t.reshape(nrow, width) for t in tabs]
    return pl.pallas_call(
        _nsa_compress_kernel,
        grid=(b,),
        in_specs=[seq, seq, full((nrow, width)), full((nrow, width)), full((nrow, width)),
                  full((2, width)), full((2, width)),
                  full((2 * width, LANES)), full((LANES, LANES)), full((2 * width, LANES)), full((LANES, LANES)),
                  full((1, LANES)), full((LANES, LANES))],
        out_specs=[out, pl.BlockSpec((None, g, NSA_VROWS, nrow), lambda ib: (ib, 0, 0, 0))],
        out_shape=[oshape, jax.ShapeDtypeStruct((b, g, NSA_VROWS, nrow), BF16)],
        compiler_params=_cparams(("arbitrary",)),
        name="nsa_compress",
    )(flat(kc), flat(vc), *tabs_r, pos_rows(pos_k), pos_rows(pos_v), w1_blk(k_w1), w2_blk(k_w2),
      w1_blk(v_w1), w2_blk(v_w2).T, jnp.concatenate([kc_norm_w, kc_norm_w]).astype(F32)[None, :],
      jnp.asarray(ones_blk))


def _nsa_attn_kernel(q_ref, gs_ref, kc_ref, vct_ref, ks_ref, vst_ref, kw_ref, vwt_ref, ovt_ref, tri_ref, wb_ref,
                     o_ref, *, kb, wlen):
    i = pl.program_id(1)
    tq = NSA_Q_BLOCK
    rep = NSA_REP
    rows = rep * tq
    pairw = 2 * tq
    groups = range(NSA_GROUPS)
    s0 = i * tq
    par = i % 2
    ncmp = kc_ref.shape[1]
    nblk = ks_ref.shape[1] // NSA_SEL_BLOCK
    lsum = slice(NSA_D, NSA_D + 1)

    lane_q = lax.broadcasted_iota(jnp.int32, (tq, LANES), 1)
    qs = []
    for g in groups:
        pieces = []
        for r in range(rep):
            pair = g * (rep // 2) + r // 2
            tile = q_ref[:, pair * LANES:(pair + 1) * LANES]
            keep = (lane_q < NSA_D) if r % 2 == 0 else (lane_q >= NSA_D)
            pieces.append(jnp.where(keep, tile, jnp.zeros_like(tile)))
        qs.append(jnp.concatenate(pieces, axis=0))

    sc_raw = [_mm_nt(kc_ref[g], qs[g]) for g in groups]
    dstart = pl.multiple_of((i - par) * tq, pairw)
    sd = [_mm_nt(ks_ref[g, pl.ds(dstart, pairw), 0:LANES], qs[g]) + tri_ref[par] for g in groups]
    wq_bias = jnp.where(lax.broadcasted_iota(jnp.int32, (rows, LANES), 1) == NSA_D, NEG, 0.0).astype(BF16)
    wstart = pl.multiple_of((i + par) * tq, pairw)
    nwsub = wlen // NSA_SUB_KEYS
    wsub = [pl.ds(wstart + j * NSA_SUB_KEYS, NSA_SUB_KEYS) for j in range(nwsub)]
    sw = [[_mm_nt(kw_ref[g, wsub[j], :], jnp.concatenate([qs[g], wq_bias], axis=1))
           + wb_ref[par, j * NSA_SUB_KEYS:(j + 1) * NSA_SUB_KEYS, :] for j in range(nwsub)] for g in groups]

    cend = (lax.broadcasted_iota(jnp.int32, (ncmp, rows), 0) * NSA_CMP_STRIDE + (NSA_CMP_BLOCK - 1)) - s0
    cmask = cend <= (lax.broadcasted_iota(jnp.int32, (ncmp, rows), 1) & (tq - 1))
    sc = [jnp.where(cmask, sc_raw[g], NEG) for g in groups]
    mc = [jnp.max(sc[g], axis=0, keepdims=True) for g in groups]
    pc = [jnp.where(cmask, jnp.exp2(sc[g] - mc[g]), 0.0) for g in groups]
    acc_c = [_mm(vct_ref[g], pc[g]) for g in groups]
    inv_c = [1.0 / jnp.maximum(acc_c[g][lsum, :], 1e-30) for g in groups]

    m0 = [jnp.max(sd[g], axis=0, keepdims=True) for g in groups]
    pd = [jnp.exp2(sd[g] - m0[g]) for g in groups]
    a0 = [_mm(vst_ref[g, :, pl.ds(dstart, pairw)], pd[g]) for g in groups]

    acc_w = []
    for g in groups:
        mw = sw[g][0]
        for j in range(1, nwsub):
            mw = jnp.maximum(mw, sw[g][j])
        mw = jnp.max(mw, axis=0, keepdims=True)
        a = _mm(vwt_ref[g, :, wsub[0]], jnp.exp2(sw[g][0] - mw))
        for j in range(1, nwsub):
            a = a + _mm(vwt_ref[g, :, wsub[j]], jnp.exp2(sw[g][j] - mw))
        acc_w.append(a)

    gates, o_cw = [], []
    for g in groups:
        gt = gs_ref[g].T
        gates.append([jnp.concatenate([gt[r * 3 + x:r * 3 + x + 1, :] for r in range(rep)], axis=1)
                      for x in range(3)])
        o_cw.append((gates[g][0] * inv_c[g]) * acc_c[g] + (gates[g][2] / acc_w[g][lsum, :]) * acc_w[g])

    lane_g = lax.broadcasted_iota(jnp.int32, (LANES, LANES), 1)
    imp = None
    for g in groups:
        pn = pc[g] * inv_c[g]
        folded = pn[:, 0:LANES] + pn[:, LANES:2 * LANES]
        psum = folded + pltpu.roll(folded, tq, 1)
        imp_g = _dot_exact_lhs(ovt_ref[...], psum)
        imp = imp_g if imp is None else jnp.where(lane_g < g * tq, imp, imp_g)
    imp = imp[0:tq]
    blk = lax.broadcasted_iota(jnp.int32, (tq, LANES), 0)
    valid = blk <= i

    def ranked():
        forced = (blk == 0) | (blk == i) | (blk == i - 1)
        key = jnp.where(valid, jnp.where(forced, 0x7F000000, pltpu.bitcast(imp, jnp.int32)), -1)
        key = jnp.where(blk < nblk, key, -2)
        key_m1 = key - 1
        rank = jnp.zeros((tq, LANES), jnp.int32)
        for jb in range(nblk):
            ahead = key[jb:jb + 1, :] > jnp.where(blk > jb, key_m1, key)
            rank = rank + jnp.where(ahead, 1, 0)
        return jnp.where(rank < NSA_N_SEL, 1.0, 0.0)

    sel = lax.cond(i >= NSA_N_SEL, ranked, lambda: jnp.where(valid, 1.0, 0.0))
    selneg_t = jnp.where((sel > 0.5) & (blk < i - par), 0.0, NEG)
    selneg = jnp.concatenate([selneg_t, jnp.zeros((LANES - tq, LANES), F32)], axis=0).T
    q2 = []
    for g in groups:
        bias = selneg[g * tq:(g + 1) * tq].astype(BF16)
        q2.append(jnp.concatenate([qs[g], jnp.concatenate([bias] * rep, axis=0)], axis=1))

    nsub = kb // NSA_SUB_KEYS

    def slc_body(ic, carry):
        m_i, acc = carry
        k0 = pl.multiple_of(ic * kb, kb)
        sub_rows = [pl.ds(k0 + j * NSA_SUB_KEYS, NSA_SUB_KEYS) for j in range(nsub)]
        s = [[_mm_nt(ks_ref[g, sub_rows[j], :], q2[g]) for j in range(nsub)] for g in groups]
        m_out, acc_out = [], []
        for g in groups:
            m_chunk = s[g][0]
            for j in range(1, nsub):
                m_chunk = jnp.maximum(m_chunk, s[g][j])
            m_new = jnp.maximum(m_i[g], jnp.max(m_chunk, axis=0, keepdims=True))
            a = jnp.exp2(m_i[g] - m_new) * acc[g]
            for j in range(nsub):
                a = a + _mm(vst_ref[g, :, sub_rows[j]], jnp.exp2(s[g][j] - m_new))
            m_out.append(m_new)
            acc_out.append(a)
        return tuple(m_out), tuple(acc_out)

    nch = ((i - par) * tq + kb - 1) // kb
    _, acc_s = lax.fori_loop(0, nch, slc_body, (tuple(m0), tuple(a0)))

    lo = lane_q < NSA_D
    for g in groups:
        ot = (o_cw[g] + (gates[g][1] / acc_s[g][lsum, :]) * acc_s[g])
        ot = jnp.concatenate([ot, jnp.zeros((LANES - NSA_VROWS, rows), F32)], axis=0)
        heads = []
        for hp in range(rows // LANES):
            o_pair = ot[:, hp * LANES:(hp + 1) * LANES].T
            heads += [o_pair[0:tq], o_pair[tq:2 * tq]]
        for pr_ in range(rep // 2):
            pair = g * (rep // 2) + pr_
            o_ref[:, pair * LANES:(pair + 1) * LANES] = jnp.where(
                lo, heads[2 * pr_], pltpu.roll(heads[2 * pr_ + 1], NSA_D, 1)).astype(o_ref.dtype)


def _nsa_attn(q, gsig, kcmp, vcmp_t, ks3, vs_t, kw3, vw_t):
    b, s, _ = q.shape
    g = NSA_GROUPS
    rep = NSA_REP
    tq = NSA_Q_BLOCK
    rows = rep * tq
    ncmp = kcmp.shape[2]
    nblk = s // NSA_SEL_BLOCK
    assert nblk <= tq and nblk <= NSA_D and (s // tq) % 2 == 0
    kb = min(1024, s)
    wlen = NSA_WINDOW + 4 * tq
    wpad = wlen - 2 * tq
    ci = np.arange(ncmp) * NSA_CMP_STRIDE
    sj = np.arange(nblk) * NSA_SEL_BLOCK
    ov = np.clip(np.minimum(ci[None, :] + NSA_CMP_BLOCK, sj[:, None] + NSA_SEL_BLOCK)
                 - np.maximum(ci[None, :], sj[:, None]), 0, None).astype(np.float32) / NSA_CMP_STRIDE
    ov_t = np.zeros((LANES, ncmp), np.float32)
    ov_t[:nblk] = ov
    tloc = (np.arange(rows) % tq)[None, :]
    kcol = np.arange(2 * tq)[:, None]
    tri = np.stack([
        np.where(kcol <= tloc, 0.0, NEG),
        np.where(kcol < tq, 0.0, np.where(kcol - tq <= tloc, 0.0, NEG)),
    ]).astype(np.float32)
    wcol = np.arange(wlen)[:, None]
    wb = np.stack([
        np.where((wcol <= wpad - par * tq + tloc) & (wcol > wpad - par * tq - NSA_WINDOW + tloc), 0.0, NEG)
        for par in range(2)]).astype(np.float32)
    front = np.zeros((wpad, 2 * LANES), np.float32)
    front[:, LANES + NSA_D] = 1.0
    kw_p = jnp.concatenate([jnp.broadcast_to(jnp.asarray(front, dtype=BF16), (b, g, wpad, 2 * LANES)), kw3,
                            jnp.zeros((b, g, 2 * tq, 2 * LANES), BF16)], axis=2)
    vw_p = jnp.pad(vw_t, ((0, 0), (0, 0), (0, 0), (wpad, 2 * tq)))
    sp = s + wpad + 2 * tq
    seq = lambda n, w: pl.BlockSpec((None, g, n, w), lambda ib, ii: (ib, 0, 0, 0))
    full = lambda shape: pl.BlockSpec(shape, lambda ib, ii: (0,) * len(shape))
    return pl.pallas_call(
        functools.partial(_nsa_attn_kernel, kb=kb, wlen=wlen),
        grid=(b, s // tq),
        in_specs=[
            pl.BlockSpec((None, tq, NSA_W), lambda ib, ii: (ib, ii, 0)),
            pl.BlockSpec((None, g, tq, LANES), lambda ib, ii: (ib, 0, ii, 0)),
            seq(ncmp, LANES), seq(NSA_VROWS, ncmp), seq(s, 2 * LANES), seq(NSA_VROWS, s),
            seq(sp, 2 * LANES), seq(NSA_VROWS, sp),
            full((LANES, ncmp)), full((2, 2 * tq, rows)), full((2, wlen, rows)),
        ],
        out_specs=pl.BlockSpec((None, tq, NSA_W), lambda ib, ii: (ib, ii, 0)),
        out_shape=jax.ShapeDtypeStruct((b, s, NSA_W), BF16),
        compiler_params=_cparams(("arbitrary", "arbitrary")),
        name="nsa_attn",
    )(q, gsig, kcmp, vcmp_t, ks3, vs_t, kw_p, vw_p, jnp.asarray(ov_t), jnp.asarray(tri), jnp.asarray(wb))


def _mem_kv_kernel(mem_ref, nw_ref, w_ref, knw_ref, k_ref, v_ref):
    x = mem_ref[...]
    ms = jnp.mean(x * x, axis=-1, keepdims=True)
    xn = x * lax.rsqrt(ms + EPS) * nw_ref[...]
    kv = _mm(xn, w_ref[...])
    v_ref[...] = kv[:, MEM_W:].astype(v_ref.dtype)
    for h in range(MEM_HEADS):
        hs = slice(h * MEM_D, (h + 1) * MEM_D)
        kh = kv[:, hs]
        msk = jnp.mean(kh * kh, axis=-1, keepdims=True)
        k_ref[:, hs] = (kh * lax.rsqrt(msk + EPS) * knw_ref[...]).astype(k_ref.dtype)


def _mem_kv(mem, mem_norm_w, w_kv, k_norm_w):
    b, m, d = mem.shape
    full = lambda shape: pl.BlockSpec(shape, lambda ib: (0,) * len(shape))
    blk = lambda w: pl.BlockSpec((None, m, w), lambda ib: (ib, 0, 0))
    return pl.pallas_call(
        _mem_kv_kernel,
        grid=(b,),
        in_specs=[blk(d), full((1, d)), full((d, 2 * MEM_W)), full((1, MEM_D))],
        out_specs=[blk(MEM_W), blk(MEM_W)],
        out_shape=[jax.ShapeDtypeStruct((b, m, MEM_W), BF16)] * 2,
        compiler_params=_cparams(("arbitrary",)),
        name="mem_kv",
    )(mem, mem_norm_w.astype(F32)[None, :], w_kv.astype(BF16), k_norm_w.astype(F32)[None, :])


def _mem_attn_kernel(q_ref, k_ref, v_ref, qnw_ref, o_ref):
    scale = MEM_D ** -0.5
    for h in range(MEM_HEADS):
        hs = slice(h * MEM_D, (h + 1) * MEM_D)
        qh = q_ref[:, hs]
        ms = jnp.mean(qh * qh, axis=-1, keepdims=True)
        qn = qh * lax.rsqrt(ms + EPS) * qnw_ref[...]
        s = _mm_nt(qn, k_ref[:, hs]) * scale
        m = jnp.max(s, axis=-1, keepdims=True)
        p = jnp.exp(s - m)
        l = jnp.sum(p, axis=-1, keepdims=True)
        o_ref[:, hs] = (_mm(p, v_ref[:, hs]) * (1.0 / l)).astype(o_ref.dtype)


def _mem_attn(mq, k, v, q_norm_w, ts):
    b, s, _ = mq.shape
    m = k.shape[1]
    return pl.pallas_call(
        _mem_attn_kernel,
        grid=(b, s // ts),
        in_specs=[
            pl.BlockSpec((None, ts, MEM_W), lambda ib, ij: (ib, ij, 0)),
            pl.BlockSpec((None, m, MEM_W), lambda ib, ij: (ib, 0, 0)),
            pl.BlockSpec((None, m, MEM_W), lambda ib, ij: (ib, 0, 0)),
            pl.BlockSpec((1, MEM_D), lambda ib, ij: (0, 0)),
        ],
        out_specs=pl.BlockSpec((None, ts, MEM_W), lambda ib, ij: (ib, ij, 0)),
        out_shape=jax.ShapeDtypeStruct((b, s, MEM_W), BF16),
        compiler_params=_cparams(("arbitrary", "arbitrary")),
        name="mem_attn",
    )(mq, k, v, q_norm_w.astype(F32)[None, :])


def _out_proj_kernel(x_ref, oa_ref, ob_ref, oc_ref, w_ref, h_ref):
    acc = x_ref[...]
    off = 0
    for o_ref in (oa_ref, ob_ref, oc_ref):
        wd = o_ref.shape[-1]
        acc = acc + jnp.dot(o_ref[...].astype(BF16), w_ref[off:off + wd, :], preferred_element_type=F32)
        off += wd
    h_ref[...] = acc


def _out_proj(x2, oa, ob, oc, w_out, tm):
    t, d = x2.shape
    row = lambda w: pl.BlockSpec((tm, w), lambda i: (i, 0))
    return pl.pallas_call(
        _out_proj_kernel,
        grid=(t // tm,),
        in_specs=[row(d), row(oa.shape[1]), row(ob.shape[1]), row(oc.shape[1]),
                  pl.BlockSpec(w_out.shape, lambda i: (0, 0))],
        out_specs=row(d),
        out_shape=jax.ShapeDtypeStruct((t, d), F32),
        compiler_params=_cparams(("arbitrary",)),
        name="out_proj",
    )(x2, oa, ob, oc, w_out)


def _ffn_kernel(h_ref, halo_ref, nw_ref, wup_ref, cw_ref, wdn_ref, o_ref, hn_s, u_s, act_s, *, fc):
    j = pl.program_id(1)
    ts = h_ref.shape[0]
    f = wdn_ref.shape[0]
    hl = halo_ref.shape[0]

    def norm(x):
        ms = jnp.mean(x * x, axis=-1, keepdims=True)
        return (x * lax.rsqrt(ms + EPS) * nw_ref[...]).astype(BF16)

    halo = jnp.where(j > 0, halo_ref[...], 0.0)
    hn_s[0:hl, :] = norm(halo)
    hn_s[hl:hl + ts, :] = norm(h_ref[...])
    for ic in range(f // fc):
        hn = hn_s[...]
        slot = ic % 2
        for part in range(2):
            cols = slice(part * f + ic * fc, part * f + (ic + 1) * fc)
            u_s[slot, part] = jnp.dot(hn, wup_ref[:, cols], preferred_element_type=F32)
        conv = []
        for part in range(2):
            cols = slice(part * f + ic * fc, part * f + (ic + 1) * fc)
            acc = cw_ref[FFN_CONV - 1:FFN_CONV, cols] * u_s[slot, part, hl:hl + ts, :]
            for jj in range(FFN_CONV - 1):
                acc = acc + cw_ref[jj:jj + 1, cols] * u_s[slot, part, pl.ds(hl - (FFN_CONV - 1) + jj, ts), :]
            conv.append(acc)
        act_s[:, ic * fc:(ic + 1) * fc] = (_silu(conv[0]) * conv[1]).astype(BF16)
    o_ref[...] = h_ref[...] + jnp.dot(act_s[...], wdn_ref[...], preferred_element_type=F32)


def _ffn(h, norm_w, w_up, conv_w, w_down, ts, fc):
    b, s, d = h.shape
    f = w_down.shape[0]
    full = lambda shape: pl.BlockSpec(shape, lambda ib, ij: (0,) * len(shape))
    return pl.pallas_call(
        functools.partial(_ffn_kernel, fc=fc),
        grid=(b, s // ts),
        in_specs=[
            pl.BlockSpec((None, ts, d), lambda ib, ij: (ib, ij, 0)),
            pl.BlockSpec((None, HALO_BF16, d), lambda ib, ij: (ib, jnp.maximum(ij * (ts // HALO_BF16) - 1, 0), 0)),
            full((1, d)), full((d, 2 * f)), full((FFN_CONV, 2 * f)), full((f, d)),
        ],
        out_specs=pl.BlockSpec((None, ts, d), lambda ib, ij: (ib, ij, 0)),
        out_shape=jax.ShapeDtypeStruct((b, s, d), F32),
        scratch_shapes=[
            pltpu.VMEM((ts + HALO_BF16, d), BF16),
            pltpu.VMEM((2, 2, ts + HALO_BF16, fc), F32),
            pltpu.VMEM((ts, f), BF16),
        ],
        compiler_params=_cparams(("arbitrary", "arbitrary")),
        name="ffn",
    )(h, h, norm_w.astype(F32)[None, :], w_up, conv_w.astype(F32), w_down)


def _split_w_in(w_in):
    sizes = (3 * GDN_W, GDN_HEADS, GDN_HEADS, GDN_W, NSA_W, NSA_KV_W, NSA_KV_W, NSA_KV_W, NSA_KV_W,
             NSA_KV_W, NSA_KV_W, 3 * NSA_HEADS, MEM_W)
    offs = np.concatenate([[0], np.cumsum(sizes)])
    (qkv, a, bb, gate, nq, kc, vc, ks, vs, kw, vw, ng, mq) = [w_in[:, offs[i]:offs[i + 1]] for i in range(len(sizes))]
    n_small = 2 * GDN_HEADS + 3 * NSA_HEADS
    small = jnp.concatenate([a, bb, ng, jnp.zeros((w_in.shape[0], LANES - n_small), w_in.dtype)], axis=1)
    widths = (3 * GDN_W, GDN_W, NSA_W, NSA_KV_W, NSA_KV_W, 4 * NSA_KV_W, MEM_W, LANES)
    w_cat = jnp.concatenate([qkv, gate, nq, kc, vc, ks, vs, kw, vw, mq, small], axis=1).astype(BF16)
    return w_cat, widths


def _layer(x, mem, attn_norm_w, mem_norm_w, w_in, gdn_conv_w, gdn_a_log, gdn_dt_bias, gdn_out_norm_w,
           nsa_q_norm_w, nsa_kc_norm_w, nsa_ks_norm_w, nsa_kw_norm_w, nsa_cmp_pos_k, nsa_cmp_pos_v,
           nsa_cmp_k_w1, nsa_cmp_k_w2, nsa_cmp_v_w1, nsa_cmp_v_w2, mem_w_kv, mem_q_norm_w, mem_k_norm_w,
           w_out, ffn_norm_w, ffn_w_up, ffn_conv_w, ffn_w_down):
    b, s, d = x.shape
    t = b * s
    ts = min(512, s)
    x2 = x.reshape(t, d)

    w_cat, widths = _split_w_in(w_in)
    qkv, gate, nq, kc, vc, kv4, mq, small = _in_proj(x2, attn_norm_w.astype(F32)[None, :], w_cat, widths, ts)
    r3 = lambda a: a.reshape(b, s, a.shape[-1])

    o_a = _gdn(r3(qkv), r3(small), r3(gate), gdn_conv_w, gdn_a_log, gdn_dt_bias, gdn_out_norm_w, ts)

    tabs = _rope_tables(s)
    q_r, ks3, vs2, kw2, vw2, gsig = _nsa_prep(r3(nq), r3(kv4), r3(small), tabs, nsa_q_norm_w, nsa_ks_norm_w,
                                              nsa_kw_norm_w, 2 * GDN_HEADS, ts)
    kcmp, vcmp = _nsa_compress(r3(kc), r3(vc), tabs, nsa_cmp_pos_k, nsa_cmp_pos_v, nsa_cmp_k_w1, nsa_cmp_k_w2,
                               nsa_cmp_v_w1, nsa_cmp_v_w2, nsa_kc_norm_w)
    o_b = _nsa_attn(q_r, gsig, kcmp, vcmp, ks3, vs2, kw2, vw2)

    mk, mv = _mem_kv(mem, mem_norm_w, mem_w_kv, mem_k_norm_w)
    o_c = _mem_attn(r3(mq), mk, mv, mem_q_norm_w, ts)

    h = _out_proj(x2, o_a.reshape(t, GDN_W), o_b.reshape(t, NSA_W), o_c.reshape(t, MEM_W), w_out.astype(BF16), ts)
    out = _ffn(h.reshape(b, s, d), ffn_norm_w, ffn_w_up.astype(BF16), ffn_conv_w, ffn_w_down.astype(BF16), ts, 256)
    return out


def kernel(x, mem, attn_norm_w, mem_norm_w, w_in, gdn_conv_w, gdn_a_log, gdn_dt_bias, gdn_out_norm_w, nsa_q_norm_w, nsa_kc_norm_w, nsa_ks_norm_w, nsa_kw_norm_w, nsa_cmp_pos_k, nsa_cmp_pos_v, nsa_cmp_k_w1, nsa_cmp_k_w2, nsa_cmp_v_w1, nsa_cmp_v_w2, mem_w_kv, mem_q_norm_w, mem_k_norm_w, w_out, ffn_norm_w, ffn_w_up, ffn_conv_w, ffn_w_down):
    h = x
    for l in range(w_in.shape[0]):
        h = _layer(h, mem, attn_norm_w[l], mem_norm_w[l], w_in[l], gdn_conv_w[l], gdn_a_log[l], gdn_dt_bias[l],
                   gdn_out_norm_w[l], nsa_q_norm_w[l], nsa_kc_norm_w[l], nsa_ks_norm_w[l], nsa_kw_norm_w[l],
                   nsa_cmp_pos_k[l], nsa_cmp_pos_v[l], nsa_cmp_k_w1[l], nsa_cmp_k_w2[l], nsa_cmp_v_w1[l],
                   nsa_cmp_v_w2[l], mem_w_kv[l], mem_q_norm_w[l], mem_k_norm_w[l], w_out[l], ffn_norm_w[l],
                   ffn_w_up[l], ffn_conv_w[l], ffn_w_down[l])
    return h
```

```python
import functools

import jax
import jax.numpy as jnp
import numpy as np
from jax import lax
from jax.experimental import pallas as pl
from jax.experimental.pallas import tpu as pltpu

F32 = jnp.float32
BF16 = jnp.bfloat16

EPS = 1e-6
ROPE_THETA = 500000.0
GDN_HEADS = 4
GDN_D = 128
GDN_CONV = 4
GDN_CHUNK = 64
GDN_SUB = 16
GDN_GROUP = 4
NSA_HEADS = 8
NSA_GROUPS = 2
NSA_REP = NSA_HEADS // NSA_GROUPS
NSA_D = 64
NSA_CMP_BLOCK = 32
NSA_CMP_STRIDE = 16
NSA_SEL_BLOCK = 64
NSA_N_SEL = 16
NSA_WINDOW = 512
NSA_Q_BLOCK = 64
NSA_ROPE_DIM = NSA_D // 4
NSA_SUB_KEYS = 256
NSA_VROWS = NSA_D + 16
MEM_HEADS = 4
MEM_D = 128
FFN_CONV = 3

GDN_W = GDN_HEADS * GDN_D
NSA_W = NSA_HEADS * NSA_D
MEM_W = MEM_HEADS * MEM_D
NSA_KV_W = NSA_GROUPS * NSA_D

LANES = 128
HALO = 8
HALO_BF16 = 16
VMEM_LIMIT = 56 * 1024 * 1024
NEG = -1e30
LOG2E = 1.4426950408889634


def _cparams(sem):
    return pltpu.CompilerParams(dimension_semantics=sem, vmem_limit_bytes=VMEM_LIMIT)


def _mm(a, b):
    return jnp.dot(a.astype(BF16), b.astype(BF16), preferred_element_type=F32)


def _mm_nt(a, b):
    return lax.dot_general(a.astype(BF16), b.astype(BF16), (((1,), (1,)), ((), ())),
                           preferred_element_type=F32)


def _mm_tn(a, b):
    return lax.dot_general(a.astype(BF16), b.astype(BF16), (((0,), (0,)), ((), ())),
                           preferred_element_type=F32)


def _split3(x):
    hi = x.astype(BF16)
    r = x - hi.astype(F32)
    mid = r.astype(BF16)
    lo = (r - mid.astype(F32)).astype(BF16)
    return hi, mid, lo


def _dot_exact_rhs(x, e):
    hi, mid, lo = _split3(x)
    eb = e.astype(BF16)
    return (jnp.dot(hi, eb, preferred_element_type=F32) + jnp.dot(mid, eb, preferred_element_type=F32)
            + jnp.dot(lo, eb, preferred_element_type=F32))


def _dot_exact_lhs(e, x):
    hi, mid, lo = _split3(x)
    eb = e.astype(BF16)
    return (jnp.dot(eb, hi, preferred_element_type=F32) + jnp.dot(eb, mid, preferred_element_type=F32)
            + jnp.dot(eb, lo, preferred_element_type=F32))


def _sigmoid(x):
    return 1.0 / (1.0 + jnp.exp2(x * (-LOG2E)))


def _silu(x):
    return x * _sigmoid(x)


def _softplus(x):
    return jnp.maximum(x, 0.0) + jnp.log(1.0 + jnp.exp(-jnp.abs(x)))


def _in_proj_kernel(x_ref, nw_ref, w_ref, *o_refs):
    x = x_ref[...]
    ms = jnp.mean(x * x, axis=-1, keepdims=True)
    xn = (x * lax.rsqrt(ms + EPS) * nw_ref[...]).astype(BF16)
    off = 0
    for o_ref in o_refs:
        wd = o_ref.shape[-1]
        o_ref[...] = jnp.dot(xn, w_ref[:, off:off + wd], preferred_element_type=F32).astype(o_ref.dtype)
        off += wd


def _in_proj(x2, norm_w, w_cat, widths, tm):
    t, d = x2.shape
    wtot = w_cat.shape[1]
    return pl.pallas_call(
        _in_proj_kernel,
        grid=(t // tm,),
        in_specs=[
            pl.BlockSpec((tm, d), lambda i: (i, 0)),
            pl.BlockSpec((1, d), lambda i: (0, 0)),
            pl.BlockSpec((d, wtot), lambda i: (0, 0)),
        ],
        out_specs=[pl.BlockSpec((tm, wd), lambda i: (i, 0)) for wd in widths],
        out_shape=[jax.ShapeDtypeStruct((t, wd), F32) for wd in widths],
        compiler_params=_cparams(("arbitrary",)),
        name="in_proj",
    )(x2, norm_w, w_cat)


def _gdn_kernel(qkv_ref, sm_ref, gate_ref, cw_ref, alog_ref, dtb_ref, onw_ref, ea_ref, eb_ref, ltri_ref,
                o_ref, xb, qn_s, kn_s, v_s, g_s, beta_s, u_s, w_s, aqk_s, egl_s, oacc_s, state_s):
    j = pl.program_id(1)
    ts = o_ref.shape[0]
    c = GDN_CHUNK
    hw = GDN_W

    @pl.when(j == 0)
    def _():
        xb[0:HALO, :] = jnp.zeros((HALO, 3 * hw), F32)
        state_s[...] = jnp.zeros_like(state_s)

    xb[HALO:HALO + ts, :] = qkv_ref[...]
    for part in range(3):
        cols = slice(part * hw, (part + 1) * hw)
        acc = cw_ref[GDN_CONV - 1:GDN_CONV, cols] * xb[HALO:HALO + ts, cols]
        for jj in range(GDN_CONV - 1):
            acc = acc + cw_ref[jj:jj + 1, cols] * xb[pl.ds(HALO - (GDN_CONV - 1) + jj, ts), cols]
        act = _silu(acc)
        if part == 2:
            v_s[...] = act
        else:
            dst = qn_s if part == 0 else kn_s
            scale = GDN_D ** -0.5 if part == 0 else 1.0
            for h in range(GDN_HEADS):
                hs = slice(h * GDN_D, (h + 1) * GDN_D)
                xh = act[:, hs]
                ss = jnp.sum(xh * xh, axis=-1, keepdims=True)
                dst[:, hs] = xh * (lax.rsqrt(ss + EPS) * scale)
    xb[0:HALO, :] = xb[ts:ts + HALO, :]

    sm = sm_ref[...]
    a_full = _dot_exact_rhs(sm, ea_ref[...])
    b_full = _dot_exact_rhs(sm, eb_ref[...])
    g_s[...] = -jnp.exp(alog_ref[...]) * _softplus(a_full + dtb_ref[...])
    beta_s[...] = _sigmoid(b_full)

    ri = lax.broadcasted_iota(jnp.int32, (c, c), 0)
    ci = lax.broadcasted_iota(jnp.int32, (c, c), 1)
    causal = ri >= ci
    strict = ri > ci
    blockdiag = (ri // GDN_SUB) == (ci // GDN_SUB)
    ltri = ltri_ref[...]
    heads = [slice(h * GDN_D, (h + 1) * GDN_D) for h in range(GDN_HEADS)]

    def precompute(ig, carry):
        base = pl.multiple_of(ig * (GDN_GROUP * c), GDN_GROUP * c)
        kb_l, kn_l, qn_l, rhs_l, decay_l, where_l = [], [], [], [], [], []
        for cc in range(GDN_GROUP):
            rows = pl.ds(base + cc * c, c)
            gc = _dot_exact_lhs(ltri, g_s[rows, :])
            glast = gc[c - 1:c, :]
            eg = jnp.exp(gc)
            beta = beta_s[rows, :]
            kn = kn_s[rows, :]
            qn = qn_s[rows, :]
            kb = kn * beta
            vb = v_s[rows, :] * beta
            kbe = kb * eg
            qn_s[rows, :] = qn * eg
            kn_s[rows, :] = kn * jnp.exp(glast - gc)
            egl_s[pl.ds(pl.multiple_of((ig * GDN_GROUP + cc) * HALO, HALO), HALO), :] = jnp.broadcast_to(
                jnp.exp(glast), (HALO, hw))
            for h, hs in enumerate(heads):
                gcol = gc[:, h * GDN_D:h * GDN_D + c]
                grow = gc[:, hs].T[0:1, 0:c]
                diff = gcol - grow
                decay_l.append(jnp.where(causal, jnp.exp(jnp.where(causal, diff, 0.0)), 0.0))
                kb_l.append(kb[:, hs].astype(BF16))
                kn_l.append(kn[:, hs].astype(BF16))
                qn_l.append(qn[:, hs].astype(BF16))
                rhs_l.append(jnp.concatenate([vb[:, hs], kbe[:, hs]], axis=-1))
                where_l.append((rows, h, hs))
        n = len(where_l)
        kk = [_mm_nt(kb_l[i], kn_l[i]) for i in range(n)]
        qk = [_mm_nt(qn_l[i], kn_l[i]) for i in range(n)]
        for i, (rows, h, hs) in enumerate(where_l):
            aqk_s[rows, h * c:(h + 1) * c] = qk[i] * decay_l[i]
        p = [-jnp.where(strict, kk[i] * decay_l[i], 0.0) for i in range(n)]
        pd = [jnp.where(blockdiag, x, 0.0) for x in p]
        pn = [p[i] - pd[i] for i in range(n)]
        p2 = [_mm(x, x) for x in pd]
        p4 = [_mm(x, x) for x in p2]
        p8 = [_mm(x, x) for x in p4]
        a1 = [pd[i] + p2[i] + _mm(pd[i], p2[i]) for i in range(n)]
        a2 = [a1[i] + p4[i] + _mm(a1[i], p4[i]) for i in range(n)]
        a3 = [a2[i] + p8[i] + _mm(a2[i], p8[i]) for i in range(n)]
        nm = [pn[i] + _mm(a3[i], pn[i]) for i in range(n)]
        n2 = [_mm(x, x) for x in nm]
        bm = [nm[i] + n2[i] + _mm(nm[i], n2[i]) for i in range(n)]
        tm = [bm[i] + a3[i] + _mm(bm[i], a3[i]) for i in range(n)]
        for i, (rows, h, hs) in enumerate(where_l):
            sol = rhs_l[i] + _mm(tm[i], rhs_l[i])
            u_s[rows, hs] = sol[:, :GDN_D]
            w_s[rows, hs] = sol[:, GDN_D:]
        return carry

    lax.fori_loop(0, ts // (GDN_GROUP * c), precompute, 0)

    def scan_body(ic, carry):
        rows = pl.ds(pl.multiple_of(ic * c, c), c)
        egl = egl_s[pl.ds(pl.multiple_of(ic * HALO, HALO), 1), :]
        st = [state_s[h] for h in range(GDN_HEADS)]
        stb = [x.astype(BF16) for x in st]
        ws = [_mm(w_s[rows, hs], stb[h]) for h, hs in enumerate(heads)]
        qs = [_mm(qn_s[rows, hs], stb[h]) for h, hs in enumerate(heads)]
        v_new = [u_s[rows, hs] - ws[h] for h, hs in enumerate(heads)]
        for h, hs in enumerate(heads):
            oacc_s[rows, hs] = qs[h] + _mm(aqk_s[rows, h * c:(h + 1) * c], v_new[h])
            state_s[h] = st[h] * egl[:, hs] + _mm_tn(kn_s[rows, hs], v_new[h])
        return carry

    lax.fori_loop(0, ts // c, scan_body, 0)

    for h in range(GDN_HEADS):
        hs = slice(h * GDN_D, (h + 1) * GDN_D)
        oh = oacc_s[:, hs]
        ms = jnp.mean(oh * oh, axis=-1, keepdims=True)
        o_ref[:, hs] = (oh * lax.rsqrt(ms + EPS) * onw_ref[...] * _silu(gate_ref[:, hs])).astype(o_ref.dtype)


def _gdn(qkv, small, gate, conv_w, a_log, dt_bias, out_norm_w, ts):
    b, s, _ = qkv.shape
    hw = GDN_W
    c = GDN_CHUNK
    rep = lambda v: jnp.repeat(v.astype(F32), GDN_D)[None, :]
    lane_head = np.arange(hw) // GDN_D
    ea = (np.arange(LANES)[:, None] == lane_head[None, :]).astype(np.float32)
    eb = (np.arange(LANES)[:, None] == (lane_head[None, :] + GDN_HEADS)).astype(np.float32)
    ltri = np.tril(np.ones((c, c), np.float32))
    full = lambda shape: pl.BlockSpec(shape, lambda ib, ij: (0,) * len(shape))
    return pl.pallas_call(
        _gdn_kernel,
        grid=(b, s // ts),
        in_specs=[
            pl.BlockSpec((None, ts, 3 * hw), lambda ib, ij: (ib, ij, 0)),
            pl.BlockSpec((None, ts, LANES), lambda ib, ij: (ib, ij, 0)),
            pl.BlockSpec((None, ts, hw), lambda ib, ij: (ib, ij, 0)),
            full((GDN_CONV, 3 * hw)), full((1, hw)), full((1, hw)), full((1, GDN_D)),
            full((LANES, hw)), full((LANES, hw)), full((c, c)),
        ],
        out_specs=pl.BlockSpec((None, ts, hw), lambda ib, ij: (ib, ij, 0)),
        out_shape=jax.ShapeDtypeStruct((b, s, hw), BF16),
        scratch_shapes=[
            pltpu.VMEM((ts + HALO, 3 * hw), F32),
            pltpu.VMEM((ts, hw), F32), pltpu.VMEM((ts, hw), F32), pltpu.VMEM((ts, hw), F32),
            pltpu.VMEM((ts, hw), F32), pltpu.VMEM((ts, hw), F32),
            pltpu.VMEM((ts, hw), F32), pltpu.VMEM((ts, hw), F32), pltpu.VMEM((ts, GDN_HEADS * c), F32),
            pltpu.VMEM((ts // c * HALO, hw), F32),
            pltpu.VMEM((ts, hw), F32),
            pltpu.VMEM((GDN_HEADS, GDN_D, GDN_D), F32),
        ],
        compiler_params=_cparams(("arbitrary", "arbitrary")),
        name="gdn",
    )(qkv, small, gate, conv_w.astype(F32), rep(a_log), rep(dt_bias), out_norm_w.astype(F32)[None, :],
      jnp.asarray(ea), jnp.asarray(eb), jnp.asarray(ltri))


def _rope_tables(s):
    half = NSA_ROPE_DIM // 2
    pos = jnp.arange(s, dtype=F32)
    inv = 1.0 / (ROPE_THETA ** (jnp.arange(0, NSA_ROPE_DIM, 2, dtype=F32) / NSA_ROPE_DIM))
    ang = pos[:, None] * inv[None, :]
    cos, sin = jnp.cos(ang), jnp.sin(ang)
    one = jnp.ones((s, NSA_D - NSA_ROPE_DIM), F32)
    zero = jnp.zeros((s, NSA_D - NSA_ROPE_DIM), F32)
    zh = jnp.zeros((s, half), F32)
    tc = jnp.concatenate([cos, cos, one], axis=-1)
    ta = jnp.concatenate([-sin, zh, zero], axis=-1)
    tb = jnp.concatenate([zh, sin, zero], axis=-1)
    dup = lambda t: jnp.concatenate([t, t], axis=-1)
    return dup(tc), dup(ta), dup(tb)


def _rope(x, tc, ta, tb):
    half = NSA_ROPE_DIM // 2
    return x * tc + pltpu.roll(x, LANES - half, 1) * ta + pltpu.roll(x, half, 1) * tb


def _group_ms(x, ones_blk):
    return _dot_exact_rhs(x * x, ones_blk) * (1.0 / NSA_D)


def _dup_groups(x):
    r = pltpu.roll(x, NSA_D, 1)
    lane = lax.broadcasted_iota(jnp.int32, x.shape, 1)
    lo = lane < NSA_D
    return jnp.where(lo, x, r), jnp.where(lo, r, x)


def _vt_block(vt):
    n = vt.shape[1]
    tail = jnp.where(lax.broadcasted_iota(jnp.int32, (NSA_VROWS - NSA_D, n), 0) == 0, 1.0, 0.0)
    return jnp.concatenate([vt, tail], axis=0).astype(BF16)


def _nsa_prep_kernel(nq_ref, kv_ref, sm_ref, tc_ref, ta_ref, tb_ref, qw_ref, ksw_ref, kww_ref, ones_ref,
                     q_ref, ks_ref, vs_ref, kw_ref, vw_ref, gs_ref, *, gate_col0):
    j = pl.program_id(1)
    ts = nq_ref.shape[0]
    tc, ta, tb = tc_ref[...], ta_ref[...], tb_ref[...]
    ones_blk = ones_ref[...]
    scale = NSA_D ** -0.5 * LOG2E
    for p in range(NSA_W // LANES):
        cols = slice(p * LANES, (p + 1) * LANES)
        x = nq_ref[:, cols]
        xn = x * lax.rsqrt(_group_ms(x, ones_blk) + EPS) * qw_ref[...]
        q_ref[:, cols] = (_rope(xn, tc, ta, tb) * scale).astype(q_ref.dtype)
    for src, nw_ref, k_out, v_out in ((0, ksw_ref, ks_ref, vs_ref), (2, kww_ref, kw_ref, vw_ref)):
        k = kv_ref[:, src * LANES:(src + 1) * LANES]
        v = kv_ref[:, (src + 1) * LANES:(src + 2) * LANES]
        kn = k * lax.rsqrt(_group_ms(k, ones_blk) + EPS) * nw_ref[...]
        kr = _rope(kn, tc, ta, tb)
        vt = v.T
        for ig, kg in enumerate(_dup_groups(kr)):
            k_out[ig, :, 0:LANES] = kg.astype(k_out.dtype)
            v_out[ig] = _vt_block(vt[ig * NSA_D:(ig + 1) * NSA_D])
    pos = j * ts + lax.broadcasted_iota(jnp.int32, (ts, LANES), 0)
    lane = lax.broadcasted_iota(jnp.int32, (ts, LANES), 1)
    onehot = jnp.where(pos // NSA_SEL_BLOCK == lane, 1.0, 0.0).astype(ks_ref.dtype)
    sig = _sigmoid(sm_ref[...])
    for ig in range(NSA_GROUPS):
        ks_ref[ig, :, LANES:2 * LANES] = onehot
        kw_ref[ig, :, LANES:2 * LANES] = jnp.zeros((ts, LANES), kw_ref.dtype)
        gs_ref[ig] = pltpu.roll(sig, LANES - (gate_col0 + ig * NSA_REP * 3), 1)


def _nsa_prep(nq, kv4, small, tabs, q_norm_w, ks_norm_w, kw_norm_w, gate_col0, ts):
    b, s, _ = nq.shape
    g = NSA_GROUPS
    tile2 = lambda w: jnp.concatenate([w, w]).astype(F32)[None, :]
    ones_blk = np.kron(np.eye(2, dtype=np.float32), np.ones((NSA_D, NSA_D), np.float32))
    full = lambda shape: pl.BlockSpec(shape, lambda ib, ij: (0,) * len(shape))
    tok = lambda w: pl.BlockSpec((None, ts, w), lambda ib, ij: (ib, ij, 0))
    tab = pl.BlockSpec((ts, LANES), lambda ib, ij: (ij, 0))
    kv_out = lambda w: pl.BlockSpec((None, g, ts, w), lambda ib, ij: (ib, 0, ij, 0))
    kv_shape = lambda w, dt: jax.ShapeDtypeStruct((b, g, s, w), dt)
    vt_out = pl.BlockSpec((None, g, NSA_VROWS, ts), lambda ib, ij: (ib, 0, 0, ij))
    vt_shape = jax.ShapeDtypeStruct((b, g, NSA_VROWS, s), BF16)
    return pl.pallas_call(
        functools.partial(_nsa_prep_kernel, gate_col0=gate_col0),
        grid=(b, s // ts),
        in_specs=[tok(NSA_W), tok(4 * LANES), tok(LANES), tab, tab, tab,
                  full((1, LANES)), full((1, LANES)), full((1, LANES)), full((LANES, LANES))],
        out_specs=[tok(NSA_W), kv_out(2 * LANES), vt_out, kv_out(2 * LANES), vt_out, kv_out(LANES)],
        out_shape=[jax.ShapeDtypeStruct((b, s, NSA_W), BF16), kv_shape(2 * LANES, BF16), vt_shape,
                   kv_shape(2 * LANES, BF16), vt_shape, kv_shape(LANES, F32)],
        compiler_params=_cparams(("arbitrary", "arbitrary")),
        name="nsa_prep",
    )(nq, kv4, small, *tabs, tile2(q_norm_w), tile2(ks_norm_w), tile2(kw_norm_w), jnp.asarray(ones_blk))


def _nsa_compress_kernel(kc_ref, vc_ref, tc_ref, ta_ref, tb_ref, pk_ref, pv_ref, kw1_ref, kw2_ref,
                         vw1_ref, vw2_ref, nw_ref, ones_ref, kc_out, vc_out):
    nrow, width = kc_ref.shape
    half = width
    outs = []
    for is_k in (True, False):
        x = (kc_ref if is_k else vc_ref)[...]
        if is_k:
            x = jnp.concatenate(
                [_rope(x[:, p * LANES:(p + 1) * LANES], tc_ref[:, p * LANES:(p + 1) * LANES],
                       ta_ref[:, p * LANES:(p + 1) * LANES], tb_ref[:, p * LANES:(p + 1) * LANES])
                 for p in range(width // LANES)], axis=-1)
        pos_ref, w1_ref, w2_ref = (pk_ref, kw1_ref, kw2_ref) if is_k else (pv_ref, vw1_ref, vw2_ref)
        first = _mm(x + pos_ref[0:1, :], w1_ref[0:half, :])
        second = _mm(x + pos_ref[1:2, :], w1_ref[half:2 * half, :])
        y = first + pltpu.roll(second, nrow - 1, 0)
        if is_k:
            y = _mm(_silu(y), w2_ref[...])
            y = y * lax.rsqrt(_group_ms(y, ones_ref[...]) + EPS) * nw_ref[...]
        else:
            y = _mm_nt(w2_ref[...], _silu(y))
        outs.append(y)
    for ig, kg in enumerate(_dup_groups(outs[0])):
        kc_out[ig] = kg.astype(kc_out.dtype)
    for ig in range(NSA_GROUPS):
        vc_out[ig] = _vt_block(outs[1][ig * NSA_D:(ig + 1) * NSA_D])


def _nsa_compress(kc, vc, tabs, pos_k, pos_v, k_w1, k_w2, v_w1, v_w2, kc_norm_w):
    b, s, _ = kc.shape
    g = NSA_GROUPS
    st = NSA_CMP_STRIDE
    nrow = s // st
    width = st * LANES
    flat = lambda t: t.reshape(t.shape[0], nrow, width)
    eye_g = jnp.eye(g, dtype=F32)

    def w1_blk(w1):
        wl = w1.reshape(NSA_CMP_BLOCK, NSA_D, NSA_D)
        return jnp.einsum("lde,gh->lgdhe", wl, eye_g).reshape(NSA_CMP_BLOCK * LANES, LANES).astype(BF16)

    def w2_blk(w2):
        return jnp.einsum("de,gh->gdhe", w2, eye_g).reshape(LANES, LANES).astype(BF16)

    def pos_rows(p):
        return jnp.concatenate([p, p], axis=-1).reshape(2, width).astype(F32)

    ones_blk = np.kron(np.eye(2, dtype=np.float32), np.ones((NSA_D, NSA_D), np.float32))
    full = lambda shape: pl.BlockSpec(shape, lambda ib: (0,) * len(shape))
    seq = pl.BlockSpec((None, nrow, width), lambda ib: (ib, 0, 0))
    out = pl.BlockSpec((None, g, nrow, LANES), lambda ib: (ib, 0, 0, 0))
    oshape = jax.ShapeDtypeStruct((b, g, nrow, LANES), BF16)
    tabs_r = [t.reshape(nrow, width) for t in tabs]
    return pl.pallas_call(
        _nsa_compress_kernel,
        grid=(b,),
        in_specs=[seq, seq, full((nrow, width)), full((nrow, width)), full((nrow, width)),
                  full((2, width)), full((2, width)),
                  full((2 * width, LANES)), full((LANES, LANES)), full((2 * width, LANES)), full((LANES, LANES)),
                  full((1, LANES)), full((LANES, LANES))],
        out_specs=[out, pl.BlockSpec((None, g, NSA_VROWS, nrow), lambda ib: (ib, 0, 0, 0))],
        out_shape=[oshape, jax.ShapeDtypeStruct((b, g, NSA_VROWS, nrow), BF16)],
        compiler_params=_cparams(("arbitrary",)),
        name="nsa_compress",
    )(flat(kc), flat(vc), *tabs_r, pos_rows(pos_k), pos_rows(pos_v), w1_blk(k_w1), w2_blk(k_w2),
      w1_blk(v_w1), w2_blk(v_w2).T, jnp.concatenate([kc_norm_w, kc_norm_w]).astype(F32)[None, :],
      jnp.asarray(ones_blk))


def _nsa_attn_kernel(q_ref, gs_ref, kc_ref, vct_ref, ks_ref, vst_ref, kw_ref, vwt_ref, ovt_ref, tri_ref, wb_ref,
                     o_ref, *, kb, wlen):
    jp = pl.program_id(1)
    tq = NSA_Q_BLOCK
    rep = NSA_REP
    rows = rep * tq
    pairw = 2 * tq
    halves = range(2)
    chains = [(g, h) for g in range(NSA_GROUPS) for h in halves]
    nc = len(chains)
    blk_i = [2 * jp + h for h in halves]
    ncmp = kc_ref.shape[1]
    nblk = ks_ref.shape[1] // NSA_SEL_BLOCK
    lsum = slice(NSA_D, NSA_D + 1)

    lane_q = lax.broadcasted_iota(jnp.int32, (tq, LANES), 1)
    qs = []
    for g, h in chains:
        pieces = []
        for r in range(rep):
            pair = g * (rep // 2) + r // 2
            tile = q_ref[h * tq:(h + 1) * tq, pair * LANES:(pair + 1) * LANES]
            keep = (lane_q < NSA_D) if r % 2 == 0 else (lane_q >= NSA_D)
            pieces.append(jnp.where(keep, tile, jnp.zeros_like(tile)))
        qs.append(jnp.concatenate(pieces, axis=0))

    sc_raw = [_mm_nt(kc_ref[g], qs[c]) for c, (g, h) in enumerate(chains)]
    base = pl.multiple_of(jp * pairw, pairw)
    sd = [_mm_nt(ks_ref[g, pl.ds(base, pairw), 0:LANES], qs[c]) + tri_ref[h]
          for c, (g, h) in enumerate(chains)]
    wq_bias = jnp.where(lax.broadcasted_iota(jnp.int32, (rows, LANES), 1) == NSA_D, NEG, 0.0).astype(BF16)
    nwsub = wlen // NSA_SUB_KEYS
    wsub = [[pl.ds(base + h * pairw + j * NSA_SUB_KEYS, NSA_SUB_KEYS) for j in range(nwsub)] for h in halves]
    sw = [[_mm_nt(kw_ref[g, wsub[h][j], :], jnp.concatenate([qs[c], wq_bias], axis=1))
           + wb_ref[h, j * NSA_SUB_KEYS:(j + 1) * NSA_SUB_KEYS, :] for j in range(nwsub)]
          for c, (g, h) in enumerate(chains)]

    cend = lax.broadcasted_iota(jnp.int32, (ncmp, rows), 0) * NSA_CMP_STRIDE + (NSA_CMP_BLOCK - 1)
    tloc = lax.broadcasted_iota(jnp.int32, (ncmp, rows), 1) & (tq - 1)
    cmask = [cend - blk_i[h] * tq <= tloc for h in halves]
    sc = [jnp.where(cmask[h], sc_raw[c], NEG) for c, (g, h) in enumerate(chains)]
    mc = [jnp.max(sc[c], axis=0, keepdims=True) for c in range(nc)]
    pc = [jnp.where(cmask[h], jnp.exp2(sc[c] - mc[c]), 0.0) for c, (g, h) in enumerate(chains)]
    acc_c = [_mm(vct_ref[g], pc[c]) for c, (g, h) in enumerate(chains)]
    inv_c = [1.0 / jnp.maximum(acc_c[c][lsum, :], 1e-30) for c in range(nc)]

    m0 = [jnp.max(sd[c], axis=0, keepdims=True) for c in range(nc)]
    pd = [jnp.exp2(sd[c] - m0[c]) for c in range(nc)]
    a0 = [_mm(vst_ref[g, :, pl.ds(base, pairw)], pd[c]) for c, (g, h) in enumerate(chains)]

    acc_w = []
    for c, (g, h) in enumerate(chains):
        mw = sw[c][0]
        for j in range(1, nwsub):
            mw = jnp.maximum(mw, sw[c][j])
        mw = jnp.max(mw, axis=0, keepdims=True)
        a = _mm(vwt_ref[g, :, wsub[h][0]], jnp.exp2(sw[c][0] - mw))
        for j in range(1, nwsub):
            a = a + _mm(vwt_ref[g, :, wsub[h][j]], jnp.exp2(sw[c][j] - mw))
        acc_w.append(a)

    gates, o_cw = [], []
    for c, (g, h) in enumerate(chains):
        gt = gs_ref[g, h * tq:(h + 1) * tq, :].T
        gates.append([jnp.concatenate([gt[r * 3 + x:r * 3 + x + 1, :] for r in range(rep)], axis=1)
                      for x in range(3)])
        o_cw.append((gates[c][0] * inv_c[c]) * acc_c[c] + (gates[c][2] / acc_w[c][lsum, :]) * acc_w[c])

    lane_g = lax.broadcasted_iota(jnp.int32, (LANES, LANES), 1)
    imp_c = []
    for c in range(nc):
        pn = pc[c] * inv_c[c]
        folded = pn[:, 0:LANES] + pn[:, LANES:2 * LANES]
        psum = folded + pltpu.roll(folded, tq, 1)
        imp_c.append(_dot_exact_lhs(ovt_ref[...], psum))
    imp = [jnp.where(lane_g < tq, imp_c[h], imp_c[2 + h])[0:tq] for h in halves]
    blk = lax.broadcasted_iota(jnp.int32, (tq, LANES), 0)
    valid = [blk <= blk_i[h] for h in halves]

    def ranked():
        out = []
        for h in halves:
            forced = (blk == 0) | (blk == blk_i[h]) | (blk == blk_i[h] - 1)
            key = jnp.where(valid[h], jnp.where(forced, 0x7F000000, pltpu.bitcast(imp[h], jnp.int32)), -1)
            key = jnp.where(blk < nblk, key, -2)
            key_m1 = key - 1
            rank = jnp.zeros((tq, LANES), jnp.int32)
            for jb in range(nblk):
                ahead = key[jb:jb + 1, :] > jnp.where(blk > jb, key_m1, key)
                rank = rank + jnp.where(ahead, 1, 0)
            out.append(jnp.where(rank < NSA_N_SEL, 1.0, 0.0))
        return tuple(out)

    sel = lax.cond(blk_i[0] >= NSA_N_SEL, ranked, lambda: tuple(jnp.where(valid[h], 1.0, 0.0) for h in halves))
    q2 = [None] * nc
    for h in halves:
        selneg_t = jnp.where((sel[h] > 0.5) & (blk < 2 * jp), 0.0, NEG)
        selneg = jnp.concatenate([selneg_t, jnp.zeros((LANES - tq, LANES), F32)], axis=0).T
        for g in range(NSA_GROUPS):
            bias = selneg[g * tq:(g + 1) * tq].astype(BF16)
            q2[2 * g + h] = jnp.concatenate([qs[2 * g + h], jnp.concatenate([bias] * rep, axis=0)], axis=1)

    nsub = kb // NSA_SUB_KEYS

    def slc_body(ic, carry):
        m_i, acc = carry
        k0 = pl.multiple_of(ic * kb, kb)
        sub_rows = [pl.ds(k0 + j * NSA_SUB_KEYS, NSA_SUB_KEYS) for j in range(nsub)]
        m_out, acc_out = [None] * nc, [None] * nc

        def score(c):
            return [_mm_nt(ks_ref[chains[c][0], sub_rows[j], :], q2[c]) for j in range(nsub)]

        def update(c, s):
            m_chunk = s[0]
            for j in range(1, nsub):
                m_chunk = jnp.maximum(m_chunk, s[j])
            m_new = jnp.maximum(m_i[c], jnp.max(m_chunk, axis=0, keepdims=True))
            a = jnp.exp2(m_i[c] - m_new) * acc[c]
            for j in range(nsub):
                a = a + _mm(vst_ref[chains[c][0], :, sub_rows[j]], jnp.exp2(s[j] - m_new))
            m_out[c], acc_out[c] = m_new, a

        s_prev = score(0)
        for c in range(1, nc):
            s_next = score(c)
            update(c - 1, s_prev)
            s_prev = s_next
        update(nc - 1, s_prev)
        return tuple(m_out), tuple(acc_out)

    nch = (2 * jp * tq + kb - 1) // kb
    _, acc_s = lax.fori_loop(0, nch, slc_body, (tuple(m0), tuple(a0)))

    lo = lane_q < NSA_D
    for c, (g, h) in enumerate(chains):
        ot = (o_cw[c] + (gates[c][1] / acc_s[c][lsum, :]) * acc_s[c])
        ot = jnp.concatenate([ot, jnp.zeros((LANES - NSA_VROWS, rows), F32)], axis=0)
        heads = []
        for hp in range(rows // LANES):
            o_pair = ot[:, hp * LANES:(hp + 1) * LANES].T
            heads += [o_pair[0:tq], o_pair[tq:2 * tq]]
        for pr_ in range(rep // 2):
            pair = g * (rep // 2) + pr_
            o_ref[h * tq:(h + 1) * tq, pair * LANES:(pair + 1) * LANES] = jnp.where(
                lo, heads[2 * pr_], pltpu.roll(heads[2 * pr_ + 1], NSA_D, 1)).astype(o_ref.dtype)


def _nsa_attn(q, gsig, kcmp, vcmp_t, ks3, vs_t, kw3, vw_t):
    b, s, _ = q.shape
    g = NSA_GROUPS
    rep = NSA_REP
    tq = NSA_Q_BLOCK
    rows = rep * tq
    ncmp = kcmp.shape[2]
    nblk = s // NSA_SEL_BLOCK
    assert nblk <= tq and nblk <= NSA_D and (s // tq) % 2 == 0
    kb = min(1024, s)
    wlen = NSA_WINDOW + 4 * tq
    wpad = wlen - 2 * tq
    ci = np.arange(ncmp) * NSA_CMP_STRIDE
    sj = np.arange(nblk) * NSA_SEL_BLOCK
    ov = np.clip(np.minimum(ci[None, :] + NSA_CMP_BLOCK, sj[:, None] + NSA_SEL_BLOCK)
                 - np.maximum(ci[None, :], sj[:, None]), 0, None).astype(np.float32) / NSA_CMP_STRIDE
    ov_t = np.zeros((LANES, ncmp), np.float32)
    ov_t[:nblk] = ov
    tloc = (np.arange(rows) % tq)[None, :]
    kcol = np.arange(2 * tq)[:, None]
    tri = np.stack([
        np.where(kcol <= tloc, 0.0, NEG),
        np.where(kcol < tq, 0.0, np.where(kcol - tq <= tloc, 0.0, NEG)),
    ]).astype(np.float32)
    wcol = np.arange(wlen)[:, None]
    wb = np.stack([
        np.where((wcol <= wpad - par * tq + tloc) & (wcol > wpad - par * tq - NSA_WINDOW + tloc), 0.0, NEG)
        for par in range(2)]).astype(np.float32)
    front = np.zeros((wpad, 2 * LANES), np.float32)
    front[:, LANES + NSA_D] = 1.0
    kw_p = jnp.concatenate([jnp.broadcast_to(jnp.asarray(front, dtype=BF16), (b, g, wpad, 2 * LANES)), kw3,
                            jnp.zeros((b, g, 2 * tq, 2 * LANES), BF16)], axis=2)
    vw_p = jnp.pad(vw_t, ((0, 0), (0, 0), (0, 0), (wpad, 2 * tq)))
    sp = s + wpad + 2 * tq
    seq = lambda n, w: pl.BlockSpec((None, g, n, w), lambda ib, ii: (ib, 0, 0, 0))
    full = lambda shape: pl.BlockSpec(shape, lambda ib, ii: (0,) * len(shape))
    return pl.pallas_call(
        functools.partial(_nsa_attn_kernel, kb=kb, wlen=wlen),
        grid=(b, s // (2 * tq)),
        in_specs=[
            pl.BlockSpec((None, 2 * tq, NSA_W), lambda ib, ii: (ib, ii, 0)),
            pl.BlockSpec((None, g, 2 * tq, LANES), lambda ib, ii: (ib, 0, ii, 0)),
            seq(ncmp, LANES), seq(NSA_VROWS, ncmp), seq(s, 2 * LANES), seq(NSA_VROWS, s),
            seq(sp, 2 * LANES), seq(NSA_VROWS, sp),
            full((LANES, ncmp)), full((2, 2 * tq, rows)), full((2, wlen, rows)),
        ],
        out_specs=pl.BlockSpec((None, 2 * tq, NSA_W), lambda ib, ii: (ib, ii, 0)),
        out_shape=jax.ShapeDtypeStruct((b, s, NSA_W), BF16),
        compiler_params=_cparams(("arbitrary", "arbitrary")),
        name="nsa_attn",
    )(q, gsig, kcmp, vcmp_t, ks3, vs_t, kw_p, vw_p, jnp.asarray(ov_t), jnp.asarray(tri), jnp.asarray(wb))


def _mem_kv_kernel(mem_ref, nw_ref, w_ref, knw_ref, k_ref, v_ref):
    x = mem_ref[...]
    ms = jnp.mean(x * x, axis=-1, keepdims=True)
    xn = x * lax.rsqrt(ms + EPS) * nw_ref[...]
    kv = _mm(xn, w_ref[...])
    v_ref[...] = kv[:, MEM_W:].astype(v_ref.dtype)
    for h in range(MEM_HEADS):
        hs = slice(h * MEM_D, (h + 1) * MEM_D)
        kh = kv[:, hs]
        msk = jnp.mean(kh * kh, axis=-1, keepdims=True)
        k_ref[:, hs] = (kh * lax.rsqrt(msk + EPS) * knw_ref[...]).astype(k_ref.dtype)


def _mem_kv(mem, mem_norm_w, w_kv, k_norm_w):
    b, m, d = mem.shape
    full = lambda shape: pl.BlockSpec(shape, lambda ib: (0,) * len(shape))
    blk = lambda w: pl.BlockSpec((None, m, w), lambda ib: (ib, 0, 0))
    return pl.pallas_call(
        _mem_kv_kernel,
        grid=(b,),
        in_specs=[blk(d), full((1, d)), full((d, 2 * MEM_W)), full((1, MEM_D))],
        out_specs=[blk(MEM_W), blk(MEM_W)],
        out_shape=[jax.ShapeDtypeStruct((b, m, MEM_W), BF16)] * 2,
        compiler_params=_cparams(("arbitrary",)),
        name="mem_kv",
    )(mem, mem_norm_w.astype(F32)[None, :], w_kv.astype(BF16), k_norm_w.astype(F32)[None, :])


def _mem_attn_kernel(q_ref, k_ref, v_ref, qnw_ref, o_ref):
    scale = MEM_D ** -0.5
    for h in range(MEM_HEADS):
        hs = slice(h * MEM_D, (h + 1) * MEM_D)
        qh = q_ref[:, hs]
        ms = jnp.mean(qh * qh, axis=-1, keepdims=True)
        qn = qh * lax.rsqrt(ms + EPS) * qnw_ref[...]
        s = _mm_nt(qn, k_ref[:, hs]) * scale
        m = jnp.max(s, axis=-1, keepdims=True)
        p = jnp.exp(s - m)
        l = jnp.sum(p, axis=-1, keepdims=True)
        o_ref[:, hs] = (_mm(p, v_ref[:, hs]) * (1.0 / l)).astype(o_ref.dtype)


def _mem_attn(mq, k, v, q_norm_w, ts):
    b, s, _ = mq.shape
    m = k.shape[1]
    return pl.pallas_call(
        _mem_attn_kernel,
        grid=(b, s // ts),
        in_specs=[
            pl.BlockSpec((None, ts, MEM_W), lambda ib, ij: (ib, ij, 0)),
            pl.BlockSpec((None, m, MEM_W), lambda ib, ij: (ib, 0, 0)),
            pl.BlockSpec((None, m, MEM_W), lambda ib, ij: (ib, 0, 0)),
            pl.BlockSpec((1, MEM_D), lambda ib, ij: (0, 0)),
        ],
        out_specs=pl.BlockSpec((None, ts, MEM_W), lambda ib, ij: (ib, ij, 0)),
        out_shape=jax.ShapeDtypeStruct((b, s, MEM_W), BF16),
        compiler_params=_cparams(("arbitrary", "arbitrary")),
        name="mem_attn",
    )(mq, k, v, q_norm_w.astype(F32)[None, :])


def _out_proj_kernel(x_ref, oa_ref, ob_ref, oc_ref, w_ref, h_ref):
    acc = x_ref[...]
    off = 0
    for o_ref in (oa_ref, ob_ref, oc_ref):
        wd = o_ref.shape[-1]
        acc = acc + jnp.dot(o_ref[...].astype(BF16), w_ref[off:off + wd, :], preferred_element_type=F32)
        off += wd
    h_ref[...] = acc


def _out_proj(x2, oa, ob, oc, w_out, tm):
    t, d = x2.shape
    row = lambda w: pl.BlockSpec((tm, w), lambda i: (i, 0))
    return pl.pallas_call(
        _out_proj_kernel,
        grid=(t // tm,),
        in_specs=[row(d), row(oa.shape[1]), row(ob.shape[1]), row(oc.shape[1]),
                  pl.BlockSpec(w_out.shape, lambda i: (0, 0))],
        out_specs=row(d),
        out_shape=jax.ShapeDtypeStruct((t, d), F32),
        compiler_params=_cparams(("arbitrary",)),
        name="out_proj",
    )(x2, oa, ob, oc, w_out)


def _ffn_kernel(h_ref, halo_ref, nw_ref, wup_ref, cw_ref, wdn_ref, o_ref, hn_s, u_s, act_s, *, fc):
    j = pl.program_id(1)
    ts = h_ref.shape[0]
    f = wdn_ref.shape[0]
    hl = halo_ref.shape[0]

    def norm(x):
        ms = jnp.mean(x * x, axis=-1, keepdims=True)
        return (x * lax.rsqrt(ms + EPS) * nw_ref[...]).astype(BF16)

    halo = jnp.where(j > 0, halo_ref[...], 0.0)
    hn_s[0:hl, :] = norm(halo)
    hn_s[hl:hl + ts, :] = norm(h_ref[...])
    for ic in range(f // fc):
        hn = hn_s[...]
        slot = ic % 2
        for part in range(2):
            cols = slice(part * f + ic * fc, part * f + (ic + 1) * fc)
            u_s[slot, part] = jnp.dot(hn, wup_ref[:, cols], preferred_element_type=F32)
        conv = []
        for part in range(2):
            cols = slice(part * f + ic * fc, part * f + (ic + 1) * fc)
            acc = cw_ref[FFN_CONV - 1:FFN_CONV, cols] * u_s[slot, part, hl:hl + ts, :]
            for jj in range(FFN_CONV - 1):
                acc = acc + cw_ref[jj:jj + 1, cols] * u_s[slot, part, pl.ds(hl - (FFN_CONV - 1) + jj, ts), :]
            conv.append(acc)
        act_s[:, ic * fc:(ic + 1) * fc] = (_silu(conv[0]) * conv[1]).astype(BF16)
    o_ref[...] = h_ref[...] + jnp.dot(act_s[...], wdn_ref[...], preferred_element_type=F32)


def _ffn(h, norm_w, w_up, conv_w, w_down, ts, fc):
    b, s, d = h.shape
    f = w_down.shape[0]
    full = lambda shape: pl.BlockSpec(shape, lambda ib, ij: (0,) * len(shape))
    return pl.pallas_call(
        functools.partial(_ffn_kernel, fc=fc),
        grid=(b, s // ts),
        in_specs=[
            pl.BlockSpec((None, ts, d), lambda ib, ij: (ib, ij, 0)),
            pl.BlockSpec((None, HALO_BF16, d), lambda ib, ij: (ib, jnp.maximum(ij * (ts // HALO_BF16) - 1, 0), 0)),
            full((1, d)), full((d, 2 * f)), full((FFN_CONV, 2 * f)), full((f, d)),
        ],
        out_specs=pl.BlockSpec((None, ts, d), lambda ib, ij: (ib, ij, 0)),
        out_shape=jax.ShapeDtypeStruct((b, s, d), F32),
        scratch_shapes=[
            pltpu.VMEM((ts + HALO_BF16, d), BF16),
            pltpu.VMEM((2, 2, ts + HALO_BF16, fc), F32),
            pltpu.VMEM((ts, f), BF16),
        ],
        compiler_params=_cparams(("arbitrary", "arbitrary")),
        name="ffn",
    )(h, h, norm_w.astype(F32)[None, :], w_up, conv_w.astype(F32), w_down)


def _split_w_in(w_in):
    sizes = (3 * GDN_W, GDN_HEADS, GDN_HEADS, GDN_W, NSA_W, NSA_KV_W, NSA_KV_W, NSA_KV_W, NSA_KV_W,
             NSA_KV_W, NSA_KV_W, 3 * NSA_HEADS, MEM_W)
    offs = np.concatenate([[0], np.cumsum(sizes)])
    (qkv, a, bb, gate, nq, kc, vc, ks, vs, kw, vw, ng, mq) = [w_in[:, offs[i]:offs[i + 1]] for i in range(len(sizes))]
    n_small = 2 * GDN_HEADS + 3 * NSA_HEADS
    small = jnp.concatenate([a, bb, ng, jnp.zeros((w_in.shape[0], LANES - n_small), w_in.dtype)], axis=1)
    widths = (3 * GDN_W, GDN_W, NSA_W, NSA_KV_W, NSA_KV_W, 4 * NSA_KV_W, MEM_W, LANES)
    w_cat = jnp.concatenate([qkv, gate, nq, kc, vc, ks, vs, kw, vw, mq, small], axis=1).astype(BF16)
    return w_cat, widths


def _layer(x, mem, attn_norm_w, mem_norm_w, w_in, gdn_conv_w, gdn_a_log, gdn_dt_bias, gdn_out_norm_w,
           nsa_q_norm_w, nsa_kc_norm_w, nsa_ks_norm_w, nsa_kw_norm_w, nsa_cmp_pos_k, nsa_cmp_pos_v,
           nsa_cmp_k_w1, nsa_cmp_k_w2, nsa_cmp_v_w1, nsa_cmp_v_w2, mem_w_kv, mem_q_norm_w, mem_k_norm_w,
           w_out, ffn_norm_w, ffn_w_up, ffn_conv_w, ffn_w_down):
    b, s, d = x.shape
    t = b * s
    ts = min(512, s)
    x2 = x.reshape(t, d)

    w_cat, widths = _split_w_in(w_in)
    qkv, gate, nq, kc, vc, kv4, mq, small = _in_proj(x2, attn_norm_w.astype(F32)[None, :], w_cat, widths, ts)
    r3 = lambda a: a.reshape(b, s, a.shape[-1])

    o_a = _gdn(r3(qkv), r3(small), r3(gate), gdn_conv_w, gdn_a_log, gdn_dt_bias, gdn_out_norm_w, ts)

    tabs = _rope_tables(s)
    q_r, ks3, vs2, kw2, vw2, gsig = _nsa_prep(r3(nq), r3(kv4), r3(small), tabs, nsa_q_norm_w, nsa_ks_norm_w,
                                              nsa_kw_norm_w, 2 * GDN_HEADS, ts)
    kcmp, vcmp = _nsa_compress(r3(kc), r3(vc), tabs, nsa_cmp_pos_k, nsa_cmp_pos_v, nsa_cmp_k_w1, nsa_cmp_k_w2,
                               nsa_cmp_v_w1, nsa_cmp_v_w2, nsa_kc_norm_w)
    o_b = _nsa_attn(q_r, gsig, kcmp, vcmp, ks3, vs2, kw2, vw2)

    mk, mv = _mem_kv(mem, mem_norm_w, mem_w_kv, mem_k_norm_w)
    o_c = _mem_attn(r3(mq), mk, mv, mem_q_norm_w, ts)

    h = _out_proj(x2, o_a.reshape(t, GDN_W), o_b.reshape(t, NSA_W), o_c.reshape(t, MEM_W), w_out.astype(BF16), ts)
    out = _ffn(h.reshape(b, s, d), ffn_norm_w, ffn_w_up.astype(BF16), ffn_conv_w, ffn_w_down.astype(BF16), ts, 256)
    return out


def kernel(x, mem, attn_norm_w, mem_norm_w, w_in, gdn_conv_w, gdn_a_log, gdn_dt_bias, gdn_out_norm_w, nsa_q_norm_w, nsa_kc_norm_w, nsa_ks_norm_w, nsa_kw_norm_w, nsa_cmp_pos_k, nsa_cmp_pos_v, nsa_cmp_k_w1, nsa_cmp_k_w2, nsa_cmp_v_w1, nsa_cmp_v_w2, mem_w_kv, mem_q_norm_w, mem_k_norm_w, w_out, ffn_norm_w, ffn_w_up, ffn_conv_w, ffn_w_down):
    h = x
    for l in range(w_in.shape[0]):
        h = _layer(h, mem, attn_norm_w[l], mem_norm_w[l], w_in[l], gdn_conv_w[l], gdn_a_log[l], gdn_dt_bias[l],
                   gdn_out_norm_w[l], nsa_q_norm_w[l], nsa_kc_norm_w[l], nsa_ks_norm_w[l], nsa_kw_norm_w[l],
                   nsa_cmp_pos_k[l], nsa_cmp_pos_v[l], nsa_cmp_k_w1[l], nsa_cmp_k_w2[l], nsa_cmp_v_w1[l],
                   nsa_cmp_v_w2[l], mem_w_kv[l], mem_q_norm_w[l], mem_k_norm_w[l], w_out[l], ffn_norm_w[l],
                   ffn_w_up[l], ffn_conv_w[l], ffn_w_down[l])
    return h
```

```python
import functools

import jax
import jax.numpy as jnp
import numpy as np
from jax import lax
from jax.experimental import pallas as pl
from jax.experimental.pallas import tpu as pltpu

F32 = jnp.float32
BF16 = jnp.bfloat16

EPS = 1e-6
ROPE_THETA = 500000.0
GDN_HEADS = 4
GDN_D = 128
GDN_CONV = 4
GDN_CHUNK = 64
GDN_SUB = 16
GDN_GROUP = 4
NSA_HEADS = 8
NSA_GROUPS = 2
NSA_REP = NSA_HEADS // NSA_GROUPS
NSA_D = 64
NSA_CMP_BLOCK = 32
NSA_CMP_STRIDE = 16
NSA_SEL_BLOCK = 64
NSA_N_SEL = 16
NSA_WINDOW = 512
NSA_Q_BLOCK = 64
NSA_ROPE_DIM = NSA_D // 4
NSA_SUB_KEYS = 256
NSA_VROWS = NSA_D + 16
MEM_HEADS = 4
MEM_D = 128
FFN_CONV = 3

GDN_W = GDN_HEADS * GDN_D
NSA_W = NSA_HEADS * NSA_D
MEM_W = MEM_HEADS * MEM_D
NSA_KV_W = NSA_GROUPS * NSA_D

LANES = 128
HALO = 8
HALO_BF16 = 16
VMEM_LIMIT = 56 * 1024 * 1024
NEG = -1e30
LOG2E = 1.4426950408889634


def _cparams(sem):
    return pltpu.CompilerParams(dimension_semantics=sem, vmem_limit_bytes=VMEM_LIMIT)


def _mm(a, b):
    return jnp.dot(a.astype(BF16), b.astype(BF16), preferred_element_type=F32)


def _mm_nt(a, b):
    return lax.dot_general(a.astype(BF16), b.astype(BF16), (((1,), (1,)), ((), ())),
                           preferred_element_type=F32)


def _mm_tn(a, b):
    return lax.dot_general(a.astype(BF16), b.astype(BF16), (((0,), (0,)), ((), ())),
                           preferred_element_type=F32)


def _split3(x):
    hi = x.astype(BF16)
    r = x - hi.astype(F32)
    mid = r.astype(BF16)
    lo = (r - mid.astype(F32)).astype(BF16)
    return hi, mid, lo


def _dot_exact_rhs(x, e):
    hi, mid, lo = _split3(x)
    eb = e.astype(BF16)
    return (jnp.dot(hi, eb, preferred_element_type=F32) + jnp.dot(mid, eb, preferred_element_type=F32)
            + jnp.dot(lo, eb, preferred_element_type=F32))


def _dot_exact_lhs(e, x):
    hi, mid, lo = _split3(x)
    eb = e.astype(BF16)
    return (jnp.dot(eb, hi, preferred_element_type=F32) + jnp.dot(eb, mid, preferred_element_type=F32)
            + jnp.dot(eb, lo, preferred_element_type=F32))


def _sigmoid(x):
    return 1.0 / (1.0 + jnp.exp2(x * (-LOG2E)))


def _silu(x):
    return x * _sigmoid(x)


def _softplus(x):
    return jnp.maximum(x, 0.0) + jnp.log(1.0 + jnp.exp(-jnp.abs(x)))


def _in_proj_kernel(x_ref, nw_ref, w_ref, cw_ref, *refs, tiles_per_seq):
    o_refs, xb = refs[:-1], refs[-1]
    i = pl.program_id(0)
    tm = x_ref.shape[0]
    hw = GDN_W
    x = x_ref[...]
    ms = jnp.mean(x * x, axis=-1, keepdims=True)
    xn = (x * lax.rsqrt(ms + EPS) * nw_ref[...]).astype(BF16)
    off = 3 * hw
    for o_ref in o_refs[1:]:
        wd = o_ref.shape[-1]
        o_ref[...] = jnp.dot(xn, w_ref[:, off:off + wd], preferred_element_type=F32).astype(o_ref.dtype)
        off += wd

    @pl.when(i % tiles_per_seq == 0)
    def _():
        xb[0:HALO, :] = jnp.zeros((HALO, 3 * hw), F32)

    qkv_ref = o_refs[0]
    xb[HALO:HALO + tm, :] = jnp.dot(xn, w_ref[:, 0:3 * hw], preferred_element_type=F32)
    for part in range(3):
        cols = slice(part * hw, (part + 1) * hw)
        acc = cw_ref[GDN_CONV - 1:GDN_CONV, cols] * xb[HALO:HALO + tm, cols]
        for jj in range(GDN_CONV - 1):
            acc = acc + cw_ref[jj:jj + 1, cols] * xb[pl.ds(HALO - (GDN_CONV - 1) + jj, tm), cols]
        act = _silu(acc)
        if part == 2:
            qkv_ref[:, cols] = act
        else:
            scale = GDN_D ** -0.5 if part == 0 else 1.0
            for h in range(GDN_HEADS):
                hs = slice(part * hw + h * GDN_D, part * hw + (h + 1) * GDN_D)
                xh = act[:, h * GDN_D:(h + 1) * GDN_D]
                ss = jnp.sum(xh * xh, axis=-1, keepdims=True)
                qkv_ref[:, hs] = xh * (lax.rsqrt(ss + EPS) * scale)
    xb[0:HALO, :] = xb[tm:tm + HALO, :]


def _in_proj(x2, norm_w, w_cat, conv_w, widths, tm, tiles_per_seq):
    t, d = x2.shape
    wtot = w_cat.shape[1]
    return pl.pallas_call(
        functools.partial(_in_proj_kernel, tiles_per_seq=tiles_per_seq),
        grid=(t // tm,),
        in_specs=[
            pl.BlockSpec((tm, d), lambda i: (i, 0)),
            pl.BlockSpec((1, d), lambda i: (0, 0)),
            pl.BlockSpec((d, wtot), lambda i: (0, 0)),
            pl.BlockSpec(conv_w.shape, lambda i: (0, 0)),
        ],
        out_specs=[pl.BlockSpec((tm, wd), lambda i: (i, 0)) for wd in widths],
        out_shape=[jax.ShapeDtypeStruct((t, wd), F32) for wd in widths],
        scratch_shapes=[pltpu.VMEM((tm + HALO, widths[0]), F32)],
        compiler_params=_cparams(("arbitrary",)),
        name="in_proj",
    )(x2, norm_w, w_cat, conv_w.astype(F32))


def _gdn_kernel(qkv_ref, sm_ref, gate_ref, alog_ref, dtb_ref, onw_ref, ea_ref, eb_ref, ltri_ref,
                o_ref, qn_s, kn_s, g_s, beta_s, u_s, w_s, aqk_s, egl_s, oacc_s, state_s):
    j = pl.program_id(1)
    ts = o_ref.shape[0]
    c = GDN_CHUNK
    hw = GDN_W

    @pl.when(j == 0)
    def _():
        state_s[...] = jnp.zeros_like(state_s)

    sm = sm_ref[...]
    a_full = _dot_exact_rhs(sm, ea_ref[...])
    b_full = _dot_exact_rhs(sm, eb_ref[...])
    g_s[...] = -jnp.exp(alog_ref[...]) * _softplus(a_full + dtb_ref[...])
    beta_s[...] = _sigmoid(b_full)

    ri = lax.broadcasted_iota(jnp.int32, (c, c), 0)
    ci = lax.broadcasted_iota(jnp.int32, (c, c), 1)
    causal = ri >= ci
    strict = ri > ci
    blockdiag = (ri // GDN_SUB) == (ci // GDN_SUB)
    ltri = ltri_ref[...]
    heads = [slice(h * GDN_D, (h + 1) * GDN_D) for h in range(GDN_HEADS)]

    def precompute(ig):
        base = ig * (GDN_GROUP * c)
        kb_l, kn_l, qn_l, rhs_l, decay_l, where_l = [], [], [], [], [], []
        for cc in range(GDN_GROUP):
            rows = pl.ds(base + cc * c, c)
            gc = _dot_exact_lhs(ltri, g_s[rows, :])
            glast = gc[c - 1:c, :]
            eg = jnp.exp(gc)
            beta = beta_s[rows, :]
            qn = qkv_ref[rows, 0:hw]
            kn = qkv_ref[rows, hw:2 * hw]
            kb = kn * beta
            vb = qkv_ref[rows, 2 * hw:3 * hw] * beta
            kbe = kb * eg
            qn_s[rows, :] = qn * eg
            kn_s[rows, :] = kn * jnp.exp(glast - gc)
            egl_s[pl.ds((ig * GDN_GROUP + cc) * HALO, HALO), :] = jnp.broadcast_to(jnp.exp(glast), (HALO, hw))
            for h, hs in enumerate(heads):
                gcol = gc[:, h * GDN_D:h * GDN_D + c]
                grow = gc[:, hs].T[0:1, 0:c]
                diff = gcol - grow
                decay_l.append(jnp.where(causal, jnp.exp(jnp.where(causal, diff, 0.0)), 0.0))
                kb_l.append(kb[:, hs].astype(BF16))
                kn_l.append(kn[:, hs].astype(BF16))
                qn_l.append(qn[:, hs].astype(BF16))
                rhs_l.append(jnp.concatenate([vb[:, hs], kbe[:, hs]], axis=-1))
                where_l.append((rows, h, hs))
        n = len(where_l)
        kk = [_mm_nt(kb_l[i], kn_l[i]) for i in range(n)]
        qk = [_mm_nt(qn_l[i], kn_l[i]) for i in range(n)]
        for i, (rows, h, hs) in enumerate(where_l):
            aqk_s[rows, h * c:(h + 1) * c] = qk[i] * decay_l[i]
        p = [-jnp.where(strict, kk[i] * decay_l[i], 0.0) for i in range(n)]
        pd = [jnp.where(blockdiag, x, 0.0) for x in p]
        pn = [p[i] - pd[i] for i in range(n)]
        p2 = [_mm(x, x) for x in pd]
        p4 = [_mm(x, x) for x in p2]
        p8 = [_mm(x, x) for x in p4]
        a1 = [pd[i] + p2[i] + _mm(pd[i], p2[i]) for i in range(n)]
        a2 = [a1[i] + p4[i] + _mm(a1[i], p4[i]) for i in range(n)]
        a3 = [a2[i] + p8[i] + _mm(a2[i], p8[i]) for i in range(n)]
        nm = [pn[i] + _mm(a3[i], pn[i]) for i in range(n)]
        n2 = [_mm(x, x) for x in nm]
        bm = [nm[i] + n2[i] + _mm(nm[i], n2[i]) for i in range(n)]
        tm = [bm[i] + a3[i] + _mm(bm[i], a3[i]) for i in range(n)]
        for i, (rows, h, hs) in enumerate(where_l):
            sol = rhs_l[i] + _mm(tm[i], rhs_l[i])
            u_s[rows, hs] = sol[:, :GDN_D]
            w_s[rows, hs] = sol[:, GDN_D:]

    def scan_step(ic):
        rows = pl.ds(ic * c, c)
        egl = egl_s[pl.ds(ic * HALO, 1), :]
        st = [state_s[h] for h in range(GDN_HEADS)]
        stb = [x.astype(BF16) for x in st]
        ws = [_mm(w_s[rows, hs], stb[h]) for h, hs in enumerate(heads)]
        qs = [_mm(qn_s[rows, hs], stb[h]) for h, hs in enumerate(heads)]
        v_new = [u_s[rows, hs] - ws[h] for h, hs in enumerate(heads)]
        for h, hs in enumerate(heads):
            oacc_s[rows, hs] = qs[h] + _mm(aqk_s[rows, h * c:(h + 1) * c], v_new[h])
            state_s[h] = st[h] * egl[:, hs] + _mm_tn(kn_s[rows, hs], v_new[h])

    ngroups = ts // (GDN_GROUP * c)
    precompute(0)
    for ig in range(1, ngroups):
        precompute(ig)
        for ic in range((ig - 1) * GDN_GROUP, ig * GDN_GROUP):
            scan_step(ic)
    for ic in range((ngroups - 1) * GDN_GROUP, ngroups * GDN_GROUP):
        scan_step(ic)

    for h in range(GDN_HEADS):
        hs = slice(h * GDN_D, (h + 1) * GDN_D)
        oh = oacc_s[:, hs]
        ms = jnp.mean(oh * oh, axis=-1, keepdims=True)
        o_ref[:, hs] = (oh * lax.rsqrt(ms + EPS) * onw_ref[...] * _silu(gate_ref[:, hs])).astype(o_ref.dtype)


def _gdn(qkv, small, gate, a_log, dt_bias, out_norm_w, ts):
    b, s, _ = qkv.shape
    hw = GDN_W
    c = GDN_CHUNK
    rep = lambda v: jnp.repeat(v.astype(F32), GDN_D)[None, :]
    lane_head = np.arange(hw) // GDN_D
    ea = (np.arange(LANES)[:, None] == lane_head[None, :]).astype(np.float32)
    eb = (np.arange(LANES)[:, None] == (lane_head[None, :] + GDN_HEADS)).astype(np.float32)
    ltri = np.tril(np.ones((c, c), np.float32))
    full = lambda shape: pl.BlockSpec(shape, lambda ib, ij: (0,) * len(shape))
    return pl.pallas_call(
        _gdn_kernel,
        grid=(b, s // ts),
        in_specs=[
            pl.BlockSpec((None, ts, 3 * hw), lambda ib, ij: (ib, ij, 0)),
            pl.BlockSpec((None, ts, LANES), lambda ib, ij: (ib, ij, 0)),
            pl.BlockSpec((None, ts, hw), lambda ib, ij: (ib, ij, 0)),
            full((1, hw)), full((1, hw)), full((1, GDN_D)),
            full((LANES, hw)), full((LANES, hw)), full((c, c)),
        ],
        out_specs=pl.BlockSpec((None, ts, hw), lambda ib, ij: (ib, ij, 0)),
        out_shape=jax.ShapeDtypeStruct((b, s, hw), BF16),
        scratch_shapes=[
            pltpu.VMEM((ts, hw), F32), pltpu.VMEM((ts, hw), F32),
            pltpu.VMEM((ts, hw), F32), pltpu.VMEM((ts, hw), F32),
            pltpu.VMEM((ts, hw), F32), pltpu.VMEM((ts, hw), F32), pltpu.VMEM((ts, GDN_HEADS * c), F32),
            pltpu.VMEM((ts // c * HALO, hw), F32),
            pltpu.VMEM((ts, hw), F32),
            pltpu.VMEM((GDN_HEADS, GDN_D, GDN_D), F32),
        ],
        compiler_params=_cparams(("arbitrary", "arbitrary")),
        name="gdn",
    )(qkv, small, gate, rep(a_log), rep(dt_bias), out_norm_w.astype(F32)[None, :],
      jnp.asarray(ea), jnp.asarray(eb), jnp.asarray(ltri))


def _rope_tables(s):
    half = NSA_ROPE_DIM // 2
    pos = jnp.arange(s, dtype=F32)
    inv = 1.0 / (ROPE_THETA ** (jnp.arange(0, NSA_ROPE_DIM, 2, dtype=F32) / NSA_ROPE_DIM))
    ang = pos[:, None] * inv[None, :]
    cos, sin = jnp.cos(ang), jnp.sin(ang)
    one = jnp.ones((s, NSA_D - NSA_ROPE_DIM), F32)
    zero = jnp.zeros((s, NSA_D - NSA_ROPE_DIM), F32)
    zh = jnp.zeros((s, half), F32)
    tc = jnp.concatenate([cos, cos, one], axis=-1)
    ta = jnp.concatenate([-sin, zh, zero], axis=-1)
    tb = jnp.concatenate([zh, sin, zero], axis=-1)
    dup = lambda t: jnp.concatenate([t, t], axis=-1)
    return dup(tc), dup(ta), dup(tb)


def _rope(x, tc, ta, tb):
    half = NSA_ROPE_DIM // 2
    return x * tc + pltpu.roll(x, LANES - half, 1) * ta + pltpu.roll(x, half, 1) * tb


def _group_ms(x, ones_blk):
    return _dot_exact_rhs(x * x, ones_blk) * (1.0 / NSA_D)


def _dup_groups(x):
    r = pltpu.roll(x, NSA_D, 1)
    lane = lax.broadcasted_iota(jnp.int32, x.shape, 1)
    lo = lane < NSA_D
    return jnp.where(lo, x, r), jnp.where(lo, r, x)


def _vt_block(vt):
    n = vt.shape[1]
    tail = jnp.where(lax.broadcasted_iota(jnp.int32, (NSA_VROWS - NSA_D, n), 0) == 0, 1.0, 0.0)
    return jnp.concatenate([vt, tail], axis=0).astype(BF16)


def _nsa_prep_kernel(nq_ref, kv_ref, sm_ref, tc_ref, ta_ref, tb_ref, qw_ref, ksw_ref, kww_ref, ones_ref,
                     q_ref, ks_ref, vs_ref, kw_ref, vw_ref, gs_ref, *, gate_col0):
    j = pl.program_id(1)
    ts = nq_ref.shape[0]
    tc, ta, tb = tc_ref[...], ta_ref[...], tb_ref[...]
    ones_blk = ones_ref[...]
    scale = NSA_D ** -0.5 * LOG2E
    for p in range(NSA_W // LANES):
        cols = slice(p * LANES, (p + 1) * LANES)
        x = nq_ref[:, cols]
        xn = x * lax.rsqrt(_group_ms(x, ones_blk) + EPS) * qw_ref[...]
        q_ref[:, cols] = (_rope(xn, tc, ta, tb) * scale).astype(q_ref.dtype)
    for src, nw_ref, k_out, v_out in ((0, ksw_ref, ks_ref, vs_ref), (2, kww_ref, kw_ref, vw_ref)):
        k = kv_ref[:, src * LANES:(src + 1) * LANES]
        v = kv_ref[:, (src + 1) * LANES:(src + 2) * LANES]
        kn = k * lax.rsqrt(_group_ms(k, ones_blk) + EPS) * nw_ref[...]
        kr = _rope(kn, tc, ta, tb)
        vt = v.T
        for ig, kg in enumerate(_dup_groups(kr)):
            k_out[ig, :, 0:LANES] = kg.astype(k_out.dtype)
            v_out[ig] = _vt_block(vt[ig * NSA_D:(ig + 1) * NSA_D])
    pos = j * ts + lax.broadcasted_iota(jnp.int32, (ts, LANES), 0)
    lane = lax.broadcasted_iota(jnp.int32, (ts, LANES), 1)
    onehot = jnp.where(pos // NSA_SEL_BLOCK == lane, 1.0, 0.0).astype(ks_ref.dtype)
    sig = _sigmoid(sm_ref[...])
    for ig in range(NSA_GROUPS):
        ks_ref[ig, :, LANES:2 * LANES] = onehot
        kw_ref[ig, :, LANES:2 * LANES] = jnp.zeros((ts, LANES), kw_ref.dtype)
        gs_ref[ig] = pltpu.roll(sig, LANES - (gate_col0 + ig * NSA_REP * 3), 1)


def _nsa_prep(nq, kv4, small, tabs, q_norm_w, ks_norm_w, kw_norm_w, gate_col0, ts):
    b, s, _ = nq.shape
    g = NSA_GROUPS
    tile2 = lambda w: jnp.concatenate([w, w]).astype(F32)[None, :]
    ones_blk = np.kron(np.eye(2, dtype=np.float32), np.ones((NSA_D, NSA_D), np.float32))
    full = lambda shape: pl.BlockSpec(shape, lambda ib, ij: (0,) * len(shape))
    tok = lambda w: pl.BlockSpec((None, ts, w), lambda ib, ij: (ib, ij, 0))
    tab = pl.BlockSpec((ts, LANES), lambda ib, ij: (ij, 0))
    kv_out = lambda w: pl.BlockSpec((None, g, ts, w), lambda ib, ij: (ib, 0, ij, 0))
    kv_shape = lambda w, dt: jax.ShapeDtypeStruct((b, g, s, w), dt)
    vt_out = pl.BlockSpec((None, g, NSA_VROWS, ts), lambda ib, ij: (ib, 0, 0, ij))
    vt_shape = jax.ShapeDtypeStruct((b, g, NSA_VROWS, s), BF16)
    return pl.pallas_call(
        functools.partial(_nsa_prep_kernel, gate_col0=gate_col0),
        grid=(b, s // ts),
        in_specs=[tok(NSA_W), tok(4 * LANES), tok(LANES), tab, tab, tab,
                  full((1, LANES)), full((1, LANES)), full((1, LANES)), full((LANES, LANES))],
        out_specs=[tok(NSA_W), kv_out(2 * LANES), vt_out, kv_out(2 * LANES), vt_out, kv_out(LANES)],
        out_shape=[jax.ShapeDtypeStruct((b, s, NSA_W), BF16), kv_shape(2 * LANES, BF16), vt_shape,
                   kv_shape(2 * LANES, BF16), vt_shape, kv_shape(LANES, F32)],
        compiler_params=_cparams(("arbitrary", "arbitrary")),
        name="nsa_prep",
    )(nq, kv4, small, *tabs, tile2(q_norm_w), tile2(ks_norm_w), tile2(kw_norm_w), jnp.asarray(ones_blk))


def _nsa_compress_kernel(kc_ref, vc_ref, tc_ref, ta_ref, tb_ref, pk_ref, pv_ref, kw1_ref, kw2_ref,
                         vw1_ref, vw2_ref, nw_ref, ones_ref, kc_out, vc_out):
    st = NSA_CMP_STRIDE
    nrow = kc_ref.shape[0] // st
    outs = []
    for is_k in (True, False):
        src = kc_ref if is_k else vc_ref
        pos_ref, w1_ref, w2_ref = (pk_ref, kw1_ref, kw2_ref) if is_k else (pv_ref, vw1_ref, vw2_ref)
        first = second = None
        for l in range(st):
            rows = pl.ds(l, nrow, stride=st)
            x = src[rows, :]
            if is_k:
                x = _rope(x, tc_ref[rows, :], ta_ref[rows, :], tb_ref[rows, :])
            f = _mm(x + pos_ref[l:l + 1, :], w1_ref[l * LANES:(l + 1) * LANES, :])
            s2 = _mm(x + pos_ref[st + l:st + l + 1, :], w1_ref[(st + l) * LANES:(st + l + 1) * LANES, :])
            first = f if first is None else first + f
            second = s2 if second is None else second + s2
        y = first + pltpu.roll(second, nrow - 1, 0)
        if is_k:
            y = _mm(_silu(y), w2_ref[...])
            y = y * lax.rsqrt(_group_ms(y, ones_ref[...]) + EPS) * nw_ref[...]
        else:
            y = _mm_nt(w2_ref[...], _silu(y))
        outs.append(y)
    for ig, kg in enumerate(_dup_groups(outs[0])):
        kc_out[ig] = kg.astype(kc_out.dtype)
    for ig in range(NSA_GROUPS):
        vc_out[ig] = _vt_block(outs[1][ig * NSA_D:(ig + 1) * NSA_D])


def _nsa_compress(kc, vc, tabs, pos_k, pos_v, k_w1, k_w2, v_w1, v_w2, kc_norm_w):
    b, s, _ = kc.shape
    g = NSA_GROUPS
    st = NSA_CMP_STRIDE
    nrow = s // st
    width = st * LANES
    eye_g = jnp.eye(g, dtype=F32)

    def w1_blk(w1):
        wl = w1.reshape(NSA_CMP_BLOCK, NSA_D, NSA_D)
        return jnp.einsum("lde,gh->lgdhe", wl, eye_g).reshape(NSA_CMP_BLOCK * LANES, LANES).astype(BF16)

    def w2_blk(w2):
        return jnp.einsum("de,gh->gdhe", w2, eye_g).reshape(LANES, LANES).astype(BF16)

    def pos_rows(p):
        return jnp.concatenate([p, p], axis=-1).astype(F32)

    ones_blk = np.kron(np.eye(2, dtype=np.float32), np.ones((NSA_D, NSA_D), np.float32))
    full = lambda shape: pl.BlockSpec(shape, lambda ib: (0,) * len(shape))
    seq = pl.BlockSpec((None, s, LANES), lambda ib: (ib, 0, 0))
    out = pl.BlockSpec((None, g, nrow, LANES), lambda ib: (ib, 0, 0, 0))
    oshape = jax.ShapeDtypeStruct((b, g, nrow, LANES), BF16)
    return pl.pallas_call(
        _nsa_compress_kernel,
        grid=(b,),
        in_specs=[seq, seq, full((s, LANES)), full((s, LANES)), full((s, LANES)),
                  full((NSA_CMP_BLOCK, LANES)), full((NSA_CMP_BLOCK, LANES)),
                  full((2 * width, LANES)), full((LANES, LANES)), full((2 * width, LANES)), full((LANES, LANES)),
                  full((1, LANES)), full((LANES, LANES))],
        out_specs=[out, pl.BlockSpec((None, g, NSA_VROWS, nrow), lambda ib: (ib, 0, 0, 0))],
        out_shape=[oshape, jax.ShapeDtypeStruct((b, g, NSA_VROWS, nrow), BF16)],
        compiler_params=_cparams(("arbitrary",)),
        name="nsa_compress",
    )(kc, vc, *tabs, pos_rows(pos_k), pos_rows(pos_v), w1_blk(k_w1), w2_blk(k_w2),
      w1_blk(v_w1), w2_blk(v_w2).T, jnp.concatenate([kc_norm_w, kc_norm_w]).astype(F32)[None, :],
      jnp.asarray(ones_blk))


def _nsa_attn_kernel(q_ref, gs_ref, kc_ref, vct_ref, ks_ref, vst_ref, kw_ref, vwt_ref, ovt_ref, tri_ref, wb_ref,
                     o_ref, *, kb, wlen):
    jp = pl.program_id(1)
    tq = NSA_Q_BLOCK
    rep = NSA_REP
    rows = rep * tq
    pairw = 2 * tq
    halves = range(2)
    chains = [(g, h) for g in range(NSA_GROUPS) for h in halves]
    nc = len(chains)
    blk_i = [2 * jp + h for h in halves]
    ncmp = kc_ref.shape[1]
    nblk = ks_ref.shape[1] // NSA_SEL_BLOCK
    lsum = slice(NSA_D, NSA_D + 1)

    lane_q = lax.broadcasted_iota(jnp.int32, (tq, LANES), 1)
    qs = []
    for g, h in chains:
        pieces = []
        for r in range(rep):
            pair = g * (rep // 2) + r // 2
            tile = q_ref[h * tq:(h + 1) * tq, pair * LANES:(pair + 1) * LANES]
            keep = (lane_q < NSA_D) if r % 2 == 0 else (lane_q >= NSA_D)
            pieces.append(jnp.where(keep, tile, jnp.zeros_like(tile)))
        qs.append(jnp.concatenate(pieces, axis=0))

    sc_raw = [_mm_nt(kc_ref[g], qs[c]) for c, (g, h) in enumerate(chains)]
    base = pl.multiple_of(jp * pairw, pairw)
    sd = [_mm_nt(ks_ref[g, pl.ds(base, pairw), 0:LANES], qs[c]) + tri_ref[h]
          for c, (g, h) in enumerate(chains)]
    wq_bias = jnp.where(lax.broadcasted_iota(jnp.int32, (rows, LANES), 1) == NSA_D, NEG, 0.0).astype(BF16)
    nwsub = wlen // NSA_SUB_KEYS
    wsub = [[pl.ds(base + h * pairw + j * NSA_SUB_KEYS, NSA_SUB_KEYS) for j in range(nwsub)] for h in halves]
    sw = [[_mm_nt(kw_ref[g, wsub[h][j], :], jnp.concatenate([qs[c], wq_bias], axis=1))
           + wb_ref[h, j * NSA_SUB_KEYS:(j + 1) * NSA_SUB_KEYS, :] for j in range(nwsub)]
          for c, (g, h) in enumerate(chains)]

    cend = lax.broadcasted_iota(jnp.int32, (ncmp, rows), 0) * NSA_CMP_STRIDE + (NSA_CMP_BLOCK - 1)
    tloc = lax.broadcasted_iota(jnp.int32, (ncmp, rows), 1) & (tq - 1)
    cmask = [cend - blk_i[h] * tq <= tloc for h in halves]
    sc = [jnp.where(cmask[h], sc_raw[c], NEG) for c, (g, h) in enumerate(chains)]
    mc = [jnp.max(sc[c], axis=0, keepdims=True) for c in range(nc)]
    pc = [jnp.where(cmask[h], jnp.exp2(sc[c] - mc[c]), 0.0) for c, (g, h) in enumerate(chains)]
    acc_c = [_mm(vct_ref[g], pc[c]) for c, (g, h) in enumerate(chains)]
    inv_c = [1.0 / jnp.maximum(acc_c[c][lsum, :], 1e-30) for c in range(nc)]

    m0 = [jnp.max(sd[c], axis=0, keepdims=True) for c in range(nc)]
    pd = [jnp.exp2(sd[c] - m0[c]) for c in range(nc)]
    a0 = [_mm(vst_ref[g, :, pl.ds(base, pairw)], pd[c]) for c, (g, h) in enumerate(chains)]

    acc_w = []
    for c, (g, h) in enumerate(chains):
        mw = sw[c][0]
        for j in range(1, nwsub):
            mw = jnp.maximum(mw, sw[c][j])
        mw = jnp.max(mw, axis=0, keepdims=True)
        a = _mm(vwt_ref[g, :, wsub[h][0]], jnp.exp2(sw[c][0] - mw))
        for j in range(1, nwsub):
            a = a + _mm(vwt_ref[g, :, wsub[h][j]], jnp.exp2(sw[c][j] - mw))
        acc_w.append(a)

    gates, o_cw = [], []
    for c, (g, h) in enumerate(chains):
        gt = gs_ref[g, h * tq:(h + 1) * tq, :].T
        gates.append([jnp.concatenate([gt[r * 3 + x:r * 3 + x + 1, :] for r in range(rep)], axis=1)
                      for x in range(3)])
        o_cw.append((gates[c][0] * inv_c[c]) * acc_c[c] + (gates[c][2] / acc_w[c][lsum, :]) * acc_w[c])

    lane_g = lax.broadcasted_iota(jnp.int32, (LANES, LANES), 1)
    imp_c = []
    for c in range(nc):
        pn = pc[c] * inv_c[c]
        folded = pn[:, 0:LANES] + pn[:, LANES:2 * LANES]
        psum = folded + pltpu.roll(folded, tq, 1)
        imp_c.append(_dot_exact_lhs(ovt_ref[...], psum))
    imp = [jnp.where(lane_g < tq, imp_c[h], imp_c[2 + h])[0:tq] for h in halves]
    blk = lax.broadcasted_iota(jnp.int32, (tq, LANES), 0)
    valid = [blk <= blk_i[h] for h in halves]

    def ranked():
        out = []
        for h in halves:
            forced = (blk == 0) | (blk == blk_i[h]) | (blk == blk_i[h] - 1)
            key = jnp.where(valid[h], jnp.where(forced, 0x7F000000, pltpu.bitcast(imp[h], jnp.int32)), -1)
            key = jnp.where(blk < nblk, key, -2)
            key_m1 = key - 1
            rank = jnp.zeros((tq, LANES), jnp.int32)
            for jb in range(nblk):
                ahead = key[jb:jb + 1, :] > jnp.where(blk > jb, key_m1, key)
                rank = rank + jnp.where(ahead, 1, 0)
            out.append(jnp.where(rank < NSA_N_SEL, 1.0, 0.0))
        return tuple(out)

    sel = lax.cond(blk_i[0] >= NSA_N_SEL, ranked, lambda: tuple(jnp.where(valid[h], 1.0, 0.0) for h in halves))
    q2 = [None] * nc
    for h in halves:
        selneg_t = jnp.where((sel[h] > 0.5) & (blk < 2 * jp), 0.0, NEG)
        selneg = jnp.concatenate([selneg_t, jnp.zeros((LANES - tq, LANES), F32)], axis=0).T
        for g in range(NSA_GROUPS):
            bias = selneg[g * tq:(g + 1) * tq].astype(BF16)
            q2[2 * g + h] = jnp.concatenate([qs[2 * g + h], jnp.concatenate([bias] * rep, axis=0)], axis=1)

    nsub = kb // NSA_SUB_KEYS

    def slc_body(ic, carry):
        m_i, acc = carry
        k0 = pl.multiple_of(ic * kb, kb)
        sub_rows = [pl.ds(k0 + j * NSA_SUB_KEYS, NSA_SUB_KEYS) for j in range(nsub)]
        m_out, acc_out = [None] * nc, [None] * nc

        def score(c):
            return [_mm_nt(ks_ref[chains[c][0], sub_rows[j], :], q2[c]) for j in range(nsub)]

        def update(c, s):
            m_chunk = s[0]
            for j in range(1, nsub):
                m_chunk = jnp.maximum(m_chunk, s[j])
            m_new = jnp.maximum(m_i[c], jnp.max(m_chunk, axis=0, keepdims=True))
            a = jnp.exp2(m_i[c] - m_new) * acc[c]
            for j in range(nsub):
                a = a + _mm(vst_ref[chains[c][0], :, sub_rows[j]], jnp.exp2(s[j] - m_new))
            m_out[c], acc_out[c] = m_new, a

        s_prev = score(0)
        for c in range(1, nc):
            s_next = score(c)
            update(c - 1, s_prev)
            s_prev = s_next
        update(nc - 1, s_prev)
        return tuple(m_out), tuple(acc_out)

    nch = (2 * jp * tq + kb - 1) // kb
    _, acc_s = lax.fori_loop(0, nch, slc_body, (tuple(m0), tuple(a0)))

    lo = lane_q < NSA_D
    for c, (g, h) in enumerate(chains):
        ot = (o_cw[c] + (gates[c][1] / acc_s[c][lsum, :]) * acc_s[c])
        ot = jnp.concatenate([ot, jnp.zeros((LANES - NSA_VROWS, rows), F32)], axis=0)
        heads = []
        for hp in range(rows // LANES):
            o_pair = ot[:, hp * LANES:(hp + 1) * LANES].T
            heads += [o_pair[0:tq], o_pair[tq:2 * tq]]
        for pr_ in range(rep // 2):
            pair = g * (rep // 2) + pr_
            o_ref[h * tq:(h + 1) * tq, pair * LANES:(pair + 1) * LANES] = jnp.where(
                lo, heads[2 * pr_], pltpu.roll(heads[2 * pr_ + 1], NSA_D, 1)).astype(o_ref.dtype)


def _nsa_attn(q, gsig, kcmp, vcmp_t, ks3, vs_t, kw3, vw_t):
    b, s, _ = q.shape
    g = NSA_GROUPS
    rep = NSA_REP
    tq = NSA_Q_BLOCK
    rows = rep * tq
    ncmp = kcmp.shape[2]
    nblk = s // NSA_SEL_BLOCK
    assert nblk <= tq and nblk <= NSA_D and (s // tq) % 2 == 0
    kb = min(1024, s)
    wlen = NSA_WINDOW + 4 * tq
    wpad = wlen - 2 * tq
    ci = np.arange(ncmp) * NSA_CMP_STRIDE
    sj = np.arange(nblk) * NSA_SEL_BLOCK
    ov = np.clip(np.minimum(ci[None, :] + NSA_CMP_BLOCK, sj[:, None] + NSA_SEL_BLOCK)
                 - np.maximum(ci[None, :], sj[:, None]), 0, None).astype(np.float32) / NSA_CMP_STRIDE
    ov_t = np.zeros((LANES, ncmp), np.float32)
    ov_t[:nblk] = ov
    tloc = (np.arange(rows) % tq)[None, :]
    kcol = np.arange(2 * tq)[:, None]
    tri = np.stack([
        np.where(kcol <= tloc, 0.0, NEG),
        np.where(kcol < tq, 0.0, np.where(kcol - tq <= tloc, 0.0, NEG)),
    ]).astype(np.float32)
    wcol = np.arange(wlen)[:, None]
    wb = np.stack([
        np.where((wcol <= wpad - par * tq + tloc) & (wcol > wpad - par * tq - NSA_WINDOW + tloc), 0.0, NEG)
        for par in range(2)]).astype(np.float32)
    front = np.zeros((wpad, 2 * LANES), np.float32)
    front[:, LANES + NSA_D] = 1.0
    kw_p = jnp.concatenate([jnp.broadcast_to(jnp.asarray(front, dtype=BF16), (b, g, wpad, 2 * LANES)), kw3,
                            jnp.zeros((b, g, 2 * tq, 2 * LANES), BF16)], axis=2)
    vw_p = jnp.pad(vw_t, ((0, 0), (0, 0), (0, 0), (wpad, 2 * tq)))
    sp = s + wpad + 2 * tq
    seq = lambda n, w: pl.BlockSpec((None, g, n, w), lambda ib, ii: (ib, 0, 0, 0))
    full = lambda shape: pl.BlockSpec(shape, lambda ib, ii: (0,) * len(shape))
    return pl.pallas_call(
        functools.partial(_nsa_attn_kernel, kb=kb, wlen=wlen),
        grid=(b, s // (2 * tq)),
        in_specs=[
            pl.BlockSpec((None, 2 * tq, NSA_W), lambda ib, ii: (ib, ii, 0)),
            pl.BlockSpec((None, g, 2 * tq, LANES), lambda ib, ii: (ib, 0, ii, 0)),
            seq(ncmp, LANES), seq(NSA_VROWS, ncmp), seq(s, 2 * LANES), seq(NSA_VROWS, s),
            seq(sp, 2 * LANES), seq(NSA_VROWS, sp),
            full((LANES, ncmp)), full((2, 2 * tq, rows)), full((2, wlen, rows)),
        ],
        out_specs=pl.BlockSpec((None, 2 * tq, NSA_W), lambda ib, ii: (ib, ii, 0)),
        out_shape=jax.ShapeDtypeStruct((b, s, NSA_W), BF16),
        compiler_params=_cparams(("arbitrary", "arbitrary")),
        name="nsa_attn",
    )(q, gsig, kcmp, vcmp_t, ks3, vs_t, kw_p, vw_p, jnp.asarray(ov_t), jnp.asarray(tri), jnp.asarray(wb))


def _mem_kv_kernel(mem_ref, nw_ref, w_ref, knw_ref, k_ref, v_ref):
    x = mem_ref[...]
    ms = jnp.mean(x * x, axis=-1, keepdims=True)
    xn = x * lax.rsqrt(ms + EPS) * nw_ref[...]
    kv = _mm(xn, w_ref[...])
    v_ref[...] = kv[:, MEM_W:].astype(v_ref.dtype)
    for h in range(MEM_HEADS):
        hs = slice(h * MEM_D, (h + 1) * MEM_D)
        kh = kv[:, hs]
        msk = jnp.mean(kh * kh, axis=-1, keepdims=True)
        k_ref[:, hs] = (kh * lax.rsqrt(msk + EPS) * knw_ref[...]).astype(k_ref.dtype)


def _mem_kv(mem, mem_norm_w, w_kv, k_norm_w):
    b, m, d = mem.shape
    full = lambda shape: pl.BlockSpec(shape, lambda ib: (0,) * len(shape))
    blk = lambda w: pl.BlockSpec((None, m, w), lambda ib: (ib, 0, 0))
    return pl.pallas_call(
        _mem_kv_kernel,
        grid=(b,),
        in_specs=[blk(d), full((1, d)), full((d, 2 * MEM_W)), full((1, MEM_D))],
        out_specs=[blk(MEM_W), blk(MEM_W)],
        out_shape=[jax.ShapeDtypeStruct((b, m, MEM_W), BF16)] * 2,
        compiler_params=_cparams(("arbitrary",)),
        name="mem_kv",
    )(mem, mem_norm_w.astype(F32)[None, :], w_kv.astype(BF16), k_norm_w.astype(F32)[None, :])


def _mem_attn_kernel(q_ref, k_ref, v_ref, qnw_ref, o_ref):
    scale = MEM_D ** -0.5
    for h in range(MEM_HEADS):
        hs = slice(h * MEM_D, (h + 1) * MEM_D)
        qh = q_ref[:, hs]
        ms = jnp.mean(qh * qh, axis=-1, keepdims=True)
        qn = qh * lax.rsqrt(ms + EPS) * qnw_ref[...]
        s = _mm_nt(qn, k_ref[:, hs]) * scale
        m = jnp.max(s, axis=-1, keepdims=True)
        p = jnp.exp(s - m)
        l = jnp.sum(p, axis=-1, keepdims=True)
        o_ref[:, hs] = (_mm(p, v_ref[:, hs]) * (1.0 / l)).astype(o_ref.dtype)


def _mem_attn(mq, k, v, q_norm_w, ts):
    b, s, _ = mq.shape
    m = k.shape[1]
    return pl.pallas_call(
        _mem_attn_kernel,
        grid=(b, s // ts),
        in_specs=[
            pl.BlockSpec((None, ts, MEM_W), lambda ib, ij: (ib, ij, 0)),
            pl.BlockSpec((None, m, MEM_W), lambda ib, ij: (ib, 0, 0)),
            pl.BlockSpec((None, m, MEM_W), lambda ib, ij: (ib, 0, 0)),
            pl.BlockSpec((1, MEM_D), lambda ib, ij: (0, 0)),
        ],
        out_specs=pl.BlockSpec((None, ts, MEM_W), lambda ib, ij: (ib, ij, 0)),
        out_shape=jax.ShapeDtypeStruct((b, s, MEM_W), BF16),
        compiler_params=_cparams(("arbitrary", "arbitrary")),
        name="mem_attn",
    )(mq, k, v, q_norm_w.astype(F32)[None, :])


def _out_proj_kernel(x_ref, oa_ref, ob_ref, oc_ref, w_ref, h_ref):
    acc = x_ref[...]
    off = 0
    for o_ref in (oa_ref, ob_ref, oc_ref):
        wd = o_ref.shape[-1]
        acc = acc + jnp.dot(o_ref[...].astype(BF16), w_ref[off:off + wd, :], preferred_element_type=F32)
        off += wd
    h_ref[...] = acc


def _out_proj(x2, oa, ob, oc, w_out, tm):
    t, d = x2.shape
    row = lambda w: pl.BlockSpec((tm, w), lambda i: (i, 0))
    return pl.pallas_call(
        _out_proj_kernel,
        grid=(t // tm,),
        in_specs=[row(d), row(oa.shape[1]), row(ob.shape[1]), row(oc.shape[1]),
                  pl.BlockSpec(w_out.shape, lambda i: (0, 0))],
        out_specs=row(d),
        out_shape=jax.ShapeDtypeStruct((t, d), F32),
        compiler_params=_cparams(("arbitrary",)),
        name="out_proj",
    )(x2, oa, ob, oc, w_out)


def _ffn_kernel(h_ref, halo_ref, nw_ref, wup_ref, cw_ref, wdn_ref, o_ref, hn_s, u_s, act_s, *, fc):
    j = pl.program_id(1)
    ts = h_ref.shape[0]
    f = wdn_ref.shape[0]
    hl = halo_ref.shape[0]

    def norm(x):
        ms = jnp.mean(x * x, axis=-1, keepdims=True)
        return (x * lax.rsqrt(ms + EPS) * nw_ref[...]).astype(BF16)

    halo = jnp.where(j > 0, halo_ref[...], 0.0)
    hn_s[0:hl, :] = norm(halo)
    hn_s[hl:hl + ts, :] = norm(h_ref[...])
    for ic in range(f // fc):
        hn = hn_s[...]
        slot = ic % 2
        for part in range(2):
            cols = slice(part * f + ic * fc, part * f + (ic + 1) * fc)
            u_s[slot, part] = jnp.dot(hn, wup_ref[:, cols], preferred_element_type=F32)
        conv = []
        for part in range(2):
            cols = slice(part * f + ic * fc, part * f + (ic + 1) * fc)
            acc = cw_ref[FFN_CONV - 1:FFN_CONV, cols] * u_s[slot, part, hl:hl + ts, :]
            for jj in range(FFN_CONV - 1):
                acc = acc + cw_ref[jj:jj + 1, cols] * u_s[slot, part, pl.ds(hl - (FFN_CONV - 1) + jj, ts), :]
            conv.append(acc)
        act_s[:, ic * fc:(ic + 1) * fc] = (_silu(conv[0]) * conv[1]).astype(BF16)
    o_ref[...] = h_ref[...] + jnp.dot(act_s[...], wdn_ref[...], preferred_element_type=F32)


def _ffn(h, norm_w, w_up, conv_w, w_down, ts, fc):
    b, s, d = h.shape
    f = w_down.shape[0]
    full = lambda shape: pl.BlockSpec(shape, lambda ib, ij: (0,) * len(shape))
    return pl.pallas_call(
        functools.partial(_ffn_kernel, fc=fc),
        grid=(b, s // ts),
        in_specs=[
            pl.BlockSpec((None, ts, d), lambda ib, ij: (ib, ij, 0)),
            pl.BlockSpec((None, HALO_BF16, d), lambda ib, ij: (ib, jnp.maximum(ij * (ts // HALO_BF16) - 1, 0), 0)),
            full((1, d)), full((d, 2 * f)), full((FFN_CONV, 2 * f)), full((f, d)),
        ],
        out_specs=pl.BlockSpec((None, ts, d), lambda ib, ij: (ib, ij, 0)),
        out_shape=jax.ShapeDtypeStruct((b, s, d), F32),
        scratch_shapes=[
            pltpu.VMEM((ts + HALO_BF16, d), BF16),
            pltpu.VMEM((2, 2, ts + HALO_BF16, fc), F32),
            pltpu.VMEM((ts, f), BF16),
        ],
        compiler_params=_cparams(("arbitrary", "arbitrary")),
        name="ffn",
    )(h, h, norm_w.astype(F32)[None, :], w_up, conv_w.astype(F32), w_down)


def _split_w_in(w_in):
    sizes = (3 * GDN_W, GDN_HEADS, GDN_HEADS, GDN_W, NSA_W, NSA_KV_W, NSA_KV_W, NSA_KV_W, NSA_KV_W,
             NSA_KV_W, NSA_KV_W, 3 * NSA_HEADS, MEM_W)
    offs = np.concatenate([[0], np.cumsum(sizes)])
    (qkv, a, bb, gate, nq, kc, vc, ks, vs, kw, vw, ng, mq) = [w_in[:, offs[i]:offs[i + 1]] for i in range(len(sizes))]
    n_small = 2 * GDN_HEADS + 3 * NSA_HEADS
    small = jnp.concatenate([a, bb, ng, jnp.zeros((w_in.shape[0], LANES - n_small), w_in.dtype)], axis=1)
    widths = (3 * GDN_W, GDN_W, NSA_W, NSA_KV_W, NSA_KV_W, 4 * NSA_KV_W, MEM_W, LANES)
    w_cat = jnp.concatenate([qkv, gate, nq, kc, vc, ks, vs, kw, vw, mq, small], axis=1).astype(BF16)
    return w_cat, widths


def _layer(x, mem, attn_norm_w, mem_norm_w, w_in, gdn_conv_w, gdn_a_log, gdn_dt_bias, gdn_out_norm_w,
           nsa_q_norm_w, nsa_kc_norm_w, nsa_ks_norm_w, nsa_kw_norm_w, nsa_cmp_pos_k, nsa_cmp_pos_v,
           nsa_cmp_k_w1, nsa_cmp_k_w2, nsa_cmp_v_w1, nsa_cmp_v_w2, mem_w_kv, mem_q_norm_w, mem_k_norm_w,
           w_out, ffn_norm_w, ffn_w_up, ffn_conv_w, ffn_w_down):
    b, s, d = x.shape
    t = b * s
    ts = min(512, s)
    x2 = x.reshape(t, d)

    w_cat, widths = _split_w_in(w_in)
    qkv, gate, nq, kc, vc, kv4, mq, small = _in_proj(x2, attn_norm_w.astype(F32)[None, :], w_cat, gdn_conv_w,
                                                     widths, ts, s // ts)
    r3 = lambda a: a.reshape(b, s, a.shape[-1])

    o_a = _gdn(r3(qkv), r3(small), r3(gate), gdn_a_log, gdn_dt_bias, gdn_out_norm_w, ts)

    tabs = _rope_tables(s)
    q_r, ks3, vs2, kw2, vw2, gsig = _nsa_prep(r3(nq), r3(kv4), r3(small), tabs, nsa_q_norm_w, nsa_ks_norm_w,
                                              nsa_kw_norm_w, 2 * GDN_HEADS, ts)
    kcmp, vcmp = _nsa_compress(r3(kc), r3(vc), tabs, nsa_cmp_pos_k, nsa_cmp_pos_v, nsa_cmp_k_w1, nsa_cmp_k_w2,
                               nsa_cmp_v_w1, nsa_cmp_v_w2, nsa_kc_norm_w)
    o_b = _nsa_attn(q_r, gsig, kcmp, vcmp, ks3, vs2, kw2, vw2)

    mk, mv = _mem_kv(mem, mem_norm_w, mem_w_kv, mem_k_norm_w)
    o_c = _mem_attn(r3(mq), mk, mv, mem_q_norm_w, ts)

    h = _out_proj(x2, o_a.reshape(t, GDN_W), o_b.reshape(t, NSA_W), o_c.reshape(t, MEM_W), w_out.astype(BF16), ts)
    out = _ffn(h.reshape(b, s, d), ffn_norm_w, ffn_w_up.astype(BF16), ffn_conv_w, ffn_w_down.astype(BF16), ts, 256)
    return out


def kernel(x, mem, attn_norm_w, mem_norm_w, w_in, gdn_conv_w, gdn_a_log, gdn_dt_bias, gdn_out_norm_w, nsa_q_norm_w, nsa_kc_norm_w, nsa_ks_norm_w, nsa_kw_norm_w, nsa_cmp_pos_k, nsa_cmp_pos_v, nsa_cmp_k_w1, nsa_cmp_k_w2, nsa_cmp_v_w1, nsa_cmp_v_w2, mem_w_kv, mem_q_norm_w, mem_k_norm_w, w_out, ffn_norm_w, ffn_w_up, ffn_conv_w, ffn_w_down):
    h = x
    for l in range(w_in.shape[0]):
        h = _layer(h, mem, attn_norm_w[l], mem_norm_w[l], w_in[l], gdn_conv_w[l], gdn_a_log[l], gdn_dt_bias[l],
                   gdn_out_norm_w[l], nsa_q_norm_w[l], nsa_kc_norm_w[l], nsa_ks_norm_w[l], nsa_kw_norm_w[l],
                   nsa_cmp_pos_k[l], nsa_cmp_pos_v[l], nsa_cmp_k_w1[l], nsa_cmp_k_w2[l], nsa_cmp_v_w1[l],
                   nsa_cmp_v_w2[l], mem_w_kv[l], mem_q_norm_w[l], mem_k_norm_w[l], w_out[l], ffn_norm_w[l],
                   ffn_w_up[l], ffn_conv_w[l], ffn_w_down[l])
    return h
```

```python
import functools

import jax
import jax.numpy as jnp
import numpy as np
from jax import lax
from jax.experimental import pallas as pl
from jax.experimental.pallas import tpu as pltpu

F32 = jnp.float32
BF16 = jnp.bfloat16

EPS = 1e-6
ROPE_THETA = 500000.0
GDN_HEADS = 4
GDN_D = 128
GDN_CONV = 4
GDN_CHUNK = 64
GDN_SUB = 16
GDN_GROUP = 4
NSA_HEADS = 8
NSA_GROUPS = 2
NSA_REP = NSA_HEADS // NSA_GROUPS
NSA_D = 64
NSA_CMP_BLOCK = 32
NSA_CMP_STRIDE = 16
NSA_SEL_BLOCK = 64
NSA_N_SEL = 16
NSA_WINDOW = 512
NSA_Q_BLOCK = 64
NSA_ROPE_DIM = NSA_D // 4
NSA_SUB_KEYS = 256
NSA_VROWS = NSA_D + 16
MEM_HEADS = 4
MEM_D = 128
FFN_CONV = 3

GDN_W = GDN_HEADS * GDN_D
NSA_W = NSA_HEADS * NSA_D
MEM_W = MEM_HEADS * MEM_D
NSA_KV_W = NSA_GROUPS * NSA_D

LANES = 128
HALO = 8
HALO_BF16 = 16
VMEM_LIMIT = 56 * 1024 * 1024
NEG = -1e30
LOG2E = 1.4426950408889634


def _cparams(sem):
    return pltpu.CompilerParams(dimension_semantics=sem, vmem_limit_bytes=VMEM_LIMIT)


def _mm(a, b):
    return jnp.dot(a.astype(BF16), b.astype(BF16), preferred_element_type=F32)


def _mm_nt(a, b):
    return lax.dot_general(a.astype(BF16), b.astype(BF16), (((1,), (1,)), ((), ())),
                           preferred_element_type=F32)


def _mm_tn(a, b):
    return lax.dot_general(a.astype(BF16), b.astype(BF16), (((0,), (0,)), ((), ())),
                           preferred_element_type=F32)


def _split3(x):
    hi = x.astype(BF16)
    r = x - hi.astype(F32)
    mid = r.astype(BF16)
    lo = (r - mid.astype(F32)).astype(BF16)
    return hi, mid, lo


def _dot_exact_rhs(x, e):
    hi, mid, lo = _split3(x)
    eb = e.astype(BF16)
    return (jnp.dot(hi, eb, preferred_element_type=F32) + jnp.dot(mid, eb, preferred_element_type=F32)
            + jnp.dot(lo, eb, preferred_element_type=F32))


def _dot_exact_lhs(e, x):
    hi, mid, lo = _split3(x)
    eb = e.astype(BF16)
    return (jnp.dot(eb, hi, preferred_element_type=F32) + jnp.dot(eb, mid, preferred_element_type=F32)
            + jnp.dot(eb, lo, preferred_element_type=F32))


def _sigmoid(x):
    return 1.0 / (1.0 + jnp.exp2(x * (-LOG2E)))


def _silu(x):
    return x * _sigmoid(x)


def _softplus(x):
    return jnp.maximum(x, 0.0) + jnp.log(1.0 + jnp.exp(-jnp.abs(x)))


def _in_proj_kernel(x_ref, nw_ref, w_ref, cw_ref, *refs, tiles_per_seq):
    o_refs, xb = refs[:-1], refs[-1]
    i = pl.program_id(0)
    tm = x_ref.shape[0]
    hw = GDN_W
    x = x_ref[...]
    ms = jnp.mean(x * x, axis=-1, keepdims=True)
    xn = (x * lax.rsqrt(ms + EPS) * nw_ref[...]).astype(BF16)

    @pl.when(i % tiles_per_seq == 0)
    def _():
        xb[0:HALO, :] = jnp.zeros((HALO, 3 * hw), F32)

    qkv_ref = o_refs[0]
    xb[HALO:HALO + tm, :] = jnp.dot(xn, w_ref[:, 0:3 * hw], preferred_element_type=F32)
    off = 3 * hw
    for o_ref in o_refs[1:]:
        wd = o_ref.shape[-1]
        o_ref[...] = jnp.dot(xn, w_ref[:, off:off + wd], preferred_element_type=F32).astype(o_ref.dtype)
        off += wd
    for part in range(3):
        cols = slice(part * hw, (part + 1) * hw)
        acc = cw_ref[GDN_CONV - 1:GDN_CONV, cols] * xb[HALO:HALO + tm, cols]
        for jj in range(GDN_CONV - 1):
            acc = acc + cw_ref[jj:jj + 1, cols] * xb[pl.ds(HALO - (GDN_CONV - 1) + jj, tm), cols]
        act = _silu(acc)
        if part == 2:
            qkv_ref[:, cols] = act
        else:
            scale = GDN_D ** -0.5 if part == 0 else 1.0
            for h in range(GDN_HEADS):
                hs = slice(part * hw + h * GDN_D, part * hw + (h + 1) * GDN_D)
                xh = act[:, h * GDN_D:(h + 1) * GDN_D]
                ss = jnp.sum(xh * xh, axis=-1, keepdims=True)
                qkv_ref[:, hs] = xh * (lax.rsqrt(ss + EPS) * scale)
    xb[0:HALO, :] = xb[tm:tm + HALO, :]


def _in_proj(x2, norm_w, w_cat, conv_w, widths, tm, tiles_per_seq):
    t, d = x2.shape
    wtot = w_cat.shape[1]
    return pl.pallas_call(
        functools.partial(_in_proj_kernel, tiles_per_seq=tiles_per_seq),
        grid=(t // tm,),
        in_specs=[
            pl.BlockSpec((tm, d), lambda i: (i, 0)),
            pl.BlockSpec((1, d), lambda i: (0, 0)),
            pl.BlockSpec((d, wtot), lambda i: (0, 0)),
            pl.BlockSpec(conv_w.shape, lambda i: (0, 0)),
        ],
        out_specs=[pl.BlockSpec((tm, wd), lambda i: (i, 0)) for wd in widths],
        out_shape=[jax.ShapeDtypeStruct((t, wd), F32) for wd in widths],
        scratch_shapes=[pltpu.VMEM((tm + HALO, widths[0]), F32)],
        compiler_params=_cparams(("arbitrary",)),
        name="in_proj",
    )(x2, norm_w, w_cat, conv_w.astype(F32))


def _gdn_kernel(qkv_ref, sm_ref, gate_ref, alog_ref, dtb_ref, onw_ref, ea_ref, eb_ref, ltri_ref,
                o_ref, qn_s, kn_s, g_s, beta_s, u_s, w_s, aqk_s, egl_s, oacc_s, state_s):
    j = pl.program_id(1)
    ts = o_ref.shape[0]
    c = GDN_CHUNK
    hw = GDN_W

    @pl.when(j == 0)
    def _():
        state_s[...] = jnp.zeros_like(state_s)

    sm = sm_ref[...]
    a_full = _dot_exact_rhs(sm, ea_ref[...])
    b_full = _dot_exact_rhs(sm, eb_ref[...])
    g_s[...] = -jnp.exp(alog_ref[...]) * _softplus(a_full + dtb_ref[...])
    beta_s[...] = _sigmoid(b_full)

    ri = lax.broadcasted_iota(jnp.int32, (c, c), 0)
    ci = lax.broadcasted_iota(jnp.int32, (c, c), 1)
    causal = ri >= ci
    strict = ri > ci
    blockdiag = (ri // GDN_SUB) == (ci // GDN_SUB)
    ltri = ltri_ref[...]
    heads = [slice(h * GDN_D, (h + 1) * GDN_D) for h in range(GDN_HEADS)]

    def precompute(ig):
        base = ig * (GDN_GROUP * c)
        kb_l, kn_l, qn_l, rhs_l, decay_l, where_l = [], [], [], [], [], []
        for cc in range(GDN_GROUP):
            rows = pl.ds(base + cc * c, c)
            gc = _dot_exact_lhs(ltri, g_s[rows, :])
            glast = gc[c - 1:c, :]
            eg = jnp.exp(gc)
            beta = beta_s[rows, :]
            qn = qkv_ref[rows, 0:hw]
            kn = qkv_ref[rows, hw:2 * hw]
            kb = kn * beta
            vb = qkv_ref[rows, 2 * hw:3 * hw] * beta
            kbe = kb * eg
            qn_s[rows, :] = qn * eg
            kn_s[rows, :] = kn * jnp.exp(glast - gc)
            egl_s[pl.ds((ig * GDN_GROUP + cc) * HALO, HALO), :] = jnp.broadcast_to(jnp.exp(glast), (HALO, hw))
            for h, hs in enumerate(heads):
                gcol = gc[:, h * GDN_D:h * GDN_D + c]
                grow = gc[:, hs].T[0:1, 0:c]
                diff = gcol - grow
                decay_l.append(jnp.where(causal, jnp.exp(jnp.where(causal, diff, 0.0)), 0.0))
                kb_l.append(kb[:, hs].astype(BF16))
                kn_l.append(kn[:, hs].astype(BF16))
                qn_l.append(qn[:, hs].astype(BF16))
                rhs_l.append(jnp.concatenate([vb[:, hs], kbe[:, hs]], axis=-1))
                where_l.append((rows, h, hs))
        n = len(where_l)
        kk = [_mm_nt(kb_l[i], kn_l[i]) for i in range(n)]
        qk = [_mm_nt(qn_l[i], kn_l[i]) for i in range(n)]
        for i, (rows, h, hs) in enumerate(where_l):
            aqk_s[rows, h * c:(h + 1) * c] = qk[i] * decay_l[i]
        p = [-jnp.where(strict, kk[i] * decay_l[i], 0.0) for i in range(n)]
        pd = [jnp.where(blockdiag, x, 0.0) for x in p]
        pn = [p[i] - pd[i] for i in range(n)]
        p2 = [_mm(x, x) for x in pd]
        p4 = [_mm(x, x) for x in p2]
        p8 = [_mm(x, x) for x in p4]
        a1 = [pd[i] + p2[i] + _mm(pd[i], p2[i]) for i in range(n)]
        a2 = [a1[i] + p4[i] + _mm(a1[i], p4[i]) for i in range(n)]
        a3 = [a2[i] + p8[i] + _mm(a2[i], p8[i]) for i in range(n)]
        nm = [pn[i] + _mm(a3[i], pn[i]) for i in range(n)]
        n2 = [_mm(x, x) for x in nm]
        bm = [nm[i] + n2[i] + _mm(nm[i], n2[i]) for i in range(n)]
        tm = [bm[i] + a3[i] + _mm(bm[i], a3[i]) for i in range(n)]
        for i, (rows, h, hs) in enumerate(where_l):
            sol = rhs_l[i] + _mm(tm[i], rhs_l[i])
            u_s[rows, hs] = sol[:, :GDN_D]
            w_s[rows, hs] = sol[:, GDN_D:]

    def scan_step(ic):
        rows = pl.ds(ic * c, c)
        egl = egl_s[pl.ds(ic * HALO, 1), :]
        st = [state_s[h] for h in range(GDN_HEADS)]
        stb = [x.astype(BF16) for x in st]
        ws = [_mm(w_s[rows, hs], stb[h]) for h, hs in enumerate(heads)]
        qs = [_mm(qn_s[rows, hs], stb[h]) for h, hs in enumerate(heads)]
        v_new = [u_s[rows, hs] - ws[h] for h, hs in enumerate(heads)]
        for h, hs in enumerate(heads):
            oacc_s[rows, hs] = qs[h] + _mm(aqk_s[rows, h * c:(h + 1) * c], v_new[h])
            state_s[h] = st[h] * egl[:, hs] + _mm_tn(kn_s[rows, hs], v_new[h])

    ngroups = ts // (GDN_GROUP * c)
    precompute(0)
    for ig in range(1, ngroups):
        precompute(ig)
        for ic in range((ig - 1) * GDN_GROUP, ig * GDN_GROUP):
            scan_step(ic)
    for ic in range((ngroups - 1) * GDN_GROUP, ngroups * GDN_GROUP):
        scan_step(ic)

    for h in range(GDN_HEADS):
        hs = slice(h * GDN_D, (h + 1) * GDN_D)
        oh = oacc_s[:, hs]
        ms = jnp.mean(oh * oh, axis=-1, keepdims=True)
        o_ref[:, hs] = (oh * lax.rsqrt(ms + EPS) * onw_ref[...] * _silu(gate_ref[:, hs])).astype(o_ref.dtype)


def _gdn(qkv, small, gate, a_log, dt_bias, out_norm_w, ts):
    b, s, _ = qkv.shape
    hw = GDN_W
    c = GDN_CHUNK
    rep = lambda v: jnp.repeat(v.astype(F32), GDN_D)[None, :]
    lane_head = np.arange(hw) // GDN_D
    ea = (np.arange(LANES)[:, None] == lane_head[None, :]).astype(np.float32)
    eb = (np.arange(LANES)[:, None] == (lane_head[None, :] + GDN_HEADS)).astype(np.float32)
    ltri = np.tril(np.ones((c, c), np.float32))
    full = lambda shape: pl.BlockSpec(shape, lambda ib, ij: (0,) * len(shape))
    return pl.pallas_call(
        _gdn_kernel,
        grid=(b, s // ts),
        in_specs=[
            pl.BlockSpec((None, ts, 3 * hw), lambda ib, ij: (ib, ij, 0)),
            pl.BlockSpec((None, ts, LANES), lambda ib, ij: (ib, ij, 0)),
            pl.BlockSpec((None, ts, hw), lambda ib, ij: (ib, ij, 0)),
            full((1, hw)), full((1, hw)), full((1, GDN_D)),
            full((LANES, hw)), full((LANES, hw)), full((c, c)),
        ],
        out_specs=pl.BlockSpec((None, ts, hw), lambda ib, ij: (ib, ij, 0)),
        out_shape=jax.ShapeDtypeStruct((b, s, hw), BF16),
        scratch_shapes=[
            pltpu.VMEM((ts, hw), F32), pltpu.VMEM((ts, hw), F32),
            pltpu.VMEM((ts, hw), F32), pltpu.VMEM((ts, hw), F32),
            pltpu.VMEM((ts, hw), F32), pltpu.VMEM((ts, hw), F32), pltpu.VMEM((ts, GDN_HEADS * c), F32),
            pltpu.VMEM((ts // c * HALO, hw), F32),
            pltpu.VMEM((ts, hw), F32),
            pltpu.VMEM((GDN_HEADS, GDN_D, GDN_D), F32),
        ],
        compiler_params=_cparams(("arbitrary", "arbitrary")),
        name="gdn",
    )(qkv, small, gate, rep(a_log), rep(dt_bias), out_norm_w.astype(F32)[None, :],
      jnp.asarray(ea), jnp.asarray(eb), jnp.asarray(ltri))


def _rope_tables(s):
    half = NSA_ROPE_DIM // 2
    pos = jnp.arange(s, dtype=F32)
    inv = 1.0 / (ROPE_THETA ** (jnp.arange(0, NSA_ROPE_DIM, 2, dtype=F32) / NSA_ROPE_DIM))
    ang = pos[:, None] * inv[None, :]
    cos, sin = jnp.cos(ang), jnp.sin(ang)
    one = jnp.ones((s, NSA_D - NSA_ROPE_DIM), F32)
    zero = jnp.zeros((s, NSA_D - NSA_ROPE_DIM), F32)
    zh = jnp.zeros((s, half), F32)
    tc = jnp.concatenate([cos, cos, one], axis=-1)
    ta = jnp.concatenate([-sin, zh, zero], axis=-1)
    tb = jnp.concatenate([zh, sin, zero], axis=-1)
    dup = lambda t: jnp.concatenate([t, t], axis=-1)
    return dup(tc), dup(ta), dup(tb)


def _rope(x, tc, ta, tb):
    half = NSA_ROPE_DIM // 2
    return x * tc + pltpu.roll(x, LANES - half, 1) * ta + pltpu.roll(x, half, 1) * tb


def _group_ms(x, ones_blk):
    return _dot_exact_rhs(x * x, ones_blk) * (1.0 / NSA_D)


def _dup_groups(x):
    r = pltpu.roll(x, NSA_D, 1)
    lane = lax.broadcasted_iota(jnp.int32, x.shape, 1)
    lo = lane < NSA_D
    return jnp.where(lo, x, r), jnp.where(lo, r, x)


def _vt_block(vt):
    n = vt.shape[1]
    tail = jnp.where(lax.broadcasted_iota(jnp.int32, (NSA_VROWS - NSA_D, n), 0) == 0, 1.0, 0.0)
    return jnp.concatenate([vt, tail], axis=0).astype(BF16)


def _nsa_prep_kernel(nq_ref, kv_ref, sm_ref, tc_ref, ta_ref, tb_ref, qw_ref, ksw_ref, kww_ref, ones_ref,
                     q_ref, ks_ref, vs_ref, kw_ref, vw_ref, gs_ref, *, gate_col0):
    j = pl.program_id(1)
    ts = nq_ref.shape[0]
    tc, ta, tb = tc_ref[...], ta_ref[...], tb_ref[...]
    ones_blk = ones_ref[...]
    scale = NSA_D ** -0.5 * LOG2E
    for p in range(NSA_W // LANES):
        cols = slice(p * LANES, (p + 1) * LANES)
        x = nq_ref[:, cols]
        xn = x * lax.rsqrt(_group_ms(x, ones_blk) + EPS) * qw_ref[...]
        q_ref[:, cols] = (_rope(xn, tc, ta, tb) * scale).astype(q_ref.dtype)
    for src, nw_ref, k_out, v_out in ((0, ksw_ref, ks_ref, vs_ref), (2, kww_ref, kw_ref, vw_ref)):
        k = kv_ref[:, src * LANES:(src + 1) * LANES]
        v = kv_ref[:, (src + 1) * LANES:(src + 2) * LANES]
        kn = k * lax.rsqrt(_group_ms(k, ones_blk) + EPS) * nw_ref[...]
        kr = _rope(kn, tc, ta, tb)
        vt = v.T
        for ig, kg in enumerate(_dup_groups(kr)):
            k_out[ig, :, 0:LANES] = kg.astype(k_out.dtype)
            v_out[ig] = _vt_block(vt[ig * NSA_D:(ig + 1) * NSA_D])
    pos = j * ts + lax.broadcasted_iota(jnp.int32, (ts, LANES), 0)
    lane = lax.broadcasted_iota(jnp.int32, (ts, LANES), 1)
    onehot = jnp.where(pos // NSA_SEL_BLOCK == lane, 1.0, 0.0).astype(ks_ref.dtype)
    sig = _sigmoid(sm_ref[...])
    for ig in range(NSA_GROUPS):
        ks_ref[ig, :, LANES:2 * LANES] = onehot
        kw_ref[ig, :, LANES:2 * LANES] = jnp.zeros((ts, LANES), kw_ref.dtype)
        gs_ref[ig] = pltpu.roll(sig, LANES - (gate_col0 + ig * NSA_REP * 3), 1)


def _nsa_prep(nq, kv4, small, tabs, q_norm_w, ks_norm_w, kw_norm_w, gate_col0, ts):
    b, s, _ = nq.shape
    g = NSA_GROUPS
    tile2 = lambda w: jnp.concatenate([w, w]).astype(F32)[None, :]
    ones_blk = np.kron(np.eye(2, dtype=np.float32), np.ones((NSA_D, NSA_D), np.float32))
    full = lambda shape: pl.BlockSpec(shape, lambda ib, ij: (0,) * len(shape))
    tok = lambda w: pl.BlockSpec((None, ts, w), lambda ib, ij: (ib, ij, 0))
    tab = pl.BlockSpec((ts, LANES), lambda ib, ij: (ij, 0))
    kv_out = lambda w: pl.BlockSpec((None, g, ts, w), lambda ib, ij: (ib, 0, ij, 0))
    kv_shape = lambda w, dt: jax.ShapeDtypeStruct((b, g, s, w), dt)
    vt_out = pl.BlockSpec((None, g, NSA_VROWS, ts), lambda ib, ij: (ib, 0, 0, ij))
    vt_shape = jax.ShapeDtypeStruct((b, g, NSA_VROWS, s), BF16)
    return pl.pallas_call(
        functools.partial(_nsa_prep_kernel, gate_col0=gate_col0),
        grid=(b, s // ts),
        in_specs=[tok(NSA_W), tok(4 * LANES), tok(LANES), tab, tab, tab,
                  full((1, LANES)), full((1, LANES)), full((1, LANES)), full((LANES, LANES))],
        out_specs=[tok(NSA_W), kv_out(2 * LANES), vt_out, kv_out(2 * LANES), vt_out, kv_out(LANES)],
        out_shape=[jax.ShapeDtypeStruct((b, s, NSA_W), BF16), kv_shape(2 * LANES, BF16), vt_shape,
                   kv_shape(2 * LANES, BF16), vt_shape, kv_shape(LANES, F32)],
        compiler_params=_cparams(("arbitrary", "arbitrary")),
        name="nsa_prep",
    )(nq, kv4, small, *tabs, tile2(q_norm_w), tile2(ks_norm_w), tile2(kw_norm_w), jnp.asarray(ones_blk))


def _nsa_compress_kernel(kc_ref, vc_ref, tc_ref, ta_ref, tb_ref, pk_ref, pv_ref, kw1_ref, kw2_ref,
                         vw1_ref, vw2_ref, nw_ref, ones_ref, kc_out, vc_out):
    st = NSA_CMP_STRIDE
    nrow = kc_ref.shape[0] // st
    outs = []
    for is_k in (True, False):
        src = kc_ref if is_k else vc_ref
        pos_ref, w1_ref, w2_ref = (pk_ref, kw1_ref, kw2_ref) if is_k else (pv_ref, vw1_ref, vw2_ref)
        first = second = None
        for l in range(st):
            rows = pl.ds(l, nrow, stride=st)
            x = src[rows, :]
            if is_k:
                x = _rope(x, tc_ref[rows, :], ta_ref[rows, :], tb_ref[rows, :])
            f = _mm(x + pos_ref[l:l + 1, :], w1_ref[l * LANES:(l + 1) * LANES, :])
            s2 = _mm(x + pos_ref[st + l:st + l + 1, :], w1_ref[(st + l) * LANES:(st + l + 1) * LANES, :])
            first = f if first is None else first + f
            second = s2 if second is None else second + s2
        y = first + pltpu.roll(second, nrow - 1, 0)
        if is_k:
            y = _mm(_silu(y), w2_ref[...])
            y = y * lax.rsqrt(_group_ms(y, ones_ref[...]) + EPS) * nw_ref[...]
        else:
            y = _mm_nt(w2_ref[...], _silu(y))
        outs.append(y)
    for ig, kg in enumerate(_dup_groups(outs[0])):
        kc_out[ig] = kg.astype(kc_out.dtype)
    for ig in range(NSA_GROUPS):
        vc_out[ig] = _vt_block(outs[1][ig * NSA_D:(ig + 1) * NSA_D])


def _nsa_compress(kc, vc, tabs, pos_k, pos_v, k_w1, k_w2, v_w1, v_w2, kc_norm_w):
    b, s, _ = kc.shape
    g = NSA_GROUPS
    st = NSA_CMP_STRIDE
    nrow = s // st
    width = st * LANES
    eye_g = jnp.eye(g, dtype=F32)

    def w1_blk(w1):
        wl = w1.reshape(NSA_CMP_BLOCK, NSA_D, NSA_D)
        return jnp.einsum("lde,gh->lgdhe", wl, eye_g).reshape(NSA_CMP_BLOCK * LANES, LANES).astype(BF16)

    def w2_blk(w2):
        return jnp.einsum("de,gh->gdhe", w2, eye_g).reshape(LANES, LANES).astype(BF16)

    def pos_rows(p):
        return jnp.concatenate([p, p], axis=-1).astype(F32)

    ones_blk = np.kron(np.eye(2, dtype=np.float32), np.ones((NSA_D, NSA_D), np.float32))
    full = lambda shape: pl.BlockSpec(shape, lambda ib: (0,) * len(shape))
    seq = pl.BlockSpec((None, s, LANES), lambda ib: (ib, 0, 0))
    out = pl.BlockSpec((None, g, nrow, LANES), lambda ib: (ib, 0, 0, 0))
    oshape = jax.ShapeDtypeStruct((b, g, nrow, LANES), BF16)
    return pl.pallas_call(
        _nsa_compress_kernel,
        grid=(b,),
        in_specs=[seq, seq, full((s, LANES)), full((s, LANES)), full((s, LANES)),
                  full((NSA_CMP_BLOCK, LANES)), full((NSA_CMP_BLOCK, LANES)),
                  full((2 * width, LANES)), full((LANES, LANES)), full((2 * width, LANES)), full((LANES, LANES)),
                  full((1, LANES)), full((LANES, LANES))],
        out_specs=[out, pl.BlockSpec((None, g, NSA_VROWS, nrow), lambda ib: (ib, 0, 0, 0))],
        out_shape=[oshape, jax.ShapeDtypeStruct((b, g, NSA_VROWS, nrow), BF16)],
        compiler_params=_cparams(("arbitrary",)),
        name="nsa_compress",
    )(kc, vc, *tabs, pos_rows(pos_k), pos_rows(pos_v), w1_blk(k_w1), w2_blk(k_w2),
      w1_blk(v_w1), w2_blk(v_w2).T, jnp.concatenate([kc_norm_w, kc_norm_w]).astype(F32)[None, :],
      jnp.asarray(ones_blk))


def _nsa_attn_kernel(q_ref, gs_ref, kc_ref, vct_ref, ks_ref, vst_ref, kw_ref, vwt_ref, ovt_ref, tri_ref, wb_ref,
                     o_ref, *, kb, wlen):
    jp = pl.program_id(1)
    tq = NSA_Q_BLOCK
    rep = NSA_REP
    rows = rep * tq
    pairw = 2 * tq
    halves = range(2)
    chains = [(g, h) for g in range(NSA_GROUPS) for h in halves]
    nc = len(chains)
    blk_i = [2 * jp + h for h in halves]
    ncmp = kc_ref.shape[1]
    nblk = ks_ref.shape[1] // NSA_SEL_BLOCK
    lsum = slice(NSA_D, NSA_D + 1)

    lane_q = lax.broadcasted_iota(jnp.int32, (tq, LANES), 1)
    qs = []
    for g, h in chains:
        pieces = []
        for r in range(rep):
            pair = g * (rep // 2) + r // 2
            tile = q_ref[h * tq:(h + 1) * tq, pair * LANES:(pair + 1) * LANES]
            keep = (lane_q < NSA_D) if r % 2 == 0 else (lane_q >= NSA_D)
            pieces.append(jnp.where(keep, tile, jnp.zeros_like(tile)))
        qs.append(jnp.concatenate(pieces, axis=0))

    sc_raw = [_mm_nt(kc_ref[g], qs[c]) for c, (g, h) in enumerate(chains)]
    base = pl.multiple_of(jp * pairw, pairw)
    sd = [_mm_nt(ks_ref[g, pl.ds(base, pairw), 0:LANES], qs[c]) + tri_ref[h]
          for c, (g, h) in enumerate(chains)]
    wq_bias = jnp.where(lax.broadcasted_iota(jnp.int32, (rows, LANES), 1) == NSA_D, NEG, 0.0).astype(BF16)
    wparts = [(o, min(NSA_SUB_KEYS, wlen - o)) for o in range(0, wlen, NSA_SUB_KEYS)]
    wrows = [pl.ds(base + o, n) for o, n in wparts]
    sw = [[_mm_nt(kw_ref[g, wrows[j], :], jnp.concatenate([qs[c], wq_bias], axis=1)) + wb_ref[h, o:o + n, :]
           for j, (o, n) in enumerate(wparts)] for c, (g, h) in enumerate(chains)]

    cend = lax.broadcasted_iota(jnp.int32, (ncmp, rows), 0) * NSA_CMP_STRIDE + (NSA_CMP_BLOCK - 1)
    tloc = lax.broadcasted_iota(jnp.int32, (ncmp, rows), 1) & (tq - 1)
    cmask = [cend - blk_i[h] * tq <= tloc for h in halves]
    sc = [jnp.where(cmask[h], sc_raw[c], NEG) for c, (g, h) in enumerate(chains)]
    mc = [jnp.max(sc[c], axis=0, keepdims=True) for c in range(nc)]
    pc = [jnp.where(cmask[h], jnp.exp2(sc[c] - mc[c]), 0.0) for c, (g, h) in enumerate(chains)]
    acc_c = [_mm(vct_ref[g], pc[c]) for c, (g, h) in enumerate(chains)]
    inv_c = [1.0 / jnp.maximum(acc_c[c][lsum, :], 1e-30) for c in range(nc)]

    m0 = [jnp.max(sd[c], axis=0, keepdims=True) for c in range(nc)]
    pd = [jnp.exp2(sd[c] - m0[c]) for c in range(nc)]
    a0 = [_mm(vst_ref[g, :, pl.ds(base, pairw)], pd[c]) for c, (g, h) in enumerate(chains)]

    acc_w = []
    for c, (g, h) in enumerate(chains):
        mw = jnp.max(sw[c][0], axis=0, keepdims=True)
        for j in range(1, len(wparts)):
            mw = jnp.maximum(mw, jnp.max(sw[c][j], axis=0, keepdims=True))
        a = _mm(vwt_ref[g, :, wrows[0]], jnp.exp2(sw[c][0] - mw))
        for j in range(1, len(wparts)):
            a = a + _mm(vwt_ref[g, :, wrows[j]], jnp.exp2(sw[c][j] - mw))
        acc_w.append(a)

    gates, o_cw = [], []
    for c, (g, h) in enumerate(chains):
        gt = gs_ref[g, h * tq:(h + 1) * tq, :].T
        gates.append([jnp.concatenate([gt[r * 3 + x:r * 3 + x + 1, :] for r in range(rep)], axis=1)
                      for x in range(3)])
        o_cw.append((gates[c][0] * inv_c[c]) * acc_c[c] + (gates[c][2] / acc_w[c][lsum, :]) * acc_w[c])

    lane_g = lax.broadcasted_iota(jnp.int32, (LANES, LANES), 1)
    imp_c = []
    for c in range(nc):
        pn = pc[c] * inv_c[c]
        folded = pn[:, 0:LANES] + pn[:, LANES:2 * LANES]
        psum = folded + pltpu.roll(folded, tq, 1)
        imp_c.append(_dot_exact_lhs(ovt_ref[...], psum))
    imp = [jnp.where(lane_g < tq, imp_c[h], imp_c[2 + h])[0:tq] for h in halves]
    blk = lax.broadcasted_iota(jnp.int32, (tq, LANES), 0)
    valid = [blk <= blk_i[h] for h in halves]

    def ranked():
        out = []
        for h in halves:
            forced = (blk == 0) | (blk == blk_i[h]) | (blk == blk_i[h] - 1)
            key = jnp.where(valid[h], jnp.where(forced, 0x7F000000, pltpu.bitcast(imp[h], jnp.int32)), -1)
            key = jnp.where(blk < nblk, key, -2)
            key_m1 = key - 1
            rank = jnp.zeros((tq, LANES), jnp.int32)
            for jb in range(nblk):
                ahead = key[jb:jb + 1, :] > jnp.where(blk > jb, key_m1, key)
                rank = rank + jnp.where(ahead, 1, 0)
            out.append(jnp.where(rank < NSA_N_SEL, 1.0, 0.0))
        return tuple(out)

    sel = lax.cond(blk_i[0] >= NSA_N_SEL, ranked, lambda: tuple(jnp.where(valid[h], 1.0, 0.0) for h in halves))
    q2 = [None] * nc
    for h in halves:
        selneg_t = jnp.where((sel[h] > 0.5) & (blk < 2 * jp), 0.0, NEG)
        selneg = jnp.concatenate([selneg_t, jnp.zeros((LANES - tq, LANES), F32)], axis=0).T
        for g in range(NSA_GROUPS):
            bias = selneg[g * tq:(g + 1) * tq].astype(BF16)
            q2[2 * g + h] = jnp.concatenate([qs[2 * g + h], jnp.concatenate([bias] * rep, axis=0)], axis=1)

    nsub = kb // NSA_SUB_KEYS

    def slc_body(ic, carry):
        m_i, acc = carry
        k0 = pl.multiple_of(ic * kb, kb)
        sub_rows = [pl.ds(k0 + j * NSA_SUB_KEYS, NSA_SUB_KEYS) for j in range(nsub)]
        m_out, acc_out = [None] * nc, [None] * nc

        def score(c):
            return [_mm_nt(ks_ref[chains[c][0], sub_rows[j], :], q2[c]) for j in range(nsub)]

        def update(c, s):
            m_chunk = s[0]
            for j in range(1, nsub):
                m_chunk = jnp.maximum(m_chunk, s[j])
            m_new = jnp.maximum(m_i[c], jnp.max(m_chunk, axis=0, keepdims=True))
            a = jnp.exp2(m_i[c] - m_new) * acc[c]
            for j in range(nsub):
                a = a + _mm(vst_ref[chains[c][0], :, sub_rows[j]], jnp.exp2(s[j] - m_new))
            m_out[c], acc_out[c] = m_new, a

        s_prev = score(0)
        for c in range(1, nc):
            s_next = score(c)
            update(c - 1, s_prev)
            s_prev = s_next
        update(nc - 1, s_prev)
        return tuple(m_out), tuple(acc_out)

    nch = (2 * jp * tq + kb - 1) // kb
    _, acc_s = lax.fori_loop(0, nch, slc_body, (tuple(m0), tuple(a0)))

    lo = lane_q < NSA_D
    for c, (g, h) in enumerate(chains):
        ot = (o_cw[c] + (gates[c][1] / acc_s[c][lsum, :]) * acc_s[c])
        ot = jnp.concatenate([ot, jnp.zeros((LANES - NSA_VROWS, rows), F32)], axis=0)
        heads = []
        for hp in range(rows // LANES):
            o_pair = ot[:, hp * LANES:(hp + 1) * LANES].T
            heads += [o_pair[0:tq], o_pair[tq:2 * tq]]
        for pr_ in range(rep // 2):
            pair = g * (rep // 2) + pr_
            o_ref[h * tq:(h + 1) * tq, pair * LANES:(pair + 1) * LANES] = jnp.where(
                lo, heads[2 * pr_], pltpu.roll(heads[2 * pr_ + 1], NSA_D, 1)).astype(o_ref.dtype)


def _nsa_attn(q, gsig, kcmp, vcmp_t, ks3, vs_t, kw3, vw_t):
    b, s, _ = q.shape
    g = NSA_GROUPS
    rep = NSA_REP
    tq = NSA_Q_BLOCK
    rows = rep * tq
    ncmp = kcmp.shape[2]
    nblk = s // NSA_SEL_BLOCK
    assert nblk <= tq and nblk <= NSA_D and (s // tq) % 2 == 0
    kb = min(1024, s)
    wpad = NSA_WINDOW
    wlen = NSA_WINDOW + 2 * tq
    ci = np.arange(ncmp) * NSA_CMP_STRIDE
    sj = np.arange(nblk) * NSA_SEL_BLOCK
    ov = np.clip(np.minimum(ci[None, :] + NSA_CMP_BLOCK, sj[:, None] + NSA_SEL_BLOCK)
                 - np.maximum(ci[None, :], sj[:, None]), 0, None).astype(np.float32) / NSA_CMP_STRIDE
    ov_t = np.zeros((LANES, ncmp), np.float32)
    ov_t[:nblk] = ov
    tloc = (np.arange(rows) % tq)[None, :]
    kcol = np.arange(2 * tq)[:, None]
    tri = np.stack([
        np.where(kcol <= tloc, 0.0, NEG),
        np.where(kcol < tq, 0.0, np.where(kcol - tq <= tloc, 0.0, NEG)),
    ]).astype(np.float32)
    wcol = np.arange(wlen)[:, None]
    wb = np.stack([
        np.where((wcol <= wpad + half * tq + tloc) & (wcol > half * tq + tloc), 0.0, NEG)
        for half in range(2)]).astype(np.float32)
    front = np.zeros((wpad, 2 * LANES), np.float32)
    front[:, LANES + NSA_D] = 1.0
    kw_p = jnp.concatenate([jnp.broadcast_to(jnp.asarray(front, dtype=BF16), (b, g, wpad, 2 * LANES)), kw3], axis=2)
    vw_p = jnp.pad(vw_t, ((0, 0), (0, 0), (0, 0), (wpad, 0)))
    sp = s + wpad
    seq = lambda n, w: pl.BlockSpec((None, g, n, w), lambda ib, ii: (ib, 0, 0, 0))
    full = lambda shape: pl.BlockSpec(shape, lambda ib, ii: (0,) * len(shape))
    return pl.pallas_call(
        functools.partial(_nsa_attn_kernel, kb=kb, wlen=wlen),
        grid=(b, s // (2 * tq)),
        in_specs=[
            pl.BlockSpec((None, 2 * tq, NSA_W), lambda ib, ii: (ib, ii, 0)),
            pl.BlockSpec((None, g, 2 * tq, LANES), lambda ib, ii: (ib, 0, ii, 0)),
            seq(ncmp, LANES), seq(NSA_VROWS, ncmp), seq(s, 2 * LANES), seq(NSA_VROWS, s),
            seq(sp, 2 * LANES), seq(NSA_VROWS, sp),
            full((LANES, ncmp)), full((2, 2 * tq, rows)), full((2, wlen, rows)),
        ],
        out_specs=pl.BlockSpec((None, 2 * tq, NSA_W), lambda ib, ii: (ib, ii, 0)),
        out_shape=jax.ShapeDtypeStruct((b, s, NSA_W), BF16),
        compiler_params=_cparams(("arbitrary", "arbitrary")),
        name="nsa_attn",
    )(q, gsig, kcmp, vcmp_t, ks3, vs_t, kw_p, vw_p, jnp.asarray(ov_t), jnp.asarray(tri), jnp.asarray(wb))


def _mem_kv_kernel(mem_ref, nw_ref, w_ref, knw_ref, k_ref, v_ref):
    x = mem_ref[...]
    ms = jnp.mean(x * x, axis=-1, keepdims=True)
    xn = x * lax.rsqrt(ms + EPS) * nw_ref[...]
    kv = _mm(xn, w_ref[...])
    v_ref[...] = kv[:, MEM_W:].astype(v_ref.dtype)
    for h in range(MEM_HEADS):
        hs = slice(h * MEM_D, (h + 1) * MEM_D)
        kh = kv[:, hs]
        msk = jnp.mean(kh * kh, axis=-1, keepdims=True)
        k_ref[:, hs] = (kh * lax.rsqrt(msk + EPS) * knw_ref[...]).astype(k_ref.dtype)


def _mem_kv(mem, mem_norm_w, w_kv, k_norm_w):
    b, m, d = mem.shape
    full = lambda shape: pl.BlockSpec(shape, lambda ib: (0,) * len(shape))
    blk = lambda w: pl.BlockSpec((None, m, w), lambda ib: (ib, 0, 0))
    return pl.pallas_call(
        _mem_kv_kernel,
        grid=(b,),
        in_specs=[blk(d), full((1, d)), full((d, 2 * MEM_W)), full((1, MEM_D))],
        out_specs=[blk(MEM_W), blk(MEM_W)],
        out_shape=[jax.ShapeDtypeStruct((b, m, MEM_W), BF16)] * 2,
        compiler_params=_cparams(("arbitrary",)),
        name="mem_kv",
    )(mem, mem_norm_w.astype(F32)[None, :], w_kv.astype(BF16), k_norm_w.astype(F32)[None, :])


def _out_proj_kernel(x_ref, oa_ref, ob_ref, mq_ref, mk_ref, mv_ref, qnw_ref, w_ref, h_ref):
    acc = x_ref[...]
    off = 0
    for o_ref in (oa_ref, ob_ref):
        wd = o_ref.shape[-1]
        acc = acc + jnp.dot(o_ref[...].astype(BF16), w_ref[off:off + wd, :], preferred_element_type=F32)
        off += wd
    scale = MEM_D ** -0.5
    for h in range(MEM_HEADS):
        hs = slice(h * MEM_D, (h + 1) * MEM_D)
        qh = mq_ref[:, hs]
        ms = jnp.mean(qh * qh, axis=-1, keepdims=True)
        qn = qh * lax.rsqrt(ms + EPS) * qnw_ref[...]
        s = _mm_nt(qn, mk_ref[:, hs]) * scale
        m = jnp.max(s, axis=-1, keepdims=True)
        p = jnp.exp(s - m)
        l = jnp.sum(p, axis=-1, keepdims=True)
        oc = _mm(p, mv_ref[:, hs]) * (1.0 / l)
        acc = acc + _mm(oc, w_ref[off + h * MEM_D:off + (h + 1) * MEM_D, :])
    h_ref[...] = acc


def _out_proj(x2, oa, ob, mq, mk, mv, q_norm_w, w_out, tm, tiles_per_seq):
    t, d = x2.shape
    m = mk.shape[1]
    row = lambda w: pl.BlockSpec((tm, w), lambda i: (i, 0))
    mem = pl.BlockSpec((None, m, MEM_W), lambda i: (i // tiles_per_seq, 0, 0))
    return pl.pallas_call(
        _out_proj_kernel,
        grid=(t // tm,),
        in_specs=[row(d), row(oa.shape[1]), row(ob.shape[1]), row(MEM_W), mem, mem,
                  pl.BlockSpec((1, MEM_D), lambda i: (0, 0)), pl.BlockSpec(w_out.shape, lambda i: (0, 0))],
        out_specs=row(d),
        out_shape=jax.ShapeDtypeStruct((t, d), F32),
        compiler_params=_cparams(("arbitrary",)),
        name="out_proj",
    )(x2, oa, ob, mq, mk, mv, q_norm_w.astype(F32)[None, :], w_out)


def _ffn_kernel(h_ref, halo_ref, nw_ref, wup_ref, cw_ref, wdn_ref, o_ref, hn_s, u_s, act_s, *, fc):
    j = pl.program_id(1)
    ts = h_ref.shape[0]
    f = wdn_ref.shape[0]
    hl = halo_ref.shape[0]

    def norm(x):
        ms = jnp.mean(x * x, axis=-1, keepdims=True)
        return (x * lax.rsqrt(ms + EPS) * nw_ref[...]).astype(BF16)

    halo = jnp.where(j > 0, halo_ref[...], 0.0)
    hn_s[0:hl, :] = norm(halo)
    hn_s[hl:hl + ts, :] = norm(h_ref[...])
    for ic in range(f // fc):
        hn = hn_s[...]
        slot = ic % 2
        for part in range(2):
            cols = slice(part * f + ic * fc, part * f + (ic + 1) * fc)
            u_s[slot, part] = jnp.dot(hn, wup_ref[:, cols], preferred_element_type=F32)
        conv = []
        for part in range(2):
            cols = slice(part * f + ic * fc, part * f + (ic + 1) * fc)
            acc = cw_ref[FFN_CONV - 1:FFN_CONV, cols] * u_s[slot, part, hl:hl + ts, :]
            for jj in range(FFN_CONV - 1):
                acc = acc + cw_ref[jj:jj + 1, cols] * u_s[slot, part, pl.ds(hl - (FFN_CONV - 1) + jj, ts), :]
            conv.append(acc)
        act_s[:, ic * fc:(ic + 1) * fc] = (_silu(conv[0]) * conv[1]).astype(BF16)
    o_ref[...] = h_ref[...] + jnp.dot(act_s[...], wdn_ref[...], preferred_element_type=F32)


def _ffn(h, norm_w, w_up, conv_w, w_down, ts, fc):
    b, s, d = h.shape
    f = w_down.shape[0]
    full = lambda shape: pl.BlockSpec(shape, lambda ib, ij: (0,) * len(shape))
    return pl.pallas_call(
        functools.partial(_ffn_kernel, fc=fc),
        grid=(b, s // ts),
        in_specs=[
            pl.BlockSpec((None, ts, d), lambda ib, ij: (ib, ij, 0)),
            pl.BlockSpec((None, HALO_BF16, d), lambda ib, ij: (ib, jnp.maximum(ij * (ts // HALO_BF16) - 1, 0), 0)),
            full((1, d)), full((d, 2 * f)), full((FFN_CONV, 2 * f)), full((f, d)),
        ],
        out_specs=pl.BlockSpec((None, ts, d), lambda ib, ij: (ib, ij, 0)),
        out_shape=jax.ShapeDtypeStruct((b, s, d), F32),
        scratch_shapes=[
            pltpu.VMEM((ts + HALO_BF16, d), BF16),
            pltpu.VMEM((2, 2, ts + HALO_BF16, fc), F32),
            pltpu.VMEM((ts, f), BF16),
        ],
        compiler_params=_cparams(("arbitrary", "arbitrary")),
        name="ffn",
    )(h, h, norm_w.astype(F32)[None, :], w_up, conv_w.astype(F32), w_down)


def _split_w_in(w_in):
    sizes = (3 * GDN_W, GDN_HEADS, GDN_HEADS, GDN_W, NSA_W, NSA_KV_W, NSA_KV_W, NSA_KV_W, NSA_KV_W,
             NSA_KV_W, NSA_KV_W, 3 * NSA_HEADS, MEM_W)
    offs = np.concatenate([[0], np.cumsum(sizes)])
    (qkv, a, bb, gate, nq, kc, vc, ks, vs, kw, vw, ng, mq) = [w_in[:, offs[i]:offs[i + 1]] for i in range(len(sizes))]
    n_small = 2 * GDN_HEADS + 3 * NSA_HEADS
    small = jnp.concatenate([a, bb, ng, jnp.zeros((w_in.shape[0], LANES - n_small), w_in.dtype)], axis=1)
    widths = (3 * GDN_W, GDN_W, NSA_W, NSA_KV_W, NSA_KV_W, 4 * NSA_KV_W, MEM_W, LANES)
    w_cat = jnp.concatenate([qkv, gate, nq, kc, vc, ks, vs, kw, vw, mq, small], axis=1).astype(BF16)
    return w_cat, widths


def _layer(x, mem, attn_norm_w, mem_norm_w, w_in, gdn_conv_w, gdn_a_log, gdn_dt_bias, gdn_out_norm_w,
           nsa_q_norm_w, nsa_kc_norm_w, nsa_ks_norm_w, nsa_kw_norm_w, nsa_cmp_pos_k, nsa_cmp_pos_v,
           nsa_cmp_k_w1, nsa_cmp_k_w2, nsa_cmp_v_w1, nsa_cmp_v_w2, mem_w_kv, mem_q_norm_w, mem_k_norm_w,
           w_out, ffn_norm_w, ffn_w_up, ffn_conv_w, ffn_w_down):
    b, s, d = x.shape
    t = b * s
    ts = min(512, s)
    x2 = x.reshape(t, d)

    w_cat, widths = _split_w_in(w_in)
    qkv, gate, nq, kc, vc, kv4, mq, small = _in_proj(x2, attn_norm_w.astype(F32)[None, :], w_cat, gdn_conv_w,
                                                     widths, ts, s // ts)
    r3 = lambda a: a.reshape(b, s, a.shape[-1])

    o_a = _gdn(r3(qkv), r3(small), r3(gate), gdn_a_log, gdn_dt_bias, gdn_out_norm_w, ts)

    tabs = _rope_tables(s)
    q_r, ks3, vs2, kw2, vw2, gsig = _nsa_prep(r3(nq), r3(kv4), r3(small), tabs, nsa_q_norm_w, nsa_ks_norm_w,
                                              nsa_kw_norm_w, 2 * GDN_HEADS, ts)
    kcmp, vcmp = _nsa_compress(r3(kc), r3(vc), tabs, nsa_cmp_pos_k, nsa_cmp_pos_v, nsa_cmp_k_w1, nsa_cmp_k_w2,
                               nsa_cmp_v_w1, nsa_cmp_v_w2, nsa_kc_norm_w)
    o_b = _nsa_attn(q_r, gsig, kcmp, vcmp, ks3, vs2, kw2, vw2)

    mk, mv = _mem_kv(mem, mem_norm_w, mem_w_kv, mem_k_norm_w)

    h = _out_proj(x2, o_a.reshape(t, GDN_W), o_b.reshape(t, NSA_W), mq, mk, mv, mem_q_norm_w, w_out.astype(BF16),
                  ts, s // ts)
    out = _ffn(h.reshape(b, s, d), ffn_norm_w, ffn_w_up.astype(BF16), ffn_conv_w, ffn_w_down.astype(BF16), ts, 256)
    return out


def kernel(x, mem, attn_norm_w, mem_norm_w, w_in, gdn_conv_w, gdn_a_log, gdn_dt_bias, gdn_out_norm_w, nsa_q_norm_w, nsa_kc_norm_w, nsa_ks_norm_w, nsa_kw_norm_w, nsa_cmp_pos_k, nsa_cmp_pos_v, nsa_cmp_k_w1, nsa_cmp_k_w2, nsa_cmp_v_w1, nsa_cmp_v_w2, mem_w_kv, mem_q_norm_w, mem_k_norm_w, w_out, ffn_norm_w, ffn_w_up, ffn_conv_w, ffn_w_down):
    h = x
    for l in range(w_in.shape[0]):
        h = _layer(h, mem, attn_norm_w[l], mem_norm_w[l], w_in[l], gdn_conv_w[l], gdn_a_log[l], gdn_dt_bias[l],
                   gdn_out_norm_w[l], nsa_q_norm_w[l], nsa_kc_norm_w[l], nsa_ks_norm_w[l], nsa_kw_norm_w[l],
                   nsa_cmp_pos_k[l], nsa_cmp_pos_v[l], nsa_cmp_k_w1[l], nsa_cmp_k_w2[l], nsa_cmp_v_w1[l],
                   nsa_cmp_v_w2[l], mem_w_kv[l], mem_q_norm_w[l], mem_k_norm_w[l], w_out[l], ffn_norm_w[l],
                   ffn_w_up[l], ffn_conv_w[l], ffn_w_down[l])
    return h
```

```python
import functools

import jax
import jax.numpy as jnp
import numpy as np
from jax import lax
from jax.experimental import pallas as pl
from jax.experimental.pallas import tpu as pltpu

F32 = jnp.float32
BF16 = jnp.bfloat16

EPS = 1e-6
ROPE_THETA = 500000.0
GDN_HEADS = 4
GDN_D = 128
GDN_CONV = 4
GDN_CHUNK = 64
GDN_SUB = 16
GDN_GROUP = 4
NSA_HEADS = 8
NSA_GROUPS = 2
NSA_REP = NSA_HEADS // NSA_GROUPS
NSA_D = 64
NSA_CMP_BLOCK = 32
NSA_CMP_STRIDE = 16
NSA_SEL_BLOCK = 64
NSA_N_SEL = 16
NSA_WINDOW = 512
NSA_Q_BLOCK = 64
NSA_ROPE_DIM = NSA_D // 4
NSA_SUB_KEYS = 256
NSA_VROWS = NSA_D + 16
MEM_HEADS = 4
MEM_D = 128
FFN_CONV = 3

GDN_W = GDN_HEADS * GDN_D
NSA_W = NSA_HEADS * NSA_D
MEM_W = MEM_HEADS * MEM_D
NSA_KV_W = NSA_GROUPS * NSA_D

LANES = 128
HALO = 8
HALO_BF16 = 16
VMEM_LIMIT = 56 * 1024 * 1024
NEG = -1e30
LOG2E = 1.4426950408889634


def _cparams(sem):
    return pltpu.CompilerParams(dimension_semantics=sem, vmem_limit_bytes=VMEM_LIMIT)


def _mm(a, b):
    return jnp.dot(a.astype(BF16), b.astype(BF16), preferred_element_type=F32)


def _mm_nt(a, b):
    return lax.dot_general(a.astype(BF16), b.astype(BF16), (((1,), (1,)), ((), ())),
                           preferred_element_type=F32)


def _mm_tn(a, b):
    return lax.dot_general(a.astype(BF16), b.astype(BF16), (((0,), (0,)), ((), ())),
                           preferred_element_type=F32)


def _split3(x):
    hi = x.astype(BF16)
    r = x - hi.astype(F32)
    mid = r.astype(BF16)
    lo = (r - mid.astype(F32)).astype(BF16)
    return hi, mid, lo


def _dot_exact_rhs(x, e):
    hi, mid, lo = _split3(x)
    eb = e.astype(BF16)
    return (jnp.dot(hi, eb, preferred_element_type=F32) + jnp.dot(mid, eb, preferred_element_type=F32)
            + jnp.dot(lo, eb, preferred_element_type=F32))


def _dot_exact_lhs(e, x):
    hi, mid, lo = _split3(x)
    eb = e.astype(BF16)
    return (jnp.dot(eb, hi, preferred_element_type=F32) + jnp.dot(eb, mid, preferred_element_type=F32)
            + jnp.dot(eb, lo, preferred_element_type=F32))


def _sigmoid(x):
    return 1.0 / (1.0 + jnp.exp2(x * (-LOG2E)))


def _silu(x):
    return x * _sigmoid(x)


def _softplus(x):
    return jnp.maximum(x, 0.0) + jnp.log(1.0 + jnp.exp(-jnp.abs(x)))


def _in_proj_kernel(x_ref, nw_ref, w_ref, cw_ref, *refs, tiles_per_seq):
    o_refs, xb = refs[:-1], refs[-1]
    i = pl.program_id(0)
    tm = x_ref.shape[0]
    hw = GDN_W
    x = x_ref[...]
    ms = jnp.mean(x * x, axis=-1, keepdims=True)
    xn = (x * lax.rsqrt(ms + EPS) * nw_ref[...]).astype(BF16)

    @pl.when(i % tiles_per_seq == 0)
    def _():
        xb[0:HALO, :] = jnp.zeros((HALO, 3 * hw), F32)

    qkv_ref = o_refs[0]
    xb[HALO:HALO + tm, :] = jnp.dot(xn, w_ref[:, 0:3 * hw], preferred_element_type=F32)
    off = 3 * hw
    rest = list(o_refs[1:])
    while rest:
        n = 2 if len(rest) > 1 and rest[0].shape[-1] == LANES and rest[1].shape[-1] == LANES else 1
        group, rest = rest[:n], rest[n:]
        wd = sum(r.shape[-1] for r in group)
        res = jnp.dot(xn, w_ref[:, off:off + wd], preferred_element_type=F32)
        col = 0
        for r in group:
            r[...] = res[:, col:col + r.shape[-1]].astype(r.dtype)
            col += r.shape[-1]
        off += wd
    for part in range(3):
        cols = slice(part * hw, (part + 1) * hw)
        acc = cw_ref[GDN_CONV - 1:GDN_CONV, cols] * xb[HALO:HALO + tm, cols]
        for jj in range(GDN_CONV - 1):
            acc = acc + cw_ref[jj:jj + 1, cols] * xb[pl.ds(HALO - (GDN_CONV - 1) + jj, tm), cols]
        act = _silu(acc)
        if part == 2:
            qkv_ref[:, cols] = act
        else:
            scale = GDN_D ** -0.5 if part == 0 else 1.0
            for h in range(GDN_HEADS):
                hs = slice(part * hw + h * GDN_D, part * hw + (h + 1) * GDN_D)
                xh = act[:, h * GDN_D:(h + 1) * GDN_D]
                ss = jnp.sum(xh * xh, axis=-1, keepdims=True)
                qkv_ref[:, hs] = xh * (lax.rsqrt(ss + EPS) * scale)
    xb[0:HALO, :] = xb[tm:tm + HALO, :]


def _in_proj(x2, norm_w, w_cat, conv_w, widths, tm, tiles_per_seq):
    t, d = x2.shape
    wtot = w_cat.shape[1]
    return pl.pallas_call(
        functools.partial(_in_proj_kernel, tiles_per_seq=tiles_per_seq),
        grid=(t // tm,),
        in_specs=[
            pl.BlockSpec((tm, d), lambda i: (i, 0)),
            pl.BlockSpec((1, d), lambda i: (0, 0)),
            pl.BlockSpec((d, wtot), lambda i: (0, 0)),
            pl.BlockSpec(conv_w.shape, lambda i: (0, 0)),
        ],
        out_specs=[pl.BlockSpec((tm, wd), lambda i: (i, 0)) for wd in widths],
        out_shape=[jax.ShapeDtypeStruct((t, wd), F32) for wd in widths],
        scratch_shapes=[pltpu.VMEM((tm + HALO, widths[0]), F32)],
        compiler_params=_cparams(("arbitrary",)),
        name="in_proj",
    )(x2, norm_w, w_cat, conv_w.astype(F32))


def _gdn_kernel(qkv_ref, sm_ref, gate_ref, alog_ref, dtb_ref, onw_ref, ea_ref, eb_ref, ltri_ref,
                o_ref, qn_s, kn_s, g_s, beta_s, u_s, w_s, aqk_s, egl_s, oacc_s, state_s):
    j = pl.program_id(1)
    ts = o_ref.shape[0]
    c = GDN_CHUNK
    hw = GDN_W

    @pl.when(j == 0)
    def _():
        state_s[...] = jnp.zeros_like(state_s)

    sm = sm_ref[...]
    a_full = _dot_exact_rhs(sm, ea_ref[...])
    b_full = _dot_exact_rhs(sm, eb_ref[...])
    g_s[...] = -jnp.exp(alog_ref[...]) * _softplus(a_full + dtb_ref[...])
    beta_s[...] = _sigmoid(b_full)

    ri = lax.broadcasted_iota(jnp.int32, (c, c), 0)
    ci = lax.broadcasted_iota(jnp.int32, (c, c), 1)
    causal = ri >= ci
    strict = ri > ci
    blockdiag = (ri // GDN_SUB) == (ci // GDN_SUB)
    ltri = ltri_ref[...]
    heads = [slice(h * GDN_D, (h + 1) * GDN_D) for h in range(GDN_HEADS)]

    def precompute(ig):
        base = ig * (GDN_GROUP * c)
        kb_l, kn_l, qn_l, rhs_l, decay_l, where_l = [], [], [], [], [], []
        for cc in range(GDN_GROUP):
            rows = pl.ds(base + cc * c, c)
            gc = _dot_exact_lhs(ltri, g_s[rows, :])
            glast = gc[c - 1:c, :]
            eg = jnp.exp(gc)
            beta = beta_s[rows, :]
            qn = qkv_ref[rows, 0:hw]
            kn = qkv_ref[rows, hw:2 * hw]
            kb = kn * beta
            vb = qkv_ref[rows, 2 * hw:3 * hw] * beta
            kbe = kb * eg
            qn_s[rows, :] = qn * eg
            kn_s[rows, :] = kn * jnp.exp(glast - gc)
            egl_s[pl.ds((ig * GDN_GROUP + cc) * HALO, HALO), :] = jnp.broadcast_to(jnp.exp(glast), (HALO, hw))
            for h, hs in enumerate(heads):
                gcol = gc[:, h * GDN_D:h * GDN_D + c]
                grow = gc[:, hs].T[0:1, 0:c]
                diff = gcol - grow
                decay_l.append(jnp.where(causal, jnp.exp(jnp.where(causal, diff, 0.0)), 0.0))
                kb_l.append(kb[:, hs].astype(BF16))
                kn_l.append(kn[:, hs].astype(BF16))
                qn_l.append(qn[:, hs].astype(BF16))
                rhs_l.append(jnp.concatenate([vb[:, hs], kbe[:, hs]], axis=-1))
                where_l.append((rows, h, hs))
        n = len(where_l)
        kk = [_mm_nt(kb_l[i], kn_l[i]) for i in range(n)]
        qk = [_mm_nt(qn_l[i], kn_l[i]) for i in range(n)]
        for i, (rows, h, hs) in enumerate(where_l):
            aqk_s[rows, h * c:(h + 1) * c] = qk[i] * decay_l[i]
        p = [-jnp.where(strict, kk[i] * decay_l[i], 0.0) for i in range(n)]
        pd = [jnp.where(blockdiag, x, 0.0) for x in p]
        pn = [p[i] - pd[i] for i in range(n)]
        p2 = [_mm(x, x) for x in pd]
        p4 = [_mm(x, x) for x in p2]
        p8 = [_mm(x, x) for x in p4]
        a1 = [pd[i] + p2[i] + _mm(pd[i], p2[i]) for i in range(n)]
        a2 = [a1[i] + p4[i] + _mm(a1[i], p4[i]) for i in range(n)]
        a3 = [a2[i] + p8[i] + _mm(a2[i], p8[i]) for i in range(n)]
        nm = [pn[i] + _mm(a3[i], pn[i]) for i in range(n)]
        n2 = [_mm(x, x) for x in nm]
        bm = [nm[i] + n2[i] + _mm(nm[i], n2[i]) for i in range(n)]
        tm = [bm[i] + a3[i] + _mm(bm[i], a3[i]) for i in range(n)]
        for i, (rows, h, hs) in enumerate(where_l):
            sol = rhs_l[i] + _mm(tm[i], rhs_l[i])
            u_s[rows, hs] = sol[:, :GDN_D]
            w_s[rows, hs] = sol[:, GDN_D:]

    def scan_step(ic):
        rows = pl.ds(ic * c, c)
        egl = egl_s[pl.ds(ic * HALO, 1), :]
        st = [state_s[h] for h in range(GDN_HEADS)]
        stb = [x.astype(BF16) for x in st]
        ws = [_mm(w_s[rows, hs], stb[h]) for h, hs in enumerate(heads)]
        qs = [_mm(qn_s[rows, hs], stb[h]) for h, hs in enumerate(heads)]
        v_new = [u_s[rows, hs] - ws[h] for h, hs in enumerate(heads)]
        for h, hs in enumerate(heads):
            oacc_s[rows, hs] = qs[h] + _mm(aqk_s[rows, h * c:(h + 1) * c], v_new[h])
            state_s[h] = st[h] * egl[:, hs] + _mm_tn(kn_s[rows, hs], v_new[h])

    ngroups = ts // (GDN_GROUP * c)
    precompute(0)
    for ig in range(1, ngroups):
        precompute(ig)
        for ic in range((ig - 1) * GDN_GROUP, ig * GDN_GROUP):
            scan_step(ic)
    for ic in range((ngroups - 1) * GDN_GROUP, ngroups * GDN_GROUP):
        scan_step(ic)

    for h in range(GDN_HEADS):
        hs = slice(h * GDN_D, (h + 1) * GDN_D)
        oh = oacc_s[:, hs]
        ms = jnp.mean(oh * oh, axis=-1, keepdims=True)
        o_ref[:, hs] = (oh * lax.rsqrt(ms + EPS) * onw_ref[...] * _silu(gate_ref[:, hs])).astype(o_ref.dtype)


def _gdn(qkv, small, gate, a_log, dt_bias, out_norm_w, ts):
    b, s, _ = qkv.shape
    hw = GDN_W
    c = GDN_CHUNK
    rep = lambda v: jnp.repeat(v.astype(F32), GDN_D)[None, :]
    lane_head = np.arange(hw) // GDN_D
    ea = (np.arange(LANES)[:, None] == lane_head[None, :]).astype(np.float32)
    eb = (np.arange(LANES)[:, None] == (lane_head[None, :] + GDN_HEADS)).astype(np.float32)
    ltri = np.tril(np.ones((c, c), np.float32))
    full = lambda shape: pl.BlockSpec(shape, lambda ib, ij: (0,) * len(shape))
    return pl.pallas_call(
        _gdn_kernel,
        grid=(b, s // ts),
        in_specs=[
            pl.BlockSpec((None, ts, 3 * hw), lambda ib, ij: (ib, ij, 0)),
            pl.BlockSpec((None, ts, LANES), lambda ib, ij: (ib, ij, 0)),
            pl.BlockSpec((None, ts, hw), lambda ib, ij: (ib, ij, 0)),
            full((1, hw)), full((1, hw)), full((1, GDN_D)),
            full((LANES, hw)), full((LANES, hw)), full((c, c)),
        ],
        out_specs=pl.BlockSpec((None, ts, hw), lambda ib, ij: (ib, ij, 0)),
        out_shape=jax.ShapeDtypeStruct((b, s, hw), BF16),
        scratch_shapes=[
            pltpu.VMEM((ts, hw), F32), pltpu.VMEM((ts, hw), F32),
            pltpu.VMEM((ts, hw), F32), pltpu.VMEM((ts, hw), F32),
            pltpu.VMEM((ts, hw), F32), pltpu.VMEM((ts, hw), F32), pltpu.VMEM((ts, GDN_HEADS * c), F32),
            pltpu.VMEM((ts // c * HALO, hw), F32),
            pltpu.VMEM((ts, hw), F32),
            pltpu.VMEM((GDN_HEADS, GDN_D, GDN_D), F32),
        ],
        compiler_params=_cparams(("arbitrary", "arbitrary")),
        name="gdn",
    )(qkv, small, gate, rep(a_log), rep(dt_bias), out_norm_w.astype(F32)[None, :],
      jnp.asarray(ea), jnp.asarray(eb), jnp.asarray(ltri))


def _rope_tables(s):
    half = NSA_ROPE_DIM // 2
    pos = jnp.arange(s, dtype=F32)
    inv = 1.0 / (ROPE_THETA ** (jnp.arange(0, NSA_ROPE_DIM, 2, dtype=F32) / NSA_ROPE_DIM))
    ang = pos[:, None] * inv[None, :]
    cos, sin = jnp.cos(ang), jnp.sin(ang)
    one = jnp.ones((s, NSA_D - NSA_ROPE_DIM), F32)
    zero = jnp.zeros((s, NSA_D - NSA_ROPE_DIM), F32)
    zh = jnp.zeros((s, half), F32)
    tc = jnp.concatenate([cos, cos, one], axis=-1)
    ta = jnp.concatenate([-sin, zh, zero], axis=-1)
    tb = jnp.concatenate([zh, sin, zero], axis=-1)
    dup = lambda t: jnp.concatenate([t, t], axis=-1)
    return dup(tc), dup(ta), dup(tb)


def _rope(x, tc, ta, tb):
    half = NSA_ROPE_DIM // 2
    return x * tc + pltpu.roll(x, LANES - half, 1) * ta + pltpu.roll(x, half, 1) * tb


def _group_ms(x, ones_blk):
    return _dot_exact_rhs(x * x, ones_blk) * (1.0 / NSA_D)


def _dup_groups(x):
    r = pltpu.roll(x, NSA_D, 1)
    lane = lax.broadcasted_iota(jnp.int32, x.shape, 1)
    lo = lane < NSA_D
    return jnp.where(lo, x, r), jnp.where(lo, r, x)


def _vt_block(vt):
    n = vt.shape[1]
    tail = jnp.where(lax.broadcasted_iota(jnp.int32, (NSA_VROWS - NSA_D, n), 0) == 0, 1.0, 0.0)
    return jnp.concatenate([vt, tail], axis=0).astype(BF16)


def _nsa_prep_kernel(nq_ref, kv_ref, sm_ref, tc_ref, ta_ref, tb_ref, qw_ref, ksw_ref, kww_ref, ones_ref,
                     q_ref, ks_ref, vs_ref, kw_ref, vw_ref, gs_ref, *, gate_col0):
    j = pl.program_id(1)
    ts = nq_ref.shape[0]
    tc, ta, tb = tc_ref[...], ta_ref[...], tb_ref[...]
    ones_blk = ones_ref[...]
    scale = NSA_D ** -0.5 * LOG2E
    for p in range(NSA_W // LANES):
        cols = slice(p * LANES, (p + 1) * LANES)
        x = nq_ref[:, cols]
        xn = x * lax.rsqrt(_group_ms(x, ones_blk) + EPS) * qw_ref[...]
        q_ref[:, cols] = (_rope(xn, tc, ta, tb) * scale).astype(q_ref.dtype)
    for src, nw_ref, k_out, v_out in ((0, ksw_ref, ks_ref, vs_ref), (2, kww_ref, kw_ref, vw_ref)):
        k = kv_ref[:, src * LANES:(src + 1) * LANES]
        v = kv_ref[:, (src + 1) * LANES:(src + 2) * LANES]
        kn = k * lax.rsqrt(_group_ms(k, ones_blk) + EPS) * nw_ref[...]
        kr = _rope(kn, tc, ta, tb)
        vt = v.T
        for ig, kg in enumerate(_dup_groups(kr)):
            k_out[ig, :, 0:LANES] = kg.astype(k_out.dtype)
            v_out[ig] = _vt_block(vt[ig * NSA_D:(ig + 1) * NSA_D])
    pos = j * ts + lax.broadcasted_iota(jnp.int32, (ts, LANES), 0)
    lane = lax.broadcasted_iota(jnp.int32, (ts, LANES), 1)
    onehot = jnp.where(pos // NSA_SEL_BLOCK == lane, 1.0, 0.0).astype(ks_ref.dtype)
    sig = _sigmoid(sm_ref[...])
    for ig in range(NSA_GROUPS):
        ks_ref[ig, :, LANES:2 * LANES] = onehot
        kw_ref[ig, :, LANES:2 * LANES] = jnp.zeros((ts, LANES), kw_ref.dtype)
        gs_ref[ig] = pltpu.roll(sig, LANES - (gate_col0 + ig * NSA_REP * 3), 1)


def _nsa_prep(nq, kv4, small, tabs, q_norm_w, ks_norm_w, kw_norm_w, gate_col0, ts):
    b, s, _ = nq.shape
    g = NSA_GROUPS
    tile2 = lambda w: jnp.concatenate([w, w]).astype(F32)[None, :]
    ones_blk = np.kron(np.eye(2, dtype=np.float32), np.ones((NSA_D, NSA_D), np.float32))
    full = lambda shape: pl.BlockSpec(shape, lambda ib, ij: (0,) * len(shape))
    tok = lambda w: pl.BlockSpec((None, ts, w), lambda ib, ij: (ib, ij, 0))
    tab = pl.BlockSpec((ts, LANES), lambda ib, ij: (ij, 0))
    kv_out = lambda w: pl.BlockSpec((None, g, ts, w), lambda ib, ij: (ib, 0, ij, 0))
    kv_shape = lambda w, dt: jax.ShapeDtypeStruct((b, g, s, w), dt)
    vt_out = pl.BlockSpec((None, g, NSA_VROWS, ts), lambda ib, ij: (ib, 0, 0, ij))
    vt_shape = jax.ShapeDtypeStruct((b, g, NSA_VROWS, s), BF16)
    return pl.pallas_call(
        functools.partial(_nsa_prep_kernel, gate_col0=gate_col0),
        grid=(b, s // ts),
        in_specs=[tok(NSA_W), tok(4 * LANES), tok(LANES), tab, tab, tab,
                  full((1, LANES)), full((1, LANES)), full((1, LANES)), full((LANES, LANES))],
        out_specs=[tok(NSA_W), kv_out(2 * LANES), vt_out, kv_out(2 * LANES), vt_out, kv_out(LANES)],
        out_shape=[jax.ShapeDtypeStruct((b, s, NSA_W), BF16), kv_shape(2 * LANES, BF16), vt_shape,
                   kv_shape(2 * LANES, BF16), vt_shape, kv_shape(LANES, F32)],
        compiler_params=_cparams(("arbitrary", "arbitrary")),
        name="nsa_prep",
    )(nq, kv4, small, *tabs, tile2(q_norm_w), tile2(ks_norm_w), tile2(kw_norm_w), jnp.asarray(ones_blk))


def _nsa_compress_kernel(kc_ref, vc_ref, tc_ref, ta_ref, tb_ref, pk_ref, pv_ref, kw1_ref, kw2_ref,
                         vw1_ref, vw2_ref, nw_ref, ones_ref, kc_out, vc_out):
    st = NSA_CMP_STRIDE
    nrow = kc_ref.shape[0] // st
    outs = []
    for is_k in (True, False):
        src = kc_ref if is_k else vc_ref
        pos_ref, w1_ref, w2_ref = (pk_ref, kw1_ref, kw2_ref) if is_k else (pv_ref, vw1_ref, vw2_ref)
        first = second = None
        for l in range(st):
            rows = pl.ds(l, nrow, stride=st)
            x = src[rows, :]
            if is_k:
                x = _rope(x, tc_ref[rows, :], ta_ref[rows, :], tb_ref[rows, :])
            f = _mm(x + pos_ref[l:l + 1, :], w1_ref[l * LANES:(l + 1) * LANES, :])
            s2 = _mm(x + pos_ref[st + l:st + l + 1, :], w1_ref[(st + l) * LANES:(st + l + 1) * LANES, :])
            first = f if first is None else first + f
            second = s2 if second is None else second + s2
        y = first + pltpu.roll(second, nrow - 1, 0)
        if is_k:
            y = _mm(_silu(y), w2_ref[...])
            y = y * lax.rsqrt(_group_ms(y, ones_ref[...]) + EPS) * nw_ref[...]
        else:
            y = _mm_nt(w2_ref[...], _silu(y))
        outs.append(y)
    for ig, kg in enumerate(_dup_groups(outs[0])):
        kc_out[ig] = kg.astype(kc_out.dtype)
    for ig in range(NSA_GROUPS):
        vc_out[ig] = _vt_block(outs[1][ig * NSA_D:(ig + 1) * NSA_D])


def _nsa_compress(kc, vc, tabs, pos_k, pos_v, k_w1, k_w2, v_w1, v_w2, kc_norm_w):
    b, s, _ = kc.shape
    g = NSA_GROUPS
    st = NSA_CMP_STRIDE
    nrow = s // st
    width = st * LANES
    eye_g = jnp.eye(g, dtype=F32)

    def w1_blk(w1):
        wl = w1.reshape(NSA_CMP_BLOCK, NSA_D, NSA_D)
        return jnp.einsum("lde,gh->lgdhe", wl, eye_g).reshape(NSA_CMP_BLOCK * LANES, LANES).astype(BF16)

    def w2_blk(w2):
        return jnp.einsum("de,gh->gdhe", w2, eye_g).reshape(LANES, LANES).astype(BF16)

    def pos_rows(p):
        return jnp.concatenate([p, p], axis=-1).astype(F32)

    ones_blk = np.kron(np.eye(2, dtype=np.float32), np.ones((NSA_D, NSA_D), np.float32))
    full = lambda shape: pl.BlockSpec(shape, lambda ib: (0,) * len(shape))
    seq = pl.BlockSpec((None, s, LANES), lambda ib: (ib, 0, 0))
    out = pl.BlockSpec((None, g, nrow, LANES), lambda ib: (ib, 0, 0, 0))
    oshape = jax.ShapeDtypeStruct((b, g, nrow, LANES), BF16)
    return pl.pallas_call(
        _nsa_compress_kernel,
        grid=(b,),
        in_specs=[seq, seq, full((s, LANES)), full((s, LANES)), full((s, LANES)),
                  full((NSA_CMP_BLOCK, LANES)), full((NSA_CMP_BLOCK, LANES)),
                  full((2 * width, LANES)), full((LANES, LANES)), full((2 * width, LANES)), full((LANES, LANES)),
                  full((1, LANES)), full((LANES, LANES))],
        out_specs=[out, pl.BlockSpec((None, g, NSA_VROWS, nrow), lambda ib: (ib, 0, 0, 0))],
        out_shape=[oshape, jax.ShapeDtypeStruct((b, g, NSA_VROWS, nrow), BF16)],
        compiler_params=_cparams(("arbitrary",)),
        name="nsa_compress",
    )(kc, vc, *tabs, pos_rows(pos_k), pos_rows(pos_v), w1_blk(k_w1), w2_blk(k_w2),
      w1_blk(v_w1), w2_blk(v_w2).T, jnp.concatenate([kc_norm_w, kc_norm_w]).astype(F32)[None, :],
      jnp.asarray(ones_blk))


def _nsa_attn_kernel(q_ref, gs_ref, kc_ref, vct_ref, ks_ref, vst_ref, kw_ref, vwt_ref, ovt_ref, tri_ref, wb_ref,
                     o_ref, *, kb, wlen):
    jp = pl.program_id(1)
    tq = NSA_Q_BLOCK
    rep = NSA_REP
    rows = rep * tq
    pairw = 2 * tq
    halves = range(2)
    chains = [(g, h) for g in range(NSA_GROUPS) for h in halves]
    nc = len(chains)
    blk_i = [2 * jp + h for h in halves]
    ncmp = kc_ref.shape[1]
    nblk = ks_ref.shape[1] // NSA_SEL_BLOCK
    lsum = slice(NSA_D, NSA_D + 1)

    lane_q = lax.broadcasted_iota(jnp.int32, (tq, LANES), 1)
    qs = []
    for g, h in chains:
        pieces = []
        for r in range(rep):
            pair = g * (rep // 2) + r // 2
            tile = q_ref[h * tq:(h + 1) * tq, pair * LANES:(pair + 1) * LANES]
            keep = (lane_q < NSA_D) if r % 2 == 0 else (lane_q >= NSA_D)
            pieces.append(jnp.where(keep, tile, jnp.zeros_like(tile)))
        qs.append(jnp.concatenate(pieces, axis=0))

    sc_raw = [_mm_nt(kc_ref[g], qs[c]) for c, (g, h) in enumerate(chains)]
    base = pl.multiple_of(jp * pairw, pairw)
    sd = [_mm_nt(ks_ref[g, pl.ds(base, pairw), 0:LANES], qs[c]) + tri_ref[h]
          for c, (g, h) in enumerate(chains)]
    wq_bias = jnp.where(lax.broadcasted_iota(jnp.int32, (rows, LANES), 1) == NSA_D, NEG, 0.0).astype(BF16)
    wparts = [(o, min(NSA_SUB_KEYS, wlen - o)) for o in range(0, wlen, NSA_SUB_KEYS)]
    wrows = [pl.ds(base + o, n) for o, n in wparts]
    sw = [[_mm_nt(kw_ref[g, wrows[j], :], jnp.concatenate([qs[c], wq_bias], axis=1)) + wb_ref[h, o:o + n, :]
           for j, (o, n) in enumerate(wparts)] for c, (g, h) in enumerate(chains)]

    cend = lax.broadcasted_iota(jnp.int32, (ncmp, rows), 0) * NSA_CMP_STRIDE + (NSA_CMP_BLOCK - 1)
    tloc = lax.broadcasted_iota(jnp.int32, (ncmp, rows), 1) & (tq - 1)
    cmask = [cend - blk_i[h] * tq <= tloc for h in halves]
    sc = [jnp.where(cmask[h], sc_raw[c], NEG) for c, (g, h) in enumerate(chains)]
    mc = [jnp.max(sc[c], axis=0, keepdims=True) for c in range(nc)]
    pc = [jnp.where(cmask[h], jnp.exp2(sc[c] - mc[c]), 0.0) for c, (g, h) in enumerate(chains)]
    acc_c = [_mm(vct_ref[g], pc[c]) for c, (g, h) in enumerate(chains)]
    inv_c = [1.0 / jnp.maximum(acc_c[c][lsum, :], 1e-30) for c in range(nc)]

    m0 = [jnp.max(sd[c], axis=0, keepdims=True) for c in range(nc)]
    pd = [jnp.exp2(sd[c] - m0[c]) for c in range(nc)]
    a0 = [_mm(vst_ref[g, :, pl.ds(base, pairw)], pd[c]) for c, (g, h) in enumerate(chains)]

    acc_w = []
    for c, (g, h) in enumerate(chains):
        mw = jnp.max(sw[c][0], axis=0, keepdims=True)
        for j in range(1, len(wparts)):
            mw = jnp.maximum(mw, jnp.max(sw[c][j], axis=0, keepdims=True))
        a = _mm(vwt_ref[g, :, wrows[0]], jnp.exp2(sw[c][0] - mw))
        for j in range(1, len(wparts)):
            a = a + _mm(vwt_ref[g, :, wrows[j]], jnp.exp2(sw[c][j] - mw))
        acc_w.append(a)

    gates, o_cw = [], []
    for c, (g, h) in enumerate(chains):
        gt = gs_ref[g, h * tq:(h + 1) * tq, :].T
        gates.append([jnp.concatenate([gt[r * 3 + x:r * 3 + x + 1, :] for r in range(rep)], axis=1)
                      for x in range(3)])
        o_cw.append((gates[c][0] * inv_c[c]) * acc_c[c] + (gates[c][2] / acc_w[c][lsum, :]) * acc_w[c])

    lane_g = lax.broadcasted_iota(jnp.int32, (LANES, LANES), 1)
    imp_c = []
    for c in range(nc):
        pn = pc[c] * inv_c[c]
        folded = pn[:, 0:LANES] + pn[:, LANES:2 * LANES]
        psum = folded + pltpu.roll(folded, tq, 1)
        imp_c.append(_dot_exact_lhs(ovt_ref[...], psum))
    imp = [jnp.where(lane_g < tq, imp_c[h], imp_c[2 + h])[0:tq] for h in halves]
    blk = lax.broadcasted_iota(jnp.int32, (tq, LANES), 0)
    valid = [blk <= blk_i[h] for h in halves]

    def ranked():
        out = []
        for h in halves:
            forced = (blk == 0) | (blk == blk_i[h]) | (blk == blk_i[h] - 1)
            key = jnp.where(valid[h], jnp.where(forced, 0x7F000000, pltpu.bitcast(imp[h], jnp.int32)), -1)
            key = jnp.where(blk < nblk, key, -2)
            key_m1 = key - 1
            rank = jnp.zeros((tq, LANES), jnp.int32)
            for jb in range(nblk):
                ahead = key[jb:jb + 1, :] > jnp.where(blk > jb, key_m1, key)
                rank = rank + jnp.where(ahead, 1, 0)
            out.append(jnp.where(rank < NSA_N_SEL, 1.0, 0.0))
        return tuple(out)

    sel = lax.cond(blk_i[0] >= NSA_N_SEL, ranked, lambda: tuple(jnp.where(valid[h], 1.0, 0.0) for h in halves))
    q2 = [None] * nc
    for h in halves:
        selneg_t = jnp.where((sel[h] > 0.5) & (blk < 2 * jp), 0.0, NEG)
        selneg = jnp.concatenate([selneg_t, jnp.zeros((LANES - tq, LANES), F32)], axis=0).T
        for g in range(NSA_GROUPS):
            bias = selneg[g * tq:(g + 1) * tq].astype(BF16)
            q2[2 * g + h] = jnp.concatenate([qs[2 * g + h], jnp.concatenate([bias] * rep, axis=0)], axis=1)

    nsub = kb // NSA_SUB_KEYS

    def slc_body(ic, carry):
        m_i, acc = carry
        k0 = pl.multiple_of(ic * kb, kb)
        sub_rows = [pl.ds(k0 + j * NSA_SUB_KEYS, NSA_SUB_KEYS) for j in range(nsub)]
        m_out, acc_out = [None] * nc, [None] * nc

        def score(c):
            return [_mm_nt(ks_ref[chains[c][0], sub_rows[j], :], q2[c]) for j in range(nsub)]

        def update(c, s):
            m_chunk = s[0]
            for j in range(1, nsub):
                m_chunk = jnp.maximum(m_chunk, s[j])
            m_new = jnp.maximum(m_i[c], jnp.max(m_chunk, axis=0, keepdims=True))
            a = jnp.exp2(m_i[c] - m_new) * acc[c]
            for j in range(nsub):
                a = a + _mm(vst_ref[chains[c][0], :, sub_rows[j]], jnp.exp2(s[j] - m_new))
            m_out[c], acc_out[c] = m_new, a

        s_prev = score(0)
        for c in range(1, nc):
            s_next = score(c)
            update(c - 1, s_prev)
            s_prev = s_next
        update(nc - 1, s_prev)
        return tuple(m_out), tuple(acc_out)

    nch = (2 * jp * tq + kb - 1) // kb
    _, acc_s = lax.fori_loop(0, nch, slc_body, (tuple(m0), tuple(a0)))

    lo = lane_q < NSA_D
    for c, (g, h) in enumerate(chains):
        ot = (o_cw[c] + (gates[c][1] / acc_s[c][lsum, :]) * acc_s[c])
        ot = jnp.concatenate([ot, jnp.zeros((LANES - NSA_VROWS, rows), F32)], axis=0)
        heads = []
        for hp in range(rows // LANES):
            o_pair = ot[:, hp * LANES:(hp + 1) * LANES].T
            heads += [o_pair[0:tq], o_pair[tq:2 * tq]]
        for pr_ in range(rep // 2):
            pair = g * (rep // 2) + pr_
            o_ref[h * tq:(h + 1) * tq, pair * LANES:(pair + 1) * LANES] = jnp.where(
                lo, heads[2 * pr_], pltpu.roll(heads[2 * pr_ + 1], NSA_D, 1)).astype(o_ref.dtype)


def _nsa_attn(q, gsig, kcmp, vcmp_t, ks3, vs_t, kw3, vw_t):
    b, s, _ = q.shape
    g = NSA_GROUPS
    rep = NSA_REP
    tq = NSA_Q_BLOCK
    rows = rep * tq
    ncmp = kcmp.shape[2]
    nblk = s // NSA_SEL_BLOCK
    assert nblk <= tq and nblk <= NSA_D and (s // tq) % 2 == 0
    kb = min(1024, s)
    wpad = NSA_WINDOW
    wlen = NSA_WINDOW + 2 * tq
    ci = np.arange(ncmp) * NSA_CMP_STRIDE
    sj = np.arange(nblk) * NSA_SEL_BLOCK
    ov = np.clip(np.minimum(ci[None, :] + NSA_CMP_BLOCK, sj[:, None] + NSA_SEL_BLOCK)
                 - np.maximum(ci[None, :], sj[:, None]), 0, None).astype(np.float32) / NSA_CMP_STRIDE
    ov_t = np.zeros((LANES, ncmp), np.float32)
    ov_t[:nblk] = ov
    tloc = (np.arange(rows) % tq)[None, :]
    kcol = np.arange(2 * tq)[:, None]
    tri = np.stack([
        np.where(kcol <= tloc, 0.0, NEG),
        np.where(kcol < tq, 0.0, np.where(kcol - tq <= tloc, 0.0, NEG)),
    ]).astype(np.float32)
    wcol = np.arange(wlen)[:, None]
    wb = np.stack([
        np.where((wcol <= wpad + half * tq + tloc) & (wcol > half * tq + tloc), 0.0, NEG)
        for half in range(2)]).astype(np.float32)
    front = np.zeros((wpad, 2 * LANES), np.float32)
    front[:, LANES + NSA_D] = 1.0
    kw_p = jnp.concatenate([jnp.broadcast_to(jnp.asarray(front, dtype=BF16), (b, g, wpad, 2 * LANES)), kw3], axis=2)
    vw_p = jnp.pad(vw_t, ((0, 0), (0, 0), (0, 0), (wpad, 0)))
    sp = s + wpad
    seq = lambda n, w: pl.BlockSpec((None, g, n, w), lambda ib, ii: (ib, 0, 0, 0))
    full = lambda shape: pl.BlockSpec(shape, lambda ib, ii: (0,) * len(shape))
    return pl.pallas_call(
        functools.partial(_nsa_attn_kernel, kb=kb, wlen=wlen),
        grid=(b, s // (2 * tq)),
        in_specs=[
            pl.BlockSpec((None, 2 * tq, NSA_W), lambda ib, ii: (ib, ii, 0)),
            pl.BlockSpec((None, g, 2 * tq, LANES), lambda ib, ii: (ib, 0, ii, 0)),
            seq(ncmp, LANES), seq(NSA_VROWS, ncmp), seq(s, 2 * LANES), seq(NSA_VROWS, s),
            seq(sp, 2 * LANES), seq(NSA_VROWS, sp),
            full((LANES, ncmp)), full((2, 2 * tq, rows)), full((2, wlen, rows)),
        ],
        out_specs=pl.BlockSpec((None, 2 * tq, NSA_W), lambda ib, ii: (ib, ii, 0)),
        out_shape=jax.ShapeDtypeStruct((b, s, NSA_W), BF16),
        compiler_params=_cparams(("arbitrary", "arbitrary")),
        name="nsa_attn",
    )(q, gsig, kcmp, vcmp_t, ks3, vs_t, kw_p, vw_p, jnp.asarray(ov_t), jnp.asarray(tri), jnp.asarray(wb))


def _mem_kv_kernel(mem_ref, nw_ref, w_ref, knw_ref, k_ref, v_ref):
    x = mem_ref[...]
    ms = jnp.mean(x * x, axis=-1, keepdims=True)
    xn = x * lax.rsqrt(ms + EPS) * nw_ref[...]
    kv = _mm(xn, w_ref[...])
    v_ref[...] = kv[:, MEM_W:].astype(v_ref.dtype)
    for h in range(MEM_HEADS):
        hs = slice(h * MEM_D, (h + 1) * MEM_D)
        kh = kv[:, hs]
        msk = jnp.mean(kh * kh, axis=-1, keepdims=True)
        k_ref[:, hs] = (kh * lax.rsqrt(msk + EPS) * knw_ref[...]).astype(k_ref.dtype)


def _mem_kv(mem, mem_norm_w, w_kv, k_norm_w):
    b, m, d = mem.shape
    full = lambda shape: pl.BlockSpec(shape, lambda ib: (0,) * len(shape))
    blk = lambda w: pl.BlockSpec((None, m, w), lambda ib: (ib, 0, 0))
    return pl.pallas_call(
        _mem_kv_kernel,
        grid=(b,),
        in_specs=[blk(d), full((1, d)), full((d, 2 * MEM_W)), full((1, MEM_D))],
        out_specs=[blk(MEM_W), blk(MEM_W)],
        out_shape=[jax.ShapeDtypeStruct((b, m, MEM_W), BF16)] * 2,
        compiler_params=_cparams(("arbitrary",)),
        name="mem_kv",
    )(mem, mem_norm_w.astype(F32)[None, :], w_kv.astype(BF16), k_norm_w.astype(F32)[None, :])


def _mem_attn_kernel(q_ref, k_ref, v_ref, qnw_ref, o_ref):
    scale = MEM_D ** -0.5
    for h in range(MEM_HEADS):
        hs = slice(h * MEM_D, (h + 1) * MEM_D)
        qh = q_ref[:, hs]
        ms = jnp.mean(qh * qh, axis=-1, keepdims=True)
        qn = qh * lax.rsqrt(ms + EPS) * qnw_ref[...]
        s = _mm_nt(qn, k_ref[:, hs]) * scale
        m = jnp.max(s, axis=-1, keepdims=True)
        p = jnp.exp(s - m)
        l = jnp.sum(p, axis=-1, keepdims=True)
        o_ref[:, hs] = (_mm(p, v_ref[:, hs]) * (1.0 / l)).astype(o_ref.dtype)


def _mem_attn(mq, k, v, q_norm_w, ts):
    b, s, _ = mq.shape
    m = k.shape[1]
    return pl.pallas_call(
        _mem_attn_kernel,
        grid=(b, s // ts),
        in_specs=[
            pl.BlockSpec((None, ts, MEM_W), lambda ib, ij: (ib, ij, 0)),
            pl.BlockSpec((None, m, MEM_W), lambda ib, ij: (ib, 0, 0)),
            pl.BlockSpec((None, m, MEM_W), lambda ib, ij: (ib, 0, 0)),
            pl.BlockSpec((1, MEM_D), lambda ib, ij: (0, 0)),
        ],
        out_specs=pl.BlockSpec((None, ts, MEM_W), lambda ib, ij: (ib, ij, 0)),
        out_shape=jax.ShapeDtypeStruct((b, s, MEM_W), BF16),
        compiler_params=_cparams(("arbitrary", "arbitrary")),
        name="mem_attn",
    )(mq, k, v, q_norm_w.astype(F32)[None, :])


def _out_proj_kernel(x_ref, oa_ref, ob_ref, oc_ref, w_ref, h_ref):
    acc = x_ref[...]
    off = 0
    for o_ref in (oa_ref, ob_ref, oc_ref):
        wd = o_ref.shape[-1]
        acc = acc + jnp.dot(o_ref[...].astype(BF16), w_ref[off:off + wd, :], preferred_element_type=F32)
        off += wd
    h_ref[...] = acc


def _out_proj(x2, oa, ob, oc, w_out, tm):
    t, d = x2.shape
    row = lambda w: pl.BlockSpec((tm, w), lambda i: (i, 0))
    return pl.pallas_call(
        _out_proj_kernel,
        grid=(t // tm,),
        in_specs=[row(d), row(oa.shape[1]), row(ob.shape[1]), row(oc.shape[1]),
                  pl.BlockSpec(w_out.shape, lambda i: (0, 0))],
        out_specs=row(d),
        out_shape=jax.ShapeDtypeStruct((t, d), F32),
        compiler_params=_cparams(("arbitrary",)),
        name="out_proj",
    )(x2, oa, ob, oc, w_out)


def _ffn_kernel(h_ref, halo_ref, nw_ref, wup_ref, cw_ref, wdn_ref, o_ref, hn_s, u_s, act_s, *, fc):
    j = pl.program_id(1)
    ts = h_ref.shape[0]
    f = wdn_ref.shape[0]
    hl = halo_ref.shape[0]

    def norm(x):
        ms = jnp.mean(x * x, axis=-1, keepdims=True)
        return (x * lax.rsqrt(ms + EPS) * nw_ref[...]).astype(BF16)

    halo = jnp.where(j > 0, halo_ref[...], 0.0)
    hn_s[0:hl, :] = norm(halo)
    hn_s[hl:hl + ts, :] = norm(h_ref[...])
    for ic in range(f // fc):
        hn = hn_s[...]
        slot = ic % 2
        for part in range(2):
            cols = slice(part * f + ic * fc, part * f + (ic + 1) * fc)
            u_s[slot, part] = jnp.dot(hn, wup_ref[:, cols], preferred_element_type=F32)
        conv = []
        for part in range(2):
            cols = slice(part * f + ic * fc, part * f + (ic + 1) * fc)
            acc = cw_ref[FFN_CONV - 1:FFN_CONV, cols] * u_s[slot, part, hl:hl + ts, :]
            for jj in range(FFN_CONV - 1):
                acc = acc + cw_ref[jj:jj + 1, cols] * u_s[slot, part, pl.ds(hl - (FFN_CONV - 1) + jj, ts), :]
            conv.append(acc)
        act_s[:, ic * fc:(ic + 1) * fc] = (_silu(conv[0]) * conv[1]).astype(BF16)
    o_ref[...] = h_ref[...] + jnp.dot(act_s[...], wdn_ref[...], preferred_element_type=F32)


def _ffn(h, norm_w, w_up, conv_w, w_down, ts, fc):
    b, s, d = h.shape
    f = w_down.shape[0]
    full = lambda shape: pl.BlockSpec(shape, lambda ib, ij: (0,) * len(shape))
    return pl.pallas_call(
        functools.partial(_ffn_kernel, fc=fc),
        grid=(b, s // ts),
        in_specs=[
            pl.BlockSpec((None, ts, d), lambda ib, ij: (ib, ij, 0)),
            pl.BlockSpec((None, HALO_BF16, d), lambda ib, ij: (ib, jnp.maximum(ij * (ts // HALO_BF16) - 1, 0), 0)),
            full((1, d)), full((d, 2 * f)), full((FFN_CONV, 2 * f)), full((f, d)),
        ],
        out_specs=pl.BlockSpec((None, ts, d), lambda ib, ij: (ib, ij, 0)),
        out_shape=jax.ShapeDtypeStruct((b, s, d), F32),
        scratch_shapes=[
            pltpu.VMEM((ts + HALO_BF16, d), BF16),
            pltpu.VMEM((2, 2, ts + HALO_BF16, fc), F32),
            pltpu.VMEM((ts, f), BF16),
        ],
        compiler_params=_cparams(("arbitrary", "arbitrary")),
        name="ffn",
    )(h, h, norm_w.astype(F32)[None, :], w_up, conv_w.astype(F32), w_down)


def _split_w_in(w_in):
    sizes = (3 * GDN_W, GDN_HEADS, GDN_HEADS, GDN_W, NSA_W, NSA_KV_W, NSA_KV_W, NSA_KV_W, NSA_KV_W,
             NSA_KV_W, NSA_KV_W, 3 * NSA_HEADS, MEM_W)
    offs = np.concatenate([[0], np.cumsum(sizes)])
    (qkv, a, bb, gate, nq, kc, vc, ks, vs, kw, vw, ng, mq) = [w_in[:, offs[i]:offs[i + 1]] for i in range(len(sizes))]
    n_small = 2 * GDN_HEADS + 3 * NSA_HEADS
    small = jnp.concatenate([a, bb, ng, jnp.zeros((w_in.shape[0], LANES - n_small), w_in.dtype)], axis=1)
    widths = (3 * GDN_W, GDN_W, NSA_W, NSA_KV_W, NSA_KV_W, 4 * NSA_KV_W, MEM_W, LANES)
    w_cat = jnp.concatenate([qkv, gate, nq, kc, vc, ks, vs, kw, vw, mq, small], axis=1).astype(BF16)
    return w_cat, widths


def _layer(x, mem, attn_norm_w, mem_norm_w, w_in, gdn_conv_w, gdn_a_log, gdn_dt_bias, gdn_out_norm_w,
           nsa_q_norm_w, nsa_kc_norm_w, nsa_ks_norm_w, nsa_kw_norm_w, nsa_cmp_pos_k, nsa_cmp_pos_v,
           nsa_cmp_k_w1, nsa_cmp_k_w2, nsa_cmp_v_w1, nsa_cmp_v_w2, mem_w_kv, mem_q_norm_w, mem_k_norm_w,
           w_out, ffn_norm_w, ffn_w_up, ffn_conv_w, ffn_w_down):
    b, s, d = x.shape
    t = b * s
    ts = min(512, s)
    x2 = x.reshape(t, d)

    w_cat, widths = _split_w_in(w_in)
    qkv, gate, nq, kc, vc, kv4, mq, small = _in_proj(x2, attn_norm_w.astype(F32)[None, :], w_cat, gdn_conv_w,
                                                     widths, ts, s // ts)
    r3 = lambda a: a.reshape(b, s, a.shape[-1])

    o_a = _gdn(r3(qkv), r3(small), r3(gate), gdn_a_log, gdn_dt_bias, gdn_out_norm_w, ts)

    tabs = _rope_tables(s)
    q_r, ks3, vs2, kw2, vw2, gsig = _nsa_prep(r3(nq), r3(kv4), r3(small), tabs, nsa_q_norm_w, nsa_ks_norm_w,
                                              nsa_kw_norm_w, 2 * GDN_HEADS, ts)
    kcmp, vcmp = _nsa_compress(r3(kc), r3(vc), tabs, nsa_cmp_pos_k, nsa_cmp_pos_v, nsa_cmp_k_w1, nsa_cmp_k_w2,
                               nsa_cmp_v_w1, nsa_cmp_v_w2, nsa_kc_norm_w)
    o_b = _nsa_attn(q_r, gsig, kcmp, vcmp, ks3, vs2, kw2, vw2)

    mk, mv = _mem_kv(mem, mem_norm_w, mem_w_kv, mem_k_norm_w)
    o_c = _mem_attn(r3(mq), mk, mv, mem_q_norm_w, ts)

    h = _out_proj(x2, o_a.reshape(t, GDN_W), o_b.reshape(t, NSA_W), o_c.reshape(t, MEM_W), w_out.astype(BF16), ts)
    out = _ffn(h.reshape(b, s, d), ffn_norm_w, ffn_w_up.astype(BF16), ffn_conv_w, ffn_w_down.astype(BF16), ts, 256)
    return out


def kernel(x, mem, attn_norm_w, mem_norm_w, w_in, gdn_conv_w, gdn_a_log, gdn_dt_bias, gdn_out_norm_w, nsa_q_norm_w, nsa_kc_norm_w, nsa_ks_norm_w, nsa_kw_norm_w, nsa_cmp_pos_k, nsa_cmp_pos_v, nsa_cmp_k_w1, nsa_cmp_k_w2, nsa_cmp_v_w1, nsa_cmp_v_w2, mem_w_kv, mem_q_norm_w, mem_k_norm_w, w_out, ffn_norm_w, ffn_w_up, ffn_conv_w, ffn_w_down):
    h = x
    for l in range(w_in.shape[0]):
        h = _layer(h, mem, attn_norm_w[l], mem_norm_w[l], w_in[l], gdn_conv_w[l], gdn_a_log[l], gdn_dt_bias[l],
                   gdn_out_norm_w[l], nsa_q_norm_w[l], nsa_kc_norm_w[l], nsa_ks_norm_w[l], nsa_kw_norm_w[l],
                   nsa_cmp_pos_k[l], nsa_cmp_pos_v[l], nsa_cmp_k_w1[l], nsa_cmp_k_w2[l], nsa_cmp_v_w1[l],
                   nsa_cmp_v_w2[l], mem_w_kv[l], mem_q_norm_w[l], mem_k_norm_w[l], w_out[l], ffn_norm_w[l],
                   ffn_w_up[l], ffn_conv_w[l], ffn_w_down[l])
    return h
```

```python
import functools

import jax
import jax.numpy as jnp
import numpy as np
from jax import lax
from jax.experimental import pallas as pl
from jax.experimental.pallas import tpu as pltpu

F32 = jnp.float32
BF16 = jnp.bfloat16

EPS = 1e-6
ROPE_THETA = 500000.0
GDN_HEADS = 4
GDN_D = 128
GDN_CONV = 4
GDN_CHUNK = 64
GDN_SUB = 16
GDN_GROUP = 4
NSA_HEADS = 8
NSA_GROUPS = 2
NSA_REP = NSA_HEADS // NSA_GROUPS
NSA_D = 64
NSA_CMP_BLOCK = 32
NSA_CMP_STRIDE = 16
NSA_SEL_BLOCK = 64
NSA_N_SEL = 16
NSA_WINDOW = 512
NSA_Q_BLOCK = 64
NSA_ROPE_DIM = NSA_D // 4
NSA_SUB_KEYS = 256
NSA_VROWS = NSA_D + 16
MEM_HEADS = 4
MEM_D = 128
FFN_CONV = 3

GDN_W = GDN_HEADS * GDN_D
NSA_W = NSA_HEADS * NSA_D
MEM_W = MEM_HEADS * MEM_D
NSA_KV_W = NSA_GROUPS * NSA_D

LANES = 128
HALO = 8
HALO_BF16 = 16
OUT_X_SLOTS = 3
VMEM_LIMIT = 56 * 1024 * 1024
NEG = -1e30
LOG2E = 1.4426950408889634


def _cparams(sem):
    return pltpu.CompilerParams(dimension_semantics=sem, vmem_limit_bytes=VMEM_LIMIT)


def _mm(a, b):
    return jnp.dot(a.astype(BF16), b.astype(BF16), preferred_element_type=F32)


def _mm_nt(a, b):
    return lax.dot_general(a.astype(BF16), b.astype(BF16), (((1,), (1,)), ((), ())),
                           preferred_element_type=F32)


def _mm_tn(a, b):
    return lax.dot_general(a.astype(BF16), b.astype(BF16), (((0,), (0,)), ((), ())),
                           preferred_element_type=F32)


def _split3(x):
    hi = x.astype(BF16)
    r = x - hi.astype(F32)
    mid = r.astype(BF16)
    lo = (r - mid.astype(F32)).astype(BF16)
    return hi, mid, lo


def _dot_exact_rhs(x, e):
    hi, mid, lo = _split3(x)
    eb = e.astype(BF16)
    return (jnp.dot(hi, eb, preferred_element_type=F32) + jnp.dot(mid, eb, preferred_element_type=F32)
            + jnp.dot(lo, eb, preferred_element_type=F32))


def _dot_exact_lhs(e, x):
    hi, mid, lo = _split3(x)
    eb = e.astype(BF16)
    return (jnp.dot(eb, hi, preferred_element_type=F32) + jnp.dot(eb, mid, preferred_element_type=F32)
            + jnp.dot(eb, lo, preferred_element_type=F32))


def _sigmoid(x):
    return 1.0 / (1.0 + jnp.exp2(x * (-LOG2E)))


def _silu(x):
    return x * _sigmoid(x)


def _softplus(x):
    return jnp.maximum(x, 0.0) + jnp.log(1.0 + jnp.exp(-jnp.abs(x)))


def _in_proj_kernel(x_ref, nw_ref, w_ref, cw_ref, *refs, tiles_per_seq):
    o_refs, xb = refs[:-1], refs[-1]
    i = pl.program_id(0)
    tm = x_ref.shape[0]
    hw = GDN_W
    x = x_ref[...]
    ms = jnp.mean(x * x, axis=-1, keepdims=True)
    xn = (x * lax.rsqrt(ms + EPS) * nw_ref[...]).astype(BF16)

    @pl.when(i % tiles_per_seq == 0)
    def _():
        xb[0:HALO, :] = jnp.zeros((HALO, 3 * hw), F32)

    qkv_ref = o_refs[0]
    xb[HALO:HALO + tm, :] = jnp.dot(xn, w_ref[:, 0:3 * hw], preferred_element_type=F32)
    off = 3 * hw
    rest = list(o_refs[1:])
    while rest:
        n = 2 if len(rest) > 1 and rest[0].shape[-1] == LANES and rest[1].shape[-1] == LANES else 1
        group, rest = rest[:n], rest[n:]
        wd = sum(r.shape[-1] for r in group)
        res = jnp.dot(xn, w_ref[:, off:off + wd], preferred_element_type=F32)
        col = 0
        for r in group:
            r[...] = res[:, col:col + r.shape[-1]].astype(r.dtype)
            col += r.shape[-1]
        off += wd
    for part in range(3):
        cols = slice(part * hw, (part + 1) * hw)
        acc = cw_ref[GDN_CONV - 1:GDN_CONV, cols] * xb[HALO:HALO + tm, cols]
        for jj in range(GDN_CONV - 1):
            acc = acc + cw_ref[jj:jj + 1, cols] * xb[pl.ds(HALO - (GDN_CONV - 1) + jj, tm), cols]
        act = _silu(acc)
        if part == 2:
            qkv_ref[:, cols] = act
        else:
            scale = GDN_D ** -0.5 if part == 0 else 1.0
            for h in range(GDN_HEADS):
                hs = slice(part * hw + h * GDN_D, part * hw + (h + 1) * GDN_D)
                xh = act[:, h * GDN_D:(h + 1) * GDN_D]
                ss = jnp.sum(xh * xh, axis=-1, keepdims=True)
                qkv_ref[:, hs] = xh * (lax.rsqrt(ss + EPS) * scale)
    xb[0:HALO, :] = xb[tm:tm + HALO, :]


def _in_proj(x2, norm_w, w_cat, conv_w, widths, tm, tiles_per_seq):
    t, d = x2.shape
    wtot = w_cat.shape[1]
    return pl.pallas_call(
        functools.partial(_in_proj_kernel, tiles_per_seq=tiles_per_seq),
        grid=(t // tm,),
        in_specs=[
            pl.BlockSpec((tm, d), lambda i: (i, 0)),
            pl.BlockSpec((1, d), lambda i: (0, 0)),
            pl.BlockSpec((d, wtot), lambda i: (0, 0)),
            pl.BlockSpec(conv_w.shape, lambda i: (0, 0)),
        ],
        out_specs=[pl.BlockSpec((tm, wd), lambda i: (i, 0)) for wd in widths],
        out_shape=[jax.ShapeDtypeStruct((t, wd), F32) for wd in widths],
        scratch_shapes=[pltpu.VMEM((tm + HALO, widths[0]), F32)],
        compiler_params=_cparams(("arbitrary",)),
        name="in_proj",
    )(x2, norm_w, w_cat, conv_w.astype(F32))


def _gdn_kernel(qkv_ref, sm_ref, gate_ref, alog_ref, dtb_ref, onw_ref, ea_ref, eb_ref, ltri_ref,
                o_ref, qn_s, kn_s, g_s, beta_s, u_s, w_s, aqk_s, egl_s, oacc_s, state_s):
    j = pl.program_id(1)
    ts = o_ref.shape[0]
    c = GDN_CHUNK
    hw = GDN_W

    @pl.when(j == 0)
    def _():
        state_s[...] = jnp.zeros_like(state_s)

    sm = sm_ref[...]
    a_full = _dot_exact_rhs(sm, ea_ref[...])
    b_full = _dot_exact_rhs(sm, eb_ref[...])
    g_s[...] = -jnp.exp(alog_ref[...]) * _softplus(a_full + dtb_ref[...])
    beta_s[...] = _sigmoid(b_full)

    ri = lax.broadcasted_iota(jnp.int32, (c, c), 0)
    ci = lax.broadcasted_iota(jnp.int32, (c, c), 1)
    causal = ri >= ci
    strict = ri > ci
    blockdiag = (ri // GDN_SUB) == (ci // GDN_SUB)
    ltri = ltri_ref[...]
    heads = [slice(h * GDN_D, (h + 1) * GDN_D) for h in range(GDN_HEADS)]

    def precompute(ig):
        base = ig * (GDN_GROUP * c)
        kb_l, kn_l, qn_l, rhs_l, decay_l, where_l = [], [], [], [], [], []
        for cc in range(GDN_GROUP):
            rows = pl.ds(base + cc * c, c)
            gc = _dot_exact_lhs(ltri, g_s[rows, :])
            glast = gc[c - 1:c, :]
            eg = jnp.exp(gc)
            beta = beta_s[rows, :]
            qn = qkv_ref[rows, 0:hw]
            kn = qkv_ref[rows, hw:2 * hw]
            kb = kn * beta
            vb = qkv_ref[rows, 2 * hw:3 * hw] * beta
            kbe = kb * eg
            qn_s[rows, :] = qn * eg
            kn_s[rows, :] = kn * jnp.exp(glast - gc)
            egl_s[pl.ds((ig * GDN_GROUP + cc) * HALO, HALO), :] = jnp.broadcast_to(jnp.exp(glast), (HALO, hw))
            for h, hs in enumerate(heads):
                gcol = gc[:, h * GDN_D:h * GDN_D + c]
                grow = gc[:, hs].T[0:1, 0:c]
                diff = gcol - grow
                decay_l.append(jnp.where(causal, jnp.exp(jnp.where(causal, diff, 0.0)), 0.0))
                kb_l.append(kb[:, hs].astype(BF16))
                kn_l.append(kn[:, hs].astype(BF16))
                qn_l.append(qn[:, hs].astype(BF16))
                rhs_l.append(jnp.concatenate([vb[:, hs], kbe[:, hs]], axis=-1))
                where_l.append((rows, h, hs))
        n = len(where_l)
        kk = [_mm_nt(kb_l[i], kn_l[i]) for i in range(n)]
        qk = [_mm_nt(qn_l[i], kn_l[i]) for i in range(n)]
        for i, (rows, h, hs) in enumerate(where_l):
            aqk_s[rows, h * c:(h + 1) * c] = qk[i] * decay_l[i]
        p = [-jnp.where(strict, kk[i] * decay_l[i], 0.0) for i in range(n)]
        pd = [jnp.where(blockdiag, x, 0.0) for x in p]
        pn = [p[i] - pd[i] for i in range(n)]
        p2 = [_mm(x, x) for x in pd]
        p4 = [_mm(x, x) for x in p2]
        p8 = [_mm(x, x) for x in p4]
        a1 = [pd[i] + p2[i] + _mm(pd[i], p2[i]) for i in range(n)]
        a2 = [a1[i] + p4[i] + _mm(a1[i], p4[i]) for i in range(n)]
        a3 = [a2[i] + p8[i] + _mm(a2[i], p8[i]) for i in range(n)]
        nm = [pn[i] + _mm(a3[i], pn[i]) for i in range(n)]
        n2 = [_mm(x, x) for x in nm]
        bm = [nm[i] + n2[i] + _mm(nm[i], n2[i]) for i in range(n)]
        tm = [bm[i] + a3[i] + _mm(bm[i], a3[i]) for i in range(n)]
        for i, (rows, h, hs) in enumerate(where_l):
            sol = rhs_l[i] + _mm(tm[i], rhs_l[i])
            u_s[rows, hs] = sol[:, :GDN_D]
            w_s[rows, hs] = sol[:, GDN_D:]

    def scan_step(ic):
        rows = pl.ds(ic * c, c)
        egl = egl_s[pl.ds(ic * HALO, 1), :]
        st = [state_s[h] for h in range(GDN_HEADS)]
        stb = [x.astype(BF16) for x in st]
        ws = [_mm(w_s[rows, hs], stb[h]) for h, hs in enumerate(heads)]
        qs = [_mm(qn_s[rows, hs], stb[h]) for h, hs in enumerate(heads)]
        v_new = [u_s[rows, hs] - ws[h] for h, hs in enumerate(heads)]
        for h, hs in enumerate(heads):
            oacc_s[rows, hs] = qs[h] + _mm(aqk_s[rows, h * c:(h + 1) * c], v_new[h])
            state_s[h] = st[h] * egl[:, hs] + _mm_tn(kn_s[rows, hs], v_new[h])

    ngroups = ts // (GDN_GROUP * c)
    precompute(0)
    for ig in range(1, ngroups):
        precompute(ig)
        for ic in range((ig - 1) * GDN_GROUP, ig * GDN_GROUP):
            scan_step(ic)
    for ic in range((ngroups - 1) * GDN_GROUP, ngroups * GDN_GROUP):
        scan_step(ic)

    for h in range(GDN_HEADS):
        hs = slice(h * GDN_D, (h + 1) * GDN_D)
        oh = oacc_s[:, hs]
        ms = jnp.mean(oh * oh, axis=-1, keepdims=True)
        o_ref[:, hs] = (oh * lax.rsqrt(ms + EPS) * onw_ref[...] * _silu(gate_ref[:, hs])).astype(o_ref.dtype)


def _gdn(qkv, small, gate, a_log, dt_bias, out_norm_w, ts):
    b, s, _ = qkv.shape
    hw = GDN_W
    c = GDN_CHUNK
    rep = lambda v: jnp.repeat(v.astype(F32), GDN_D)[None, :]
    lane_head = np.arange(hw) // GDN_D
    ea = (np.arange(LANES)[:, None] == lane_head[None, :]).astype(np.float32)
    eb = (np.arange(LANES)[:, None] == (lane_head[None, :] + GDN_HEADS)).astype(np.float32)
    ltri = np.tril(np.ones((c, c), np.float32))
    full = lambda shape: pl.BlockSpec(shape, lambda ib, ij: (0,) * len(shape))
    return pl.pallas_call(
        _gdn_kernel,
        grid=(b, s // ts),
        in_specs=[
            pl.BlockSpec((None, ts, 3 * hw), lambda ib, ij: (ib, ij, 0)),
            pl.BlockSpec((None, ts, LANES), lambda ib, ij: (ib, ij, 0)),
            pl.BlockSpec((None, ts, hw), lambda ib, ij: (ib, ij, 0)),
            full((1, hw)), full((1, hw)), full((1, GDN_D)),
            full((LANES, hw)), full((LANES, hw)), full((c, c)),
        ],
        out_specs=pl.BlockSpec((None, ts, hw), lambda ib, ij: (ib, ij, 0)),
        out_shape=jax.ShapeDtypeStruct((b, s, hw), BF16),
        scratch_shapes=[
            pltpu.VMEM((ts, hw), F32), pltpu.VMEM((ts, hw), F32),
            pltpu.VMEM((ts, hw), F32), pltpu.VMEM((ts, hw), F32),
            pltpu.VMEM((ts, hw), F32), pltpu.VMEM((ts, hw), F32), pltpu.VMEM((ts, GDN_HEADS * c), F32),
            pltpu.VMEM((ts // c * HALO, hw), F32),
            pltpu.VMEM((ts, hw), F32),
            pltpu.VMEM((GDN_HEADS, GDN_D, GDN_D), F32),
        ],
        compiler_params=_cparams(("arbitrary", "arbitrary")),
        name="gdn",
    )(qkv, small, gate, rep(a_log), rep(dt_bias), out_norm_w.astype(F32)[None, :],
      jnp.asarray(ea), jnp.asarray(eb), jnp.asarray(ltri))


def _rope_tables(s):
    half = NSA_ROPE_DIM // 2
    pos = jnp.arange(s, dtype=F32)
    inv = 1.0 / (ROPE_THETA ** (jnp.arange(0, NSA_ROPE_DIM, 2, dtype=F32) / NSA_ROPE_DIM))
    ang = pos[:, None] * inv[None, :]
    cos, sin = jnp.cos(ang), jnp.sin(ang)
    one = jnp.ones((s, NSA_D - NSA_ROPE_DIM), F32)
    zero = jnp.zeros((s, NSA_D - NSA_ROPE_DIM), F32)
    zh = jnp.zeros((s, half), F32)
    tc = jnp.concatenate([cos, cos, one], axis=-1)
    ta = jnp.concatenate([-sin, zh, zero], axis=-1)
    tb = jnp.concatenate([zh, sin, zero], axis=-1)
    dup = lambda t: jnp.concatenate([t, t], axis=-1)
    return dup(tc), dup(ta), dup(tb)


def _rope(x, tc, ta, tb):
    half = NSA_ROPE_DIM // 2
    return x * tc + pltpu.roll(x, LANES - half, 1) * ta + pltpu.roll(x, half, 1) * tb


def _group_ms(x, ones_blk):
    return _dot_exact_rhs(x * x, ones_blk) * (1.0 / NSA_D)


def _dup_groups(x):
    r = pltpu.roll(x, NSA_D, 1)
    lane = lax.broadcasted_iota(jnp.int32, x.shape, 1)
    lo = lane < NSA_D
    return jnp.where(lo, x, r), jnp.where(lo, r, x)


def _vt_block(vt):
    n = vt.shape[1]
    tail = jnp.where(lax.broadcasted_iota(jnp.int32, (NSA_VROWS - NSA_D, n), 0) == 0, 1.0, 0.0)
    return jnp.concatenate([vt, tail], axis=0).astype(BF16)


def _nsa_prep_kernel(nq_ref, kv_ref, sm_ref, tc_ref, ta_ref, tb_ref, qw_ref, ksw_ref, kww_ref, ones_ref,
                     q_ref, ks_ref, vs_ref, kw_ref, vw_ref, gs_ref, *, gate_col0):
    j = pl.program_id(1)
    ts = nq_ref.shape[0]
    tc, ta, tb = tc_ref[...], ta_ref[...], tb_ref[...]
    ones_blk = ones_ref[...]
    scale = NSA_D ** -0.5 * LOG2E
    for p in range(NSA_W // LANES):
        cols = slice(p * LANES, (p + 1) * LANES)
        x = nq_ref[:, cols]
        xn = x * lax.rsqrt(_group_ms(x, ones_blk) + EPS) * qw_ref[...]
        q_ref[:, cols] = (_rope(xn, tc, ta, tb) * scale).astype(q_ref.dtype)
    for src, nw_ref, k_out, v_out in ((0, ksw_ref, ks_ref, vs_ref), (2, kww_ref, kw_ref, vw_ref)):
        k = kv_ref[:, src * LANES:(src + 1) * LANES]
        v = kv_ref[:, (src + 1) * LANES:(src + 2) * LANES]
        kn = k * lax.rsqrt(_group_ms(k, ones_blk) + EPS) * nw_ref[...]
        kr = _rope(kn, tc, ta, tb)
        vt = v.T
        for ig, kg in enumerate(_dup_groups(kr)):
            k_out[ig, :, 0:LANES] = kg.astype(k_out.dtype)
            v_out[ig] = _vt_block(vt[ig * NSA_D:(ig + 1) * NSA_D])
    pos = j * ts + lax.broadcasted_iota(jnp.int32, (ts, LANES), 0)
    lane = lax.broadcasted_iota(jnp.int32, (ts, LANES), 1)
    onehot = jnp.where(pos // NSA_SEL_BLOCK == lane, 1.0, 0.0).astype(ks_ref.dtype)
    sig = _sigmoid(sm_ref[...])
    for ig in range(NSA_GROUPS):
        ks_ref[ig, :, LANES:2 * LANES] = onehot
        kw_ref[ig, :, LANES:2 * LANES] = jnp.zeros((ts, LANES), kw_ref.dtype)
        gs_ref[ig] = pltpu.roll(sig, LANES - (gate_col0 + ig * NSA_REP * 3), 1)


def _nsa_prep(nq, kv4, small, tabs, q_norm_w, ks_norm_w, kw_norm_w, gate_col0, ts):
    b, s, _ = nq.shape
    g = NSA_GROUPS
    tile2 = lambda w: jnp.concatenate([w, w]).astype(F32)[None, :]
    ones_blk = np.kron(np.eye(2, dtype=np.float32), np.ones((NSA_D, NSA_D), np.float32))
    full = lambda shape: pl.BlockSpec(shape, lambda ib, ij: (0,) * len(shape))
    tok = lambda w: pl.BlockSpec((None, ts, w), lambda ib, ij: (ib, ij, 0))
    tab = pl.BlockSpec((ts, LANES), lambda ib, ij: (ij, 0))
    kv_out = lambda w: pl.BlockSpec((None, g, ts, w), lambda ib, ij: (ib, 0, ij, 0))
    kv_shape = lambda w, dt: jax.ShapeDtypeStruct((b, g, s, w), dt)
    vt_out = pl.BlockSpec((None, g, NSA_VROWS, ts), lambda ib, ij: (ib, 0, 0, ij))
    vt_shape = jax.ShapeDtypeStruct((b, g, NSA_VROWS, s), BF16)
    return pl.pallas_call(
        functools.partial(_nsa_prep_kernel, gate_col0=gate_col0),
        grid=(b, s // ts),
        in_specs=[tok(NSA_W), tok(4 * LANES), tok(LANES), tab, tab, tab,
                  full((1, LANES)), full((1, LANES)), full((1, LANES)), full((LANES, LANES))],
        out_specs=[tok(NSA_W), kv_out(2 * LANES), vt_out, kv_out(2 * LANES), vt_out, kv_out(LANES)],
        out_shape=[jax.ShapeDtypeStruct((b, s, NSA_W), BF16), kv_shape(2 * LANES, BF16), vt_shape,
                   kv_shape(2 * LANES, BF16), vt_shape, kv_shape(LANES, F32)],
        compiler_params=_cparams(("arbitrary", "arbitrary")),
        name="nsa_prep",
    )(nq, kv4, small, *tabs, tile2(q_norm_w), tile2(ks_norm_w), tile2(kw_norm_w), jnp.asarray(ones_blk))


def _nsa_compress_kernel(kc_ref, vc_ref, tc_ref, ta_ref, tb_ref, pk_ref, pv_ref, kw1_ref, kw2_ref,
                         vw1_ref, vw2_ref, nw_ref, ones_ref, kc_out, vc_out):
    st = NSA_CMP_STRIDE
    nrow = kc_ref.shape[0] // st
    outs = []
    for is_k in (True, False):
        src = kc_ref if is_k else vc_ref
        pos_ref, w1_ref, w2_ref = (pk_ref, kw1_ref, kw2_ref) if is_k else (pv_ref, vw1_ref, vw2_ref)
        first = second = None
        for l in range(st):
            rows = pl.ds(l, nrow, stride=st)
            x = src[rows, :]
            if is_k:
                x = _rope(x, tc_ref[rows, :], ta_ref[rows, :], tb_ref[rows, :])
            f = _mm(x + pos_ref[l:l + 1, :], w1_ref[l * LANES:(l + 1) * LANES, :])
            s2 = _mm(x + pos_ref[st + l:st + l + 1, :], w1_ref[(st + l) * LANES:(st + l + 1) * LANES, :])
            first = f if first is None else first + f
            second = s2 if second is None else second + s2
        y = first + pltpu.roll(second, nrow - 1, 0)
        if is_k:
            y = _mm(_silu(y), w2_ref[...])
            y = y * lax.rsqrt(_group_ms(y, ones_ref[...]) + EPS) * nw_ref[...]
        else:
            y = _mm_nt(w2_ref[...], _silu(y))
        outs.append(y)
    for ig, kg in enumerate(_dup_groups(outs[0])):
        kc_out[ig] = kg.astype(kc_out.dtype)
    for ig in range(NSA_GROUPS):
        vc_out[ig] = _vt_block(outs[1][ig * NSA_D:(ig + 1) * NSA_D])


def _nsa_compress(kc, vc, tabs, pos_k, pos_v, k_w1, k_w2, v_w1, v_w2, kc_norm_w):
    b, s, _ = kc.shape
    g = NSA_GROUPS
    st = NSA_CMP_STRIDE
    nrow = s // st
    width = st * LANES
    eye_g = jnp.eye(g, dtype=F32)

    def w1_blk(w1):
        wl = w1.reshape(NSA_CMP_BLOCK, NSA_D, NSA_D)
        return jnp.einsum("lde,gh->lgdhe", wl, eye_g).reshape(NSA_CMP_BLOCK * LANES, LANES).astype(BF16)

    def w2_blk(w2):
        return jnp.einsum("de,gh->gdhe", w2, eye_g).reshape(LANES, LANES).astype(BF16)

    def pos_rows(p):
        return jnp.concatenate([p, p], axis=-1).astype(F32)

    ones_blk = np.kron(np.eye(2, dtype=np.float32), np.ones((NSA_D, NSA_D), np.float32))
    full = lambda shape: pl.BlockSpec(shape, lambda ib: (0,) * len(shape))
    seq = pl.BlockSpec((None, s, LANES), lambda ib: (ib, 0, 0))
    out = pl.BlockSpec((None, g, nrow, LANES), lambda ib: (ib, 0, 0, 0))
    oshape = jax.ShapeDtypeStruct((b, g, nrow, LANES), BF16)
    return pl.pallas_call(
        _nsa_compress_kernel,
        grid=(b,),
        in_specs=[seq, seq, full((s, LANES)), full((s, LANES)), full((s, LANES)),
                  full((NSA_CMP_BLOCK, LANES)), full((NSA_CMP_BLOCK, LANES)),
                  full((2 * width, LANES)), full((LANES, LANES)), full((2 * width, LANES)), full((LANES, LANES)),
                  full((1, LANES)), full((LANES, LANES))],
        out_specs=[out, pl.BlockSpec((None, g, NSA_VROWS, nrow), lambda ib: (ib, 0, 0, 0))],
        out_shape=[oshape, jax.ShapeDtypeStruct((b, g, NSA_VROWS, nrow), BF16)],
        compiler_params=_cparams(("arbitrary",)),
        name="nsa_compress",
    )(kc, vc, *tabs, pos_rows(pos_k), pos_rows(pos_v), w1_blk(k_w1), w2_blk(k_w2),
      w1_blk(v_w1), w2_blk(v_w2).T, jnp.concatenate([kc_norm_w, kc_norm_w]).astype(F32)[None, :],
      jnp.asarray(ones_blk))


def _nsa_attn_kernel(q_ref, gs_ref, kc_ref, vct_ref, ks_ref, vst_ref, kw_ref, vwt_ref, ovt_ref, tri_ref, wb_ref,
                     o_ref, *, kb, wlen):
    jp = pl.program_id(1)
    tq = NSA_Q_BLOCK
    rep = NSA_REP
    rows = rep * tq
    pairw = 2 * tq
    halves = range(2)
    chains = [(g, h) for g in range(NSA_GROUPS) for h in halves]
    nc = len(chains)
    blk_i = [2 * jp + h for h in halves]
    ncmp = kc_ref.shape[1]
    nblk = ks_ref.shape[1] // NSA_SEL_BLOCK
    lsum = slice(NSA_D, NSA_D + 1)

    lane_q = lax.broadcasted_iota(jnp.int32, (tq, LANES), 1)
    qs = []
    for g, h in chains:
        pieces = []
        for r in range(rep):
            pair = g * (rep // 2) + r // 2
            tile = q_ref[h * tq:(h + 1) * tq, pair * LANES:(pair + 1) * LANES]
            keep = (lane_q < NSA_D) if r % 2 == 0 else (lane_q >= NSA_D)
            pieces.append(jnp.where(keep, tile, jnp.zeros_like(tile)))
        qs.append(jnp.concatenate(pieces, axis=0))

    sc_raw = [_mm_nt(kc_ref[g], qs[c]) for c, (g, h) in enumerate(chains)]
    base = pl.multiple_of(jp * pairw, pairw)
    sd = [_mm_nt(ks_ref[g, pl.ds(base, pairw), 0:LANES], qs[c]) + tri_ref[h]
          for c, (g, h) in enumerate(chains)]
    wq_bias = jnp.where(lax.broadcasted_iota(jnp.int32, (rows, LANES), 1) == NSA_D, NEG, 0.0).astype(BF16)
    wparts = [(o, min(NSA_SUB_KEYS, wlen - o)) for o in range(0, wlen, NSA_SUB_KEYS)]
    wrows = [pl.ds(base + o, n) for o, n in wparts]
    sw = [[_mm_nt(kw_ref[g, wrows[j], :], jnp.concatenate([qs[c], wq_bias], axis=1)) + wb_ref[h, o:o + n, :]
           for j, (o, n) in enumerate(wparts)] for c, (g, h) in enumerate(chains)]

    cend = lax.broadcasted_iota(jnp.int32, (ncmp, rows), 0) * NSA_CMP_STRIDE + (NSA_CMP_BLOCK - 1)
    tloc = lax.broadcasted_iota(jnp.int32, (ncmp, rows), 1) & (tq - 1)
    cmask = [cend - blk_i[h] * tq <= tloc for h in halves]
    sc = [jnp.where(cmask[h], sc_raw[c], NEG) for c, (g, h) in enumerate(chains)]
    mc = [jnp.max(sc[c], axis=0, keepdims=True) for c in range(nc)]
    pc = [jnp.where(cmask[h], jnp.exp2(sc[c] - mc[c]), 0.0) for c, (g, h) in enumerate(chains)]
    acc_c = [_mm(vct_ref[g], pc[c]) for c, (g, h) in enumerate(chains)]
    inv_c = [1.0 / jnp.maximum(acc_c[c][lsum, :], 1e-30) for c in range(nc)]

    m0 = [jnp.max(sd[c], axis=0, keepdims=True) for c in range(nc)]
    pd = [jnp.exp2(sd[c] - m0[c]) for c in range(nc)]
    a0 = [_mm(vst_ref[g, :, pl.ds(base, pairw)], pd[c]) for c, (g, h) in enumerate(chains)]

    acc_w = []
    for c, (g, h) in enumerate(chains):
        mw = jnp.max(sw[c][0], axis=0, keepdims=True)
        for j in range(1, len(wparts)):
            mw = jnp.maximum(mw, jnp.max(sw[c][j], axis=0, keepdims=True))
        a = _mm(vwt_ref[g, :, wrows[0]], jnp.exp2(sw[c][0] - mw))
        for j in range(1, len(wparts)):
            a = a + _mm(vwt_ref[g, :, wrows[j]], jnp.exp2(sw[c][j] - mw))
        acc_w.append(a)

    gates, o_cw = [], []
    for c, (g, h) in enumerate(chains):
        gt = gs_ref[g, h * tq:(h + 1) * tq, :].T
        gates.append([jnp.concatenate([gt[r * 3 + x:r * 3 + x + 1, :] for r in range(rep)], axis=1)
                      for x in range(3)])
        o_cw.append((gates[c][0] * inv_c[c]) * acc_c[c] + (gates[c][2] / acc_w[c][lsum, :]) * acc_w[c])

    lane_g = lax.broadcasted_iota(jnp.int32, (LANES, LANES), 1)
    imp_c = []
    for c in range(nc):
        pn = pc[c] * inv_c[c]
        folded = pn[:, 0:LANES] + pn[:, LANES:2 * LANES]
        psum = folded + pltpu.roll(folded, tq, 1)
        imp_c.append(_dot_exact_lhs(ovt_ref[...], psum))
    imp = [jnp.where(lane_g < tq, imp_c[h], imp_c[2 + h])[0:tq] for h in halves]
    blk = lax.broadcasted_iota(jnp.int32, (tq, LANES), 0)
    valid = [blk <= blk_i[h] for h in halves]

    def ranked():
        out = []
        for h in halves:
            forced = (blk == 0) | (blk == blk_i[h]) | (blk == blk_i[h] - 1)
            key = jnp.where(valid[h], jnp.where(forced, 0x7F000000, pltpu.bitcast(imp[h], jnp.int32)), -1)
            key = jnp.where(blk < nblk, key, -2)
            key_m1 = key - 1
            rank = jnp.zeros((tq, LANES), jnp.int32)
            for jb in range(nblk):
                ahead = key[jb:jb + 1, :] > jnp.where(blk > jb, key_m1, key)
                rank = rank + jnp.where(ahead, 1, 0)
            out.append(jnp.where(rank < NSA_N_SEL, 1.0, 0.0))
        return tuple(out)

    sel = lax.cond(blk_i[0] >= NSA_N_SEL, ranked, lambda: tuple(jnp.where(valid[h], 1.0, 0.0) for h in halves))
    q2 = [None] * nc
    for h in halves:
        selneg_t = jnp.where((sel[h] > 0.5) & (blk < 2 * jp), 0.0, NEG)
        selneg = jnp.concatenate([selneg_t, jnp.zeros((LANES - tq, LANES), F32)], axis=0).T
        for g in range(NSA_GROUPS):
            bias = selneg[g * tq:(g + 1) * tq].astype(BF16)
            q2[2 * g + h] = jnp.concatenate([qs[2 * g + h], jnp.concatenate([bias] * rep, axis=0)], axis=1)

    nsub = kb // NSA_SUB_KEYS

    def slc_body(ic, carry):
        m_i, acc = carry
        k0 = pl.multiple_of(ic * kb, kb)
        sub_rows = [pl.ds(k0 + j * NSA_SUB_KEYS, NSA_SUB_KEYS) for j in range(nsub)]
        m_out, acc_out = [None] * nc, [None] * nc

        def score(c):
            return [_mm_nt(ks_ref[chains[c][0], sub_rows[j], :], q2[c]) for j in range(nsub)]

        def update(c, s):
            m_chunk = s[0]
            for j in range(1, nsub):
                m_chunk = jnp.maximum(m_chunk, s[j])
            m_new = jnp.maximum(m_i[c], jnp.max(m_chunk, axis=0, keepdims=True))
            a = jnp.exp2(m_i[c] - m_new) * acc[c]
            for j in range(nsub):
                a = a + _mm(vst_ref[chains[c][0], :, sub_rows[j]], jnp.exp2(s[j] - m_new))
            m_out[c], acc_out[c] = m_new, a

        s_prev = score(0)
        for c in range(1, nc):
            s_next = score(c)
            update(c - 1, s_prev)
            s_prev = s_next
        update(nc - 1, s_prev)
        return tuple(m_out), tuple(acc_out)

    nch = (2 * jp * tq + kb - 1) // kb
    _, acc_s = lax.fori_loop(0, nch, slc_body, (tuple(m0), tuple(a0)))

    lo = lane_q < NSA_D
    for c, (g, h) in enumerate(chains):
        ot = (o_cw[c] + (gates[c][1] / acc_s[c][lsum, :]) * acc_s[c])
        ot = jnp.concatenate([ot, jnp.zeros((LANES - NSA_VROWS, rows), F32)], axis=0)
        heads = []
        for hp in range(rows // LANES):
            o_pair = ot[:, hp * LANES:(hp + 1) * LANES].T
            heads += [o_pair[0:tq], o_pair[tq:2 * tq]]
        for pr_ in range(rep // 2):
            pair = g * (rep // 2) + pr_
            o_ref[h * tq:(h + 1) * tq, pair * LANES:(pair + 1) * LANES] = jnp.where(
                lo, heads[2 * pr_], pltpu.roll(heads[2 * pr_ + 1], NSA_D, 1)).astype(o_ref.dtype)


def _nsa_attn(q, gsig, kcmp, vcmp_t, ks3, vs_t, kw3, vw_t):
    b, s, _ = q.shape
    g = NSA_GROUPS
    rep = NSA_REP
    tq = NSA_Q_BLOCK
    rows = rep * tq
    ncmp = kcmp.shape[2]
    nblk = s // NSA_SEL_BLOCK
    assert nblk <= tq and nblk <= NSA_D and (s // tq) % 2 == 0
    kb = min(1024, s)
    wpad = NSA_WINDOW
    wlen = NSA_WINDOW + 2 * tq
    ci = np.arange(ncmp) * NSA_CMP_STRIDE
    sj = np.arange(nblk) * NSA_SEL_BLOCK
    ov = np.clip(np.minimum(ci[None, :] + NSA_CMP_BLOCK, sj[:, None] + NSA_SEL_BLOCK)
                 - np.maximum(ci[None, :], sj[:, None]), 0, None).astype(np.float32) / NSA_CMP_STRIDE
    ov_t = np.zeros((LANES, ncmp), np.float32)
    ov_t[:nblk] = ov
    tloc = (np.arange(rows) % tq)[None, :]
    kcol = np.arange(2 * tq)[:, None]
    tri = np.stack([
        np.where(kcol <= tloc, 0.0, NEG),
        np.where(kcol < tq, 0.0, np.where(kcol - tq <= tloc, 0.0, NEG)),
    ]).astype(np.float32)
    wcol = np.arange(wlen)[:, None]
    wb = np.stack([
        np.where((wcol <= wpad + half * tq + tloc) & (wcol > half * tq + tloc), 0.0, NEG)
        for half in range(2)]).astype(np.float32)
    front = np.zeros((wpad, 2 * LANES), np.float32)
    front[:, LANES + NSA_D] = 1.0
    kw_p = jnp.concatenate([jnp.broadcast_to(jnp.asarray(front, dtype=BF16), (b, g, wpad, 2 * LANES)), kw3], axis=2)
    vw_p = jnp.pad(vw_t, ((0, 0), (0, 0), (0, 0), (wpad, 0)))
    sp = s + wpad
    seq = lambda n, w: pl.BlockSpec((None, g, n, w), lambda ib, ii: (ib, 0, 0, 0))
    full = lambda shape: pl.BlockSpec(shape, lambda ib, ii: (0,) * len(shape))
    return pl.pallas_call(
        functools.partial(_nsa_attn_kernel, kb=kb, wlen=wlen),
        grid=(b, s // (2 * tq)),
        in_specs=[
            pl.BlockSpec((None, 2 * tq, NSA_W), lambda ib, ii: (ib, ii, 0)),
            pl.BlockSpec((None, g, 2 * tq, LANES), lambda ib, ii: (ib, 0, ii, 0)),
            seq(ncmp, LANES), seq(NSA_VROWS, ncmp), seq(s, 2 * LANES), seq(NSA_VROWS, s),
            seq(sp, 2 * LANES), seq(NSA_VROWS, sp),
            full((LANES, ncmp)), full((2, 2 * tq, rows)), full((2, wlen, rows)),
        ],
        out_specs=pl.BlockSpec((None, 2 * tq, NSA_W), lambda ib, ii: (ib, ii, 0)),
        out_shape=jax.ShapeDtypeStruct((b, s, NSA_W), BF16),
        compiler_params=_cparams(("arbitrary", "arbitrary")),
        name="nsa_attn",
    )(q, gsig, kcmp, vcmp_t, ks3, vs_t, kw_p, vw_p, jnp.asarray(ov_t), jnp.asarray(tri), jnp.asarray(wb))


def _mem_kv_kernel(mem_ref, nw_ref, w_ref, knw_ref, k_ref, v_ref):
    x = mem_ref[...]
    ms = jnp.mean(x * x, axis=-1, keepdims=True)
    xn = x * lax.rsqrt(ms + EPS) * nw_ref[...]
    kv = _mm(xn, w_ref[...])
    v_ref[...] = kv[:, MEM_W:].astype(v_ref.dtype)
    for h in range(MEM_HEADS):
        hs = slice(h * MEM_D, (h + 1) * MEM_D)
        kh = kv[:, hs]
        msk = jnp.mean(kh * kh, axis=-1, keepdims=True)
        k_ref[:, hs] = (kh * lax.rsqrt(msk + EPS) * knw_ref[...]).astype(k_ref.dtype)


def _mem_kv(mem, mem_norm_w, w_kv, k_norm_w):
    b, m, d = mem.shape
    full = lambda shape: pl.BlockSpec(shape, lambda ib: (0,) * len(shape))
    blk = lambda w: pl.BlockSpec((None, m, w), lambda ib: (ib, 0, 0))
    return pl.pallas_call(
        _mem_kv_kernel,
        grid=(b,),
        in_specs=[blk(d), full((1, d)), full((d, 2 * MEM_W)), full((1, MEM_D))],
        out_specs=[blk(MEM_W), blk(MEM_W)],
        out_shape=[jax.ShapeDtypeStruct((b, m, MEM_W), BF16)] * 2,
        compiler_params=_cparams(("arbitrary",)),
        name="mem_kv",
    )(mem, mem_norm_w.astype(F32)[None, :], w_kv.astype(BF16), k_norm_w.astype(F32)[None, :])


def _mem_attn_kernel(q_ref, k_ref, v_ref, qnw_ref, o_ref):
    scale = MEM_D ** -0.5
    for h in range(MEM_HEADS):
        hs = slice(h * MEM_D, (h + 1) * MEM_D)
        qh = q_ref[:, hs]
        ms = jnp.mean(qh * qh, axis=-1, keepdims=True)
        qn = qh * lax.rsqrt(ms + EPS) * qnw_ref[...]
        s = _mm_nt(qn, k_ref[:, hs]) * scale
        m = jnp.max(s, axis=-1, keepdims=True)
        p = jnp.exp(s - m)
        l = jnp.sum(p, axis=-1, keepdims=True)
        o_ref[:, hs] = (_mm(p, v_ref[:, hs]) * (1.0 / l)).astype(o_ref.dtype)


def _mem_attn(mq, k, v, q_norm_w, ts):
    b, s, _ = mq.shape
    m = k.shape[1]
    return pl.pallas_call(
        _mem_attn_kernel,
        grid=(b, s // ts),
        in_specs=[
            pl.BlockSpec((None, ts, MEM_W), lambda ib, ij: (ib, ij, 0)),
            pl.BlockSpec((None, m, MEM_W), lambda ib, ij: (ib, 0, 0)),
            pl.BlockSpec((None, m, MEM_W), lambda ib, ij: (ib, 0, 0)),
            pl.BlockSpec((1, MEM_D), lambda ib, ij: (0, 0)),
        ],
        out_specs=pl.BlockSpec((None, ts, MEM_W), lambda ib, ij: (ib, ij, 0)),
        out_shape=jax.ShapeDtypeStruct((b, s, MEM_W), BF16),
        compiler_params=_cparams(("arbitrary", "arbitrary")),
        name="mem_attn",
    )(mq, k, v, q_norm_w.astype(F32)[None, :])


def _out_proj_kernel(x_hbm, oa_ref, ob_ref, oc_ref, w_ref, h_ref, xbuf, xsem):
    i = pl.program_id(0)
    n = pl.num_programs(0)
    tm = h_ref.shape[0]

    def x_copy(k):
        start = k * tm if isinstance(k, int) else pl.multiple_of(k * tm, tm)
        slot = k % OUT_X_SLOTS
        return pltpu.make_async_copy(x_hbm.at[pl.ds(start, tm), :], xbuf.at[slot], xsem.at[slot])

    @pl.when(i == 0)
    def _():
        for k in range(OUT_X_SLOTS - 1):
            x_copy(k).start()

    @pl.when(i + (OUT_X_SLOTS - 1) < n)
    def _():
        x_copy(i + (OUT_X_SLOTS - 1)).start()

    x_copy(i).wait()
    acc = xbuf[i % OUT_X_SLOTS]
    off = 0
    for o_ref in (oa_ref, ob_ref, oc_ref):
        wd = o_ref.shape[-1]
        acc = acc + jnp.dot(o_ref[...].astype(BF16), w_ref[off:off + wd, :], preferred_element_type=F32)
        off += wd
    h_ref[...] = acc


def _out_proj(x2, oa, ob, oc, w_out, tm):
    t, d = x2.shape
    assert t // tm >= OUT_X_SLOTS - 1
    row = lambda w: pl.BlockSpec((tm, w), lambda i: (i, 0))
    return pl.pallas_call(
        _out_proj_kernel,
        grid=(t // tm,),
        in_specs=[pl.BlockSpec(memory_space=pl.ANY), row(oa.shape[1]), row(ob.shape[1]), row(oc.shape[1]),
                  pl.BlockSpec(w_out.shape, lambda i: (0, 0))],
        out_specs=row(d),
        out_shape=jax.ShapeDtypeStruct((t, d), F32),
        scratch_shapes=[pltpu.VMEM((OUT_X_SLOTS, tm, d), F32), pltpu.SemaphoreType.DMA((OUT_X_SLOTS,))],
        compiler_params=_cparams(("arbitrary",)),
        name="out_proj",
    )(x2, oa, ob, oc, w_out)


def _ffn_kernel(h_ref, halo_ref, nw_ref, wup_ref, cw_ref, wdn_ref, o_ref, hn_s, u_s, act_s, *, fc):
    j = pl.program_id(1)
    ts = h_ref.shape[0]
    f = wdn_ref.shape[0]
    hl = halo_ref.shape[0]

    def norm(x):
        ms = jnp.mean(x * x, axis=-1, keepdims=True)
        return (x * lax.rsqrt(ms + EPS) * nw_ref[...]).astype(BF16)

    halo = jnp.where(j > 0, halo_ref[...], 0.0)
    hn_s[0:hl, :] = norm(halo)
    hn_s[hl:hl + ts, :] = norm(h_ref[...])
    for ic in range(f // fc):
        hn = hn_s[...]
        slot = ic % 2
        for part in range(2):
            cols = slice(part * f + ic * fc, part * f + (ic + 1) * fc)
            u_s[slot, part] = jnp.dot(hn, wup_ref[:, cols], preferred_element_type=F32)
        conv = []
        for part in range(2):
            cols = slice(part * f + ic * fc, part * f + (ic + 1) * fc)
            acc = cw_ref[FFN_CONV - 1:FFN_CONV, cols] * u_s[slot, part, hl:hl + ts, :]
            for jj in range(FFN_CONV - 1):
                acc = acc + cw_ref[jj:jj + 1, cols] * u_s[slot, part, pl.ds(hl - (FFN_CONV - 1) + jj, ts), :]
            conv.append(acc)
        act_s[:, ic * fc:(ic + 1) * fc] = (_silu(conv[0]) * conv[1]).astype(BF16)
    o_ref[...] = h_ref[...] + jnp.dot(act_s[...], wdn_ref[...], preferred_element_type=F32)


def _ffn(h, norm_w, w_up, conv_w, w_down, ts, fc):
    b, s, d = h.shape
    f = w_down.shape[0]
    full = lambda shape: pl.BlockSpec(shape, lambda ib, ij: (0,) * len(shape))
    return pl.pallas_call(
        functools.partial(_ffn_kernel, fc=fc),
        grid=(b, s // ts),
        in_specs=[
            pl.BlockSpec((None, ts, d), lambda ib, ij: (ib, ij, 0)),
            pl.BlockSpec((None, HALO_BF16, d), lambda ib, ij: (ib, jnp.maximum(ij * (ts // HALO_BF16) - 1, 0), 0)),
            full((1, d)), full((d, 2 * f)), full((FFN_CONV, 2 * f)), full((f, d)),
        ],
        out_specs=pl.BlockSpec((None, ts, d), lambda ib, ij: (ib, ij, 0)),
        out_shape=jax.ShapeDtypeStruct((b, s, d), F32),
        scratch_shapes=[
            pltpu.VMEM((ts + HALO_BF16, d), BF16),
            pltpu.VMEM((2, 2, ts + HALO_BF16, fc), F32),
            pltpu.VMEM((ts, f), BF16),
        ],
        compiler_params=_cparams(("arbitrary", "arbitrary")),
        name="ffn",
    )(h, h, norm_w.astype(F32)[None, :], w_up, conv_w.astype(F32), w_down)


def _split_w_in(w_in):
    sizes = (3 * GDN_W, GDN_HEADS, GDN_HEADS, GDN_W, NSA_W, NSA_KV_W, NSA_KV_W, NSA_KV_W, NSA_KV_W,
             NSA_KV_W, NSA_KV_W, 3 * NSA_HEADS, MEM_W)
    offs = np.concatenate([[0], np.cumsum(sizes)])
    (qkv, a, bb, gate, nq, kc, vc, ks, vs, kw, vw, ng, mq) = [w_in[:, offs[i]:offs[i + 1]] for i in range(len(sizes))]
    n_small = 2 * GDN_HEADS + 3 * NSA_HEADS
    small = jnp.concatenate([a, bb, ng, jnp.zeros((w_in.shape[0], LANES - n_small), w_in.dtype)], axis=1)
    widths = (3 * GDN_W, GDN_W, NSA_W, NSA_KV_W, NSA_KV_W, 4 * NSA_KV_W, MEM_W, LANES)
    w_cat = jnp.concatenate([qkv, gate, nq, kc, vc, ks, vs, kw, vw, mq, small], axis=1).astype(BF16)
    return w_cat, widths


def _layer(x, mem, attn_norm_w, mem_norm_w, w_in, gdn_conv_w, gdn_a_log, gdn_dt_bias, gdn_out_norm_w,
           nsa_q_norm_w, nsa_kc_norm_w, nsa_ks_norm_w, nsa_kw_norm_w, nsa_cmp_pos_k, nsa_cmp_pos_v,
           nsa_cmp_k_w1, nsa_cmp_k_w2, nsa_cmp_v_w1, nsa_cmp_v_w2, mem_w_kv, mem_q_norm_w, mem_k_norm_w,
           w_out, ffn_norm_w, ffn_w_up, ffn_conv_w, ffn_w_down):
    b, s, d = x.shape
    t = b * s
    ts = min(512, s)
    x2 = x.reshape(t, d)

    w_cat, widths = _split_w_in(w_in)
    qkv, gate, nq, kc, vc, kv4, mq, small = _in_proj(x2, attn_norm_w.astype(F32)[None, :], w_cat, gdn_conv_w,
                                                     widths, ts, s // ts)
    r3 = lambda a: a.reshape(b, s, a.shape[-1])

    o_a = _gdn(r3(qkv), r3(small), r3(gate), gdn_a_log, gdn_dt_bias, gdn_out_norm_w, ts)

    tabs = _rope_tables(s)
    q_r, ks3, vs2, kw2, vw2, gsig = _nsa_prep(r3(nq), r3(kv4), r3(small), tabs, nsa_q_norm_w, nsa_ks_norm_w,
                                              nsa_kw_norm_w, 2 * GDN_HEADS, ts)
    kcmp, vcmp = _nsa_compress(r3(kc), r3(vc), tabs, nsa_cmp_pos_k, nsa_cmp_pos_v, nsa_cmp_k_w1, nsa_cmp_k_w2,
                               nsa_cmp_v_w1, nsa_cmp_v_w2, nsa_kc_norm_w)
    o_b = _nsa_attn(q_r, gsig, kcmp, vcmp, ks3, vs2, kw2, vw2)

    mk, mv = _mem_kv(mem, mem_norm_w, mem_w_kv, mem_k_norm_w)
    o_c = _mem_attn(r3(mq), mk, mv, mem_q_norm_w, ts)

    h = _out_proj(x2, o_a.reshape(t, GDN_W), o_b.reshape(t, NSA_W), o_c.reshape(t, MEM_W), w_out.astype(BF16), ts)
    out = _ffn(h.reshape(b, s, d), ffn_norm_w, ffn_w_up.astype(BF16), ffn_conv_w, ffn_w_down.astype(BF16), ts, 256)
    return out


def kernel(x, mem, attn_norm_w, mem_norm_w, w_in, gdn_conv_w, gdn_a_log, gdn_dt_bias, gdn_out_norm_w, nsa_q_norm_w, nsa_kc_norm_w, nsa_ks_norm_w, nsa_kw_norm_w, nsa_cmp_pos_k, nsa_cmp_pos_v, nsa_cmp_k_w1, nsa_cmp_k_w2, nsa_cmp_v_w1, nsa_cmp_v_w2, mem_w_kv, mem_q_norm_w, mem_k_norm_w, w_out, ffn_norm_w, ffn_w_up, ffn_conv_w, ffn_w_down):
    h = x
    for l in range(w_in.shape[0]):
        h = _layer(h, mem, attn_norm_w[l], mem_norm_w[l], w_in[l], gdn_conv_w[l], gdn_a_log[l], gdn_dt_bias[l],
                   gdn_out_norm_w[l], nsa_q_norm_w[l], nsa_kc_norm_w[l], nsa_ks_norm_w[l], nsa_kw_norm_w[l],
                   nsa_cmp_pos_k[l], nsa_cmp_pos_v[l], nsa_cmp_k_w1[l], nsa_cmp_k_w2[l], nsa_cmp_v_w1[l],
                   nsa_cmp_v_w2[l], mem_w_kv[l], mem_q_norm_w[l], mem_k_norm_w[l], w_out[l], ffn_norm_w[l],
                   ffn_w_up[l], ffn_conv_w[l], ffn_w_down[l])
    return h
```

```python
import functools

import jax
import jax.numpy as jnp
import numpy as np
from jax import lax
from jax.experimental import pallas as pl
from jax.experimental.pallas import tpu as pltpu

F32 = jnp.float32
BF16 = jnp.bfloat16

EPS = 1e-6
ROPE_THETA = 500000.0
GDN_HEADS = 4
GDN_D = 128
GDN_CONV = 4
GDN_CHUNK = 64
GDN_SUB = 16
GDN_GROUP = 4
NSA_HEADS = 8
NSA_GROUPS = 2
NSA_REP = NSA_HEADS // NSA_GROUPS
NSA_D = 64
NSA_CMP_BLOCK = 32
NSA_CMP_STRIDE = 16
NSA_SEL_BLOCK = 64
NSA_N_SEL = 16
NSA_WINDOW = 512
NSA_Q_BLOCK = 64
NSA_ROPE_DIM = NSA_D // 4
NSA_SUB_KEYS = 256
NSA_VROWS = NSA_D + 16
MEM_HEADS = 4
MEM_D = 128
FFN_CONV = 3

GDN_W = GDN_HEADS * GDN_D
NSA_W = NSA_HEADS * NSA_D
MEM_W = MEM_HEADS * MEM_D
NSA_KV_W = NSA_GROUPS * NSA_D

LANES = 128
HALO = 8
HALO_BF16 = 16
OUT_X_SLOTS = 3
VMEM_LIMIT = 56 * 1024 * 1024
NEG = -1e30
LOG2E = 1.4426950408889634


def _cparams(sem):
    return pltpu.CompilerParams(dimension_semantics=sem, vmem_limit_bytes=VMEM_LIMIT)


def _mm(a, b):
    return jnp.dot(a.astype(BF16), b.astype(BF16), preferred_element_type=F32)


def _mm_nt(a, b):
    return lax.dot_general(a.astype(BF16), b.astype(BF16), (((1,), (1,)), ((), ())),
                           preferred_element_type=F32)


def _mm_tn(a, b):
    return lax.dot_general(a.astype(BF16), b.astype(BF16), (((0,), (0,)), ((), ())),
                           preferred_element_type=F32)


def _split3(x):
    hi = x.astype(BF16)
    r = x - hi.astype(F32)
    mid = r.astype(BF16)
    lo = (r - mid.astype(F32)).astype(BF16)
    return hi, mid, lo


def _dot_exact_rhs(x, e):
    hi, mid, lo = _split3(x)
    eb = e.astype(BF16)
    return (jnp.dot(hi, eb, preferred_element_type=F32) + jnp.dot(mid, eb, preferred_element_type=F32)
            + jnp.dot(lo, eb, preferred_element_type=F32))


def _dot_exact_lhs(e, x):
    hi, mid, lo = _split3(x)
    eb = e.astype(BF16)
    return (jnp.dot(eb, hi, preferred_element_type=F32) + jnp.dot(eb, mid, preferred_element_type=F32)
            + jnp.dot(eb, lo, preferred_element_type=F32))


def _sigmoid(x):
    return 1.0 / (1.0 + jnp.exp2(x * (-LOG2E)))


def _silu(x):
    return x * _sigmoid(x)


def _softplus(x):
    return jnp.maximum(x, 0.0) + jnp.log(1.0 + jnp.exp(-jnp.abs(x)))


def _in_proj_kernel(x_ref, nw_ref, w_ref, cw_ref, *refs, tiles_per_seq):
    o_refs, xb = refs[:-1], refs[-1]
    i = pl.program_id(0)
    tm = x_ref.shape[0]
    hw = GDN_W
    x = x_ref[...]
    ms = jnp.mean(x * x, axis=-1, keepdims=True)
    xn = (x * lax.rsqrt(ms + EPS) * nw_ref[...]).astype(BF16)

    @pl.when(i % tiles_per_seq == 0)
    def _():
        xb[0:HALO, :] = jnp.zeros((HALO, 3 * hw), F32)

    qkv_ref = o_refs[0]
    xb[HALO:HALO + tm, :] = jnp.dot(xn, w_ref[:, 0:3 * hw], preferred_element_type=F32)
    off = 3 * hw
    rest = list(o_refs[1:])
    while rest:
        n = 2 if len(rest) > 1 and rest[0].shape[-1] == LANES and rest[1].shape[-1] == LANES else 1
        group, rest = rest[:n], rest[n:]
        wd = sum(r.shape[-1] for r in group)
        res = jnp.dot(xn, w_ref[:, off:off + wd], preferred_element_type=F32)
        col = 0
        for r in group:
            r[...] = res[:, col:col + r.shape[-1]].astype(r.dtype)
            col += r.shape[-1]
        off += wd
    for part in range(3):
        cols = slice(part * hw, (part + 1) * hw)
        acc = cw_ref[GDN_CONV - 1:GDN_CONV, cols] * xb[HALO:HALO + tm, cols]
        for jj in range(GDN_CONV - 1):
            acc = acc + cw_ref[jj:jj + 1, cols] * xb[pl.ds(HALO - (GDN_CONV - 1) + jj, tm), cols]
        act = _silu(acc)
        if part == 2:
            qkv_ref[:, cols] = act
        else:
            scale = GDN_D ** -0.5 if part == 0 else 1.0
            for h in range(GDN_HEADS):
                hs = slice(part * hw + h * GDN_D, part * hw + (h + 1) * GDN_D)
                xh = act[:, h * GDN_D:(h + 1) * GDN_D]
                ss = jnp.sum(xh * xh, axis=-1, keepdims=True)
                qkv_ref[:, hs] = xh * (lax.rsqrt(ss + EPS) * scale)
    xb[0:HALO, :] = xb[tm:tm + HALO, :]


def _in_proj(x2, norm_w, w_cat, conv_w, widths, tm, tiles_per_seq):
    t, d = x2.shape
    wtot = w_cat.shape[1]
    return pl.pallas_call(
        functools.partial(_in_proj_kernel, tiles_per_seq=tiles_per_seq),
        grid=(t // tm,),
        in_specs=[
            pl.BlockSpec((tm, d), lambda i: (i, 0)),
            pl.BlockSpec((1, d), lambda i: (0, 0)),
            pl.BlockSpec((d, wtot), lambda i: (0, 0)),
            pl.BlockSpec(conv_w.shape, lambda i: (0, 0)),
        ],
        out_specs=[pl.BlockSpec((tm, wd), lambda i: (i, 0)) for wd in widths],
        out_shape=[jax.ShapeDtypeStruct((t, wd), F32) for wd in widths],
        scratch_shapes=[pltpu.VMEM((tm + HALO, widths[0]), F32)],
        compiler_params=_cparams(("arbitrary",)),
        name="in_proj",
    )(x2, norm_w, w_cat, conv_w.astype(F32))


def _gdn_kernel(qkv_ref, sm_ref, gate_ref, alog_ref, dtb_ref, onw_ref, ea_ref, eb_ref, ltri_ref,
                o_ref, qn_s, kn_s, g_s, beta_s, u_s, w_s, aqk_s, egl_s, oacc_s, state_s):
    j = pl.program_id(1)
    ts = o_ref.shape[0]
    c = GDN_CHUNK
    hw = GDN_W

    @pl.when(j == 0)
    def _():
        state_s[...] = jnp.zeros_like(state_s)

    sm = sm_ref[...]
    g_s[...] = _dot_exact_rhs(-jnp.exp(alog_ref[...]) * _softplus(sm + dtb_ref[...]), ea_ref[...])
    beta_s[...] = _dot_exact_rhs(_sigmoid(sm), eb_ref[...])

    ri = lax.broadcasted_iota(jnp.int32, (c, c), 0)
    ci = lax.broadcasted_iota(jnp.int32, (c, c), 1)
    causal = ri >= ci
    strict = ri > ci
    blockdiag = (ri // GDN_SUB) == (ci // GDN_SUB)
    ltri = ltri_ref[...]
    heads = [slice(h * GDN_D, (h + 1) * GDN_D) for h in range(GDN_HEADS)]

    def precompute(ig):
        base = ig * (GDN_GROUP * c)
        kb_l, kn_l, qn_l, rhs_l, decay_l, where_l = [], [], [], [], [], []
        for cc in range(GDN_GROUP):
            rows = pl.ds(base + cc * c, c)
            gc = _dot_exact_lhs(ltri, g_s[rows, :])
            glast = gc[c - 1:c, :]
            eg = jnp.exp(gc)
            beta = beta_s[rows, :]
            qn = qkv_ref[rows, 0:hw]
            kn = qkv_ref[rows, hw:2 * hw]
            kb = kn * beta
            vb = qkv_ref[rows, 2 * hw:3 * hw] * beta
            kbe = kb * eg
            qn_s[rows, :] = qn * eg
            kn_s[rows, :] = kn * jnp.exp(glast - gc)
            egl_s[pl.ds((ig * GDN_GROUP + cc) * HALO, HALO), :] = jnp.broadcast_to(jnp.exp(glast), (HALO, hw))
            for h, hs in enumerate(heads):
                gcol = gc[:, h * GDN_D:h * GDN_D + c]
                grow = gc[:, hs].T[0:1, 0:c]
                diff = gcol - grow
                decay_l.append(jnp.where(causal, jnp.exp(jnp.where(causal, diff, 0.0)), 0.0))
                kb_l.append(kb[:, hs].astype(BF16))
                kn_l.append(kn[:, hs].astype(BF16))
                qn_l.append(qn[:, hs].astype(BF16))
                rhs_l.append(jnp.concatenate([vb[:, hs], kbe[:, hs]], axis=-1))
                where_l.append((rows, h, hs))
        n = len(where_l)
        kk = [_mm_nt(kb_l[i], kn_l[i]) for i in range(n)]
        qk = [_mm_nt(qn_l[i], kn_l[i]) for i in range(n)]
        for i, (rows, h, hs) in enumerate(where_l):
            aqk_s[rows, h * c:(h + 1) * c] = qk[i] * decay_l[i]
        p = [-jnp.where(strict, kk[i] * decay_l[i], 0.0) for i in range(n)]
        pd = [jnp.where(blockdiag, x, 0.0) for x in p]
        pn = [p[i] - pd[i] for i in range(n)]
        p2 = [_mm(x, x) for x in pd]
        p4 = [_mm(x, x) for x in p2]
        p8 = [_mm(x, x) for x in p4]
        a1 = [pd[i] + p2[i] + _mm(pd[i], p2[i]) for i in range(n)]
        a2 = [a1[i] + p4[i] + _mm(a1[i], p4[i]) for i in range(n)]
        a3 = [a2[i] + p8[i] + _mm(a2[i], p8[i]) for i in range(n)]
        nm = [pn[i] + _mm(a3[i], pn[i]) for i in range(n)]
        n2 = [_mm(x, x) for x in nm]
        bm = [nm[i] + n2[i] + _mm(nm[i], n2[i]) for i in range(n)]
        tm = [bm[i] + a3[i] + _mm(bm[i], a3[i]) for i in range(n)]
        for i, (rows, h, hs) in enumerate(where_l):
            sol = rhs_l[i] + _mm(tm[i], rhs_l[i])
            u_s[rows, hs] = sol[:, :GDN_D]
            w_s[rows, hs] = sol[:, GDN_D:]

    def scan_step(ic):
        rows = pl.ds(ic * c, c)
        egl = egl_s[pl.ds(ic * HALO, 1), :]
        st = [state_s[h] for h in range(GDN_HEADS)]
        stb = [x.astype(BF16) for x in st]
        ws = [_mm(w_s[rows, hs], stb[h]) for h, hs in enumerate(heads)]
        qs = [_mm(qn_s[rows, hs], stb[h]) for h, hs in enumerate(heads)]
        v_new = [u_s[rows, hs] - ws[h] for h, hs in enumerate(heads)]
        for h, hs in enumerate(heads):
            oacc_s[rows, hs] = qs[h] + _mm(aqk_s[rows, h * c:(h + 1) * c], v_new[h])
            state_s[h] = st[h] * egl[:, hs] + _mm_tn(kn_s[rows, hs], v_new[h])

    ngroups = ts // (GDN_GROUP * c)
    precompute(0)
    for ig in range(1, ngroups):
        precompute(ig)
        for ic in range((ig - 1) * GDN_GROUP, ig * GDN_GROUP):
            scan_step(ic)
    for ic in range((ngroups - 1) * GDN_GROUP, ngroups * GDN_GROUP):
        scan_step(ic)

    for h in range(GDN_HEADS):
        hs = slice(h * GDN_D, (h + 1) * GDN_D)
        oh = oacc_s[:, hs]
        ms = jnp.mean(oh * oh, axis=-1, keepdims=True)
        o_ref[:, hs] = (oh * lax.rsqrt(ms + EPS) * onw_ref[...] * _silu(gate_ref[:, hs])).astype(o_ref.dtype)


def _gdn(qkv, small, gate, a_log, dt_bias, out_norm_w, ts):
    b, s, _ = qkv.shape
    hw = GDN_W
    c = GDN_CHUNK
    rep = lambda v: jnp.zeros((1, LANES), F32).at[0, :GDN_HEADS].set(v.astype(F32))
    lane_head = np.arange(hw) // GDN_D
    ea = (np.arange(LANES)[:, None] == lane_head[None, :]).astype(np.float32)
    eb = (np.arange(LANES)[:, None] == (lane_head[None, :] + GDN_HEADS)).astype(np.float32)
    ltri = np.tril(np.ones((c, c), np.float32))
    full = lambda shape: pl.BlockSpec(shape, lambda ib, ij: (0,) * len(shape))
    return pl.pallas_call(
        _gdn_kernel,
        grid=(b, s // ts),
        in_specs=[
            pl.BlockSpec((None, ts, 3 * hw), lambda ib, ij: (ib, ij, 0)),
            pl.BlockSpec((None, ts, LANES), lambda ib, ij: (ib, ij, 0)),
            pl.BlockSpec((None, ts, hw), lambda ib, ij: (ib, ij, 0)),
            full((1, LANES)), full((1, LANES)), full((1, GDN_D)),
            full((LANES, hw)), full((LANES, hw)), full((c, c)),
        ],
        out_specs=pl.BlockSpec((None, ts, hw), lambda ib, ij: (ib, ij, 0)),
        out_shape=jax.ShapeDtypeStruct((b, s, hw), BF16),
        scratch_shapes=[
            pltpu.VMEM((ts, hw), F32), pltpu.VMEM((ts, hw), F32),
            pltpu.VMEM((ts, hw), F32), pltpu.VMEM((ts, hw), F32),
            pltpu.VMEM((ts, hw), F32), pltpu.VMEM((ts, hw), F32), pltpu.VMEM((ts, GDN_HEADS * c), F32),
            pltpu.VMEM((ts // c * HALO, hw), F32),
            pltpu.VMEM((ts, hw), F32),
            pltpu.VMEM((GDN_HEADS, GDN_D, GDN_D), F32),
        ],
        compiler_params=_cparams(("arbitrary", "arbitrary")),
        name="gdn",
    )(qkv, small, gate, rep(a_log), rep(dt_bias), out_norm_w.astype(F32)[None, :],
      jnp.asarray(ea), jnp.asarray(eb), jnp.asarray(ltri))


def _rope_tables(s):
    half = NSA_ROPE_DIM // 2
    pos = jnp.arange(s, dtype=F32)
    inv = 1.0 / (ROPE_THETA ** (jnp.arange(0, NSA_ROPE_DIM, 2, dtype=F32) / NSA_ROPE_DIM))
    ang = pos[:, None] * inv[None, :]
    cos, sin = jnp.cos(ang), jnp.sin(ang)
    one = jnp.ones((s, NSA_D - NSA_ROPE_DIM), F32)
    zero = jnp.zeros((s, NSA_D - NSA_ROPE_DIM), F32)
    zh = jnp.zeros((s, half), F32)
    tc = jnp.concatenate([cos, cos, one], axis=-1)
    ta = jnp.concatenate([-sin, zh, zero], axis=-1)
    tb = jnp.concatenate([zh, sin, zero], axis=-1)
    dup = lambda t: jnp.concatenate([t, t], axis=-1)
    return dup(tc), dup(ta), dup(tb)


def _rope(x, tc, ta, tb):
    half = NSA_ROPE_DIM // 2
    return x * tc + pltpu.roll(x, LANES - half, 1) * ta + pltpu.roll(x, half, 1) * tb


def _group_ms(x, ones_blk):
    return _dot_exact_rhs(x * x, ones_blk) * (1.0 / NSA_D)


def _dup_groups(x):
    r = pltpu.roll(x, NSA_D, 1)
    lane = lax.broadcasted_iota(jnp.int32, x.shape, 1)
    lo = lane < NSA_D
    return jnp.where(lo, x, r), jnp.where(lo, r, x)


def _vt_block(vt):
    n = vt.shape[1]
    tail = jnp.where(lax.broadcasted_iota(jnp.int32, (NSA_VROWS - NSA_D, n), 0) == 0, 1.0, 0.0)
    return jnp.concatenate([vt, tail], axis=0).astype(BF16)


def _nsa_prep_kernel(nq_ref, kv_ref, sm_ref, tc_ref, ta_ref, tb_ref, qw_ref, ksw_ref, kww_ref, ones_ref,
                     q_ref, ks_ref, vs_ref, kw_ref, vw_ref, gs_ref, *, gate_col0):
    j = pl.program_id(1)
    ts = nq_ref.shape[0]
    tc, ta, tb = tc_ref[...], ta_ref[...], tb_ref[...]
    ones_blk = ones_ref[...]
    scale = NSA_D ** -0.5 * LOG2E
    for p in range(NSA_W // LANES):
        cols = slice(p * LANES, (p + 1) * LANES)
        x = nq_ref[:, cols]
        xn = x * lax.rsqrt(_group_ms(x, ones_blk) + EPS) * qw_ref[...]
        q_ref[:, cols] = (_rope(xn, tc, ta, tb) * scale).astype(q_ref.dtype)
    for src, nw_ref, k_out, v_out in ((0, ksw_ref, ks_ref, vs_ref), (2, kww_ref, kw_ref, vw_ref)):
        k = kv_ref[:, src * LANES:(src + 1) * LANES]
        v = kv_ref[:, (src + 1) * LANES:(src + 2) * LANES]
        kn = k * lax.rsqrt(_group_ms(k, ones_blk) + EPS) * nw_ref[...]
        kr = _rope(kn, tc, ta, tb)
        vt = v.T
        for ig, kg in enumerate(_dup_groups(kr)):
            k_out[ig, :, 0:LANES] = kg.astype(k_out.dtype)
            v_out[ig] = _vt_block(vt[ig * NSA_D:(ig + 1) * NSA_D])
    pos = j * ts + lax.broadcasted_iota(jnp.int32, (ts, LANES), 0)
    lane = lax.broadcasted_iota(jnp.int32, (ts, LANES), 1)
    onehot = jnp.where(pos // NSA_SEL_BLOCK == lane, 1.0, 0.0).astype(ks_ref.dtype)
    sig = _sigmoid(sm_ref[...])
    for ig in range(NSA_GROUPS):
        ks_ref[ig, :, LANES:2 * LANES] = onehot
        kw_ref[ig, :, LANES:2 * LANES] = jnp.zeros((ts, LANES), kw_ref.dtype)
        gs_ref[ig] = pltpu.roll(sig, LANES - (gate_col0 + ig * NSA_REP * 3), 1)


def _nsa_prep(nq, kv4, small, tabs, q_norm_w, ks_norm_w, kw_norm_w, gate_col0, ts):
    b, s, _ = nq.shape
    g = NSA_GROUPS
    tile2 = lambda w: jnp.concatenate([w, w]).astype(F32)[None, :]
    ones_blk = np.kron(np.eye(2, dtype=np.float32), np.ones((NSA_D, NSA_D), np.float32))
    full = lambda shape: pl.BlockSpec(shape, lambda ib, ij: (0,) * len(shape))
    tok = lambda w: pl.BlockSpec((None, ts, w), lambda ib, ij: (ib, ij, 0))
    tab = pl.BlockSpec((ts, LANES), lambda ib, ij: (ij, 0))
    kv_out = lambda w: pl.BlockSpec((None, g, ts, w), lambda ib, ij: (ib, 0, ij, 0))
    kv_shape = lambda w, dt: jax.ShapeDtypeStruct((b, g, s, w), dt)
    vt_out = pl.BlockSpec((None, g, NSA_VROWS, ts), lambda ib, ij: (ib, 0, 0, ij))
    vt_shape = jax.ShapeDtypeStruct((b, g, NSA_VROWS, s), BF16)
    return pl.pallas_call(
        functools.partial(_nsa_prep_kernel, gate_col0=gate_col0),
        grid=(b, s // ts),
        in_specs=[tok(NSA_W), tok(4 * LANES), tok(LANES), tab, tab, tab,
                  full((1, LANES)), full((1, LANES)), full((1, LANES)), full((LANES, LANES))],
        out_specs=[tok(NSA_W), kv_out(2 * LANES), vt_out, kv_out(2 * LANES), vt_out, kv_out(LANES)],
        out_shape=[jax.ShapeDtypeStruct((b, s, NSA_W), BF16), kv_shape(2 * LANES, BF16), vt_shape,
                   kv_shape(2 * LANES, BF16), vt_shape, kv_shape(LANES, F32)],
        compiler_params=_cparams(("arbitrary", "arbitrary")),
        name="nsa_prep",
    )(nq, kv4, small, *tabs, tile2(q_norm_w), tile2(ks_norm_w), tile2(kw_norm_w), jnp.asarray(ones_blk))


def _nsa_compress_kernel(kc_ref, vc_ref, tc_ref, ta_ref, tb_ref, pk_ref, pv_ref, kw1_ref, kw2_ref,
                         vw1_ref, vw2_ref, nw_ref, ones_ref, kc_out, vc_out):
    st = NSA_CMP_STRIDE
    nrow = kc_ref.shape[0] // st
    outs = []
    for is_k in (True, False):
        src = kc_ref if is_k else vc_ref
        pos_ref, w1_ref, w2_ref = (pk_ref, kw1_ref, kw2_ref) if is_k else (pv_ref, vw1_ref, vw2_ref)
        first = second = None
        for l in range(st):
            rows = pl.ds(l, nrow, stride=st)
            x = src[rows, :]
            if is_k:
                x = _rope(x, tc_ref[rows, :], ta_ref[rows, :], tb_ref[rows, :])
            f = _mm(x + pos_ref[l:l + 1, :], w1_ref[l * LANES:(l + 1) * LANES, :])
            s2 = _mm(x + pos_ref[st + l:st + l + 1, :], w1_ref[(st + l) * LANES:(st + l + 1) * LANES, :])
            first = f if first is None else first + f
            second = s2 if second is None else second + s2
        y = first + pltpu.roll(second, nrow - 1, 0)
        if is_k:
            y = _mm(_silu(y), w2_ref[...])
            y = y * lax.rsqrt(_group_ms(y, ones_ref[...]) + EPS) * nw_ref[...]
        else:
            y = _mm_nt(w2_ref[...], _silu(y))
        outs.append(y)
    for ig, kg in enumerate(_dup_groups(outs[0])):
        kc_out[ig] = kg.astype(kc_out.dtype)
    for ig in range(NSA_GROUPS):
        vc_out[ig] = _vt_block(outs[1][ig * NSA_D:(ig + 1) * NSA_D])


def _nsa_compress(kc, vc, tabs, pos_k, pos_v, k_w1, k_w2, v_w1, v_w2, kc_norm_w):
    b, s, _ = kc.shape
    g = NSA_GROUPS
    st = NSA_CMP_STRIDE
    nrow = s // st
    width = st * LANES
    eye_g = jnp.eye(g, dtype=F32)

    def w1_blk(w1):
        wl = w1.reshape(NSA_CMP_BLOCK, NSA_D, NSA_D)
        return jnp.einsum("lde,gh->lgdhe", wl, eye_g).reshape(NSA_CMP_BLOCK * LANES, LANES).astype(BF16)

    def w2_blk(w2):
        return jnp.einsum("de,gh->gdhe", w2, eye_g).reshape(LANES, LANES).astype(BF16)

    def pos_rows(p):
        return jnp.concatenate([p, p], axis=-1).astype(F32)

    ones_blk = np.kron(np.eye(2, dtype=np.float32), np.ones((NSA_D, NSA_D), np.float32))
    full = lambda shape: pl.BlockSpec(shape, lambda ib: (0,) * len(shape))
    seq = pl.BlockSpec((None, s, LANES), lambda ib: (ib, 0, 0))
    out = pl.BlockSpec((None, g, nrow, LANES), lambda ib: (ib, 0, 0, 0))
    oshape = jax.ShapeDtypeStruct((b, g, nrow, LANES), BF16)
    return pl.pallas_call(
        _nsa_compress_kernel,
        grid=(b,),
        in_specs=[seq, seq, full((s, LANES)), full((s, LANES)), full((s, LANES)),
                  full((NSA_CMP_BLOCK, LANES)), full((NSA_CMP_BLOCK, LANES)),
                  full((2 * width, LANES)), full((LANES, LANES)), full((2 * width, LANES)), full((LANES, LANES)),
                  full((1, LANES)), full((LANES, LANES))],
        out_specs=[out, pl.BlockSpec((None, g, NSA_VROWS, nrow), lambda ib: (ib, 0, 0, 0))],
        out_shape=[oshape, jax.ShapeDtypeStruct((b, g, NSA_VROWS, nrow), BF16)],
        compiler_params=_cparams(("arbitrary",)),
        name="nsa_compress",
    )(kc, vc, *tabs, pos_rows(pos_k), pos_rows(pos_v), w1_blk(k_w1), w2_blk(k_w2),
      w1_blk(v_w1), w2_blk(v_w2).T, jnp.concatenate([kc_norm_w, kc_norm_w]).astype(F32)[None, :],
      jnp.asarray(ones_blk))


def _nsa_attn_kernel(q_ref, gs_ref, kc_ref, vct_ref, ks_ref, vst_ref, kw_ref, vwt_ref, ovt_ref, tri_ref, wb_ref,
                     o_ref, *, kb, wlen):
    jp = pl.program_id(1)
    tq = NSA_Q_BLOCK
    rep = NSA_REP
    rows = rep * tq
    pairw = 2 * tq
    halves = range(2)
    chains = [(g, h) for g in range(NSA_GROUPS) for h in halves]
    nc = len(chains)
    blk_i = [2 * jp + h for h in halves]
    ncmp = kc_ref.shape[1]
    nblk = ks_ref.shape[1] // NSA_SEL_BLOCK
    lsum = slice(NSA_D, NSA_D + 1)

    lane_q = lax.broadcasted_iota(jnp.int32, (tq, LANES), 1)
    qs = []
    for g, h in chains:
        pieces = []
        for r in range(rep):
            pair = g * (rep // 2) + r // 2
            tile = q_ref[h * tq:(h + 1) * tq, pair * LANES:(pair + 1) * LANES]
            keep = (lane_q < NSA_D) if r % 2 == 0 else (lane_q >= NSA_D)
            pieces.append(jnp.where(keep, tile, jnp.zeros_like(tile)))
        qs.append(jnp.concatenate(pieces, axis=0))

    sc_raw = [_mm_nt(kc_ref[g], qs[c]) for c, (g, h) in enumerate(chains)]
    base = pl.multiple_of(jp * pairw, pairw)
    sd = [_mm_nt(ks_ref[g, pl.ds(base, pairw), 0:LANES], qs[c]) + tri_ref[h]
          for c, (g, h) in enumerate(chains)]
    wq_bias = jnp.where(lax.broadcasted_iota(jnp.int32, (rows, LANES), 1) == NSA_D, NEG, 0.0).astype(BF16)
    wparts = [(o, min(NSA_SUB_KEYS, wlen - o)) for o in range(0, wlen, NSA_SUB_KEYS)]
    wrows = [pl.ds(base + o, n) for o, n in wparts]
    sw = [[_mm_nt(kw_ref[g, wrows[j], :], jnp.concatenate([qs[c], wq_bias], axis=1)) + wb_ref[h, o:o + n, :]
           for j, (o, n) in enumerate(wparts)] for c, (g, h) in enumerate(chains)]

    cend = lax.broadcasted_iota(jnp.int32, (ncmp, rows), 0) * NSA_CMP_STRIDE + (NSA_CMP_BLOCK - 1)
    tloc = lax.broadcasted_iota(jnp.int32, (ncmp, rows), 1) & (tq - 1)
    cmask = [cend - blk_i[h] * tq <= tloc for h in halves]
    sc = [jnp.where(cmask[h], sc_raw[c], NEG) for c, (g, h) in enumerate(chains)]
    mc = [jnp.max(sc[c], axis=0, keepdims=True) for c in range(nc)]
    pc = [jnp.where(cmask[h], jnp.exp2(sc[c] - mc[c]), 0.0) for c, (g, h) in enumerate(chains)]
    acc_c = [_mm(vct_ref[g], pc[c]) for c, (g, h) in enumerate(chains)]
    inv_c = [1.0 / jnp.maximum(acc_c[c][lsum, :], 1e-30) for c in range(nc)]

    m0 = [jnp.max(sd[c], axis=0, keepdims=True) for c in range(nc)]
    pd = [jnp.exp2(sd[c] - m0[c]) for c in range(nc)]
    a0 = [_mm(vst_ref[g, :, pl.ds(base, pairw)], pd[c]) for c, (g, h) in enumerate(chains)]

    acc_w = []
    for c, (g, h) in enumerate(chains):
        mw = jnp.max(sw[c][0], axis=0, keepdims=True)
        for j in range(1, len(wparts)):
            mw = jnp.maximum(mw, jnp.max(sw[c][j], axis=0, keepdims=True))
        a = _mm(vwt_ref[g, :, wrows[0]], jnp.exp2(sw[c][0] - mw))
        for j in range(1, len(wparts)):
            a = a + _mm(vwt_ref[g, :, wrows[j]], jnp.exp2(sw[c][j] - mw))
        acc_w.append(a)

    gates, o_cw = [], []
    for c, (g, h) in enumerate(chains):
        gt = gs_ref[g, h * tq:(h + 1) * tq, :].T
        gates.append([jnp.concatenate([gt[r * 3 + x:r * 3 + x + 1, :] for r in range(rep)], axis=1)
                      for x in range(3)])
        o_cw.append((gates[c][0] * inv_c[c]) * acc_c[c] + (gates[c][2] / acc_w[c][lsum, :]) * acc_w[c])

    lane_g = lax.broadcasted_iota(jnp.int32, (LANES, LANES), 1)
    imp_c = []
    for c in range(nc):
        pn = pc[c] * inv_c[c]
        folded = pn[:, 0:LANES] + pn[:, LANES:2 * LANES]
        psum = folded + pltpu.roll(folded, tq, 1)
        imp_c.append(_dot_exact_lhs(ovt_ref[...], psum))
    imp = [jnp.where(lane_g < tq, imp_c[h], imp_c[2 + h])[0:tq] for h in halves]
    blk = lax.broadcasted_iota(jnp.int32, (tq, LANES), 0)
    valid = [blk <= blk_i[h] for h in halves]

    def ranked():
        out = []
        for h in halves:
            forced = (blk == 0) | (blk == blk_i[h]) | (blk == blk_i[h] - 1)
            key = jnp.where(valid[h], jnp.where(forced, 0x7F000000, pltpu.bitcast(imp[h], jnp.int32)), -1)
            key = jnp.where(blk < nblk, key, -2)
            key_m1 = key - 1
            rank = jnp.zeros((tq, LANES), jnp.int32)
            for jb in range(nblk):
                ahead = key[jb:jb + 1, :] > jnp.where(blk > jb, key_m1, key)
                rank = rank + jnp.where(ahead, 1, 0)
            out.append(jnp.where(rank < NSA_N_SEL, 1.0, 0.0))
        return tuple(out)

    sel = lax.cond(blk_i[0] >= NSA_N_SEL, ranked, lambda: tuple(jnp.where(valid[h], 1.0, 0.0) for h in halves))
    q2 = [None] * nc
    for h in halves:
        selneg_t = jnp.where((sel[h] > 0.5) & (blk < 2 * jp), 0.0, NEG)
        selneg = jnp.concatenate([selneg_t, jnp.zeros((LANES - tq, LANES), F32)], axis=0).T
        for g in range(NSA_GROUPS):
            bias = selneg[g * tq:(g + 1) * tq].astype(BF16)
            q2[2 * g + h] = jnp.concatenate([qs[2 * g + h], jnp.concatenate([bias] * rep, axis=0)], axis=1)

    nsub = kb // NSA_SUB_KEYS

    def slc_body(ic, carry):
        m_i, acc = carry
        k0 = pl.multiple_of(ic * kb, kb)
        sub_rows = [pl.ds(k0 + j * NSA_SUB_KEYS, NSA_SUB_KEYS) for j in range(nsub)]
        m_out, acc_out = [None] * nc, [None] * nc

        def score(c):
            return [_mm_nt(ks_ref[chains[c][0], sub_rows[j], :], q2[c]) for j in range(nsub)]

        def update(c, s):
            m_chunk = s[0]
            for j in range(1, nsub):
                m_chunk = jnp.maximum(m_chunk, s[j])
            m_new = jnp.maximum(m_i[c], jnp.max(m_chunk, axis=0, keepdims=True))
            a = jnp.exp2(m_i[c] - m_new) * acc[c]
            for j in range(nsub):
                a = a + _mm(vst_ref[chains[c][0], :, sub_rows[j]], jnp.exp2(s[j] - m_new))
            m_out[c], acc_out[c] = m_new, a

        s_prev = score(0)
        for c in range(1, nc):
            s_next = score(c)
            update(c - 1, s_prev)
            s_prev = s_next
        update(nc - 1, s_prev)
        return tuple(m_out), tuple(acc_out)

    nch = (2 * jp * tq + kb - 1) // kb
    _, acc_s = lax.fori_loop(0, nch, slc_body, (tuple(m0), tuple(a0)))

    lo = lane_q < NSA_D
    for c, (g, h) in enumerate(chains):
        ot = (o_cw[c] + (gates[c][1] / acc_s[c][lsum, :]) * acc_s[c])
        ot = jnp.concatenate([ot, jnp.zeros((LANES - NSA_VROWS, rows), F32)], axis=0)
        heads = []
        for hp in range(rows // LANES):
            o_pair = ot[:, hp * LANES:(hp + 1) * LANES].T
            heads += [o_pair[0:tq], o_pair[tq:2 * tq]]
        for pr_ in range(rep // 2):
            pair = g * (rep // 2) + pr_
            o_ref[h * tq:(h + 1) * tq, pair * LANES:(pair + 1) * LANES] = jnp.where(
                lo, heads[2 * pr_], pltpu.roll(heads[2 * pr_ + 1], NSA_D, 1)).astype(o_ref.dtype)


def _nsa_attn(q, gsig, kcmp, vcmp_t, ks3, vs_t, kw3, vw_t):
    b, s, _ = q.shape
    g = NSA_GROUPS
    rep = NSA_REP
    tq = NSA_Q_BLOCK
    rows = rep * tq
    ncmp = kcmp.shape[2]
    nblk = s // NSA_SEL_BLOCK
    assert nblk <= tq and nblk <= NSA_D and (s // tq) % 2 == 0
    kb = min(1024, s)
    wpad = NSA_WINDOW
    wlen = NSA_WINDOW + 2 * tq
    ci = np.arange(ncmp) * NSA_CMP_STRIDE
    sj = np.arange(nblk) * NSA_SEL_BLOCK
    ov = np.clip(np.minimum(ci[None, :] + NSA_CMP_BLOCK, sj[:, None] + NSA_SEL_BLOCK)
                 - np.maximum(ci[None, :], sj[:, None]), 0, None).astype(np.float32) / NSA_CMP_STRIDE
    ov_t = np.zeros((LANES, ncmp), np.float32)
    ov_t[:nblk] = ov
    tloc = (np.arange(rows) % tq)[None, :]
    kcol = np.arange(2 * tq)[:, None]
    tri = np.stack([
        np.where(kcol <= tloc, 0.0, NEG),
        np.where(kcol < tq, 0.0, np.where(kcol - tq <= tloc, 0.0, NEG)),
    ]).astype(np.float32)
    wcol = np.arange(wlen)[:, None]
    wb = np.stack([
        np.where((wcol <= wpad + half * tq + tloc) & (wcol > half * tq + tloc), 0.0, NEG)
        for half in range(2)]).astype(np.float32)
    front = np.zeros((wpad, 2 * LANES), np.float32)
    front[:, LANES + NSA_D] = 1.0
    kw_p = jnp.concatenate([jnp.broadcast_to(jnp.asarray(front, dtype=BF16), (b, g, wpad, 2 * LANES)), kw3], axis=2)
    vw_p = jnp.pad(vw_t, ((0, 0), (0, 0), (0, 0), (wpad, 0)))
    sp = s + wpad
    seq = lambda n, w: pl.BlockSpec((None, g, n, w), lambda ib, ii: (ib, 0, 0, 0))
    full = lambda shape: pl.BlockSpec(shape, lambda ib, ii: (0,) * len(shape))
    return pl.pallas_call(
        functools.partial(_nsa_attn_kernel, kb=kb, wlen=wlen),
        grid=(b, s // (2 * tq)),
        in_specs=[
            pl.BlockSpec((None, 2 * tq, NSA_W), lambda ib, ii: (ib, ii, 0)),
            pl.BlockSpec((None, g, 2 * tq, LANES), lambda ib, ii: (ib, 0, ii, 0)),
            seq(ncmp, LANES), seq(NSA_VROWS, ncmp), seq(s, 2 * LANES), seq(NSA_VROWS, s),
            seq(sp, 2 * LANES), seq(NSA_VROWS, sp),
            full((LANES, ncmp)), full((2, 2 * tq, rows)), full((2, wlen, rows)),
        ],
        out_specs=pl.BlockSpec((None, 2 * tq, NSA_W), lambda ib, ii: (ib, ii, 0)),
        out_shape=jax.ShapeDtypeStruct((b, s, NSA_W), BF16),
        compiler_params=_cparams(("arbitrary", "arbitrary")),
        name="nsa_attn",
    )(q, gsig, kcmp, vcmp_t, ks3, vs_t, kw_p, vw_p, jnp.asarray(ov_t), jnp.asarray(tri), jnp.asarray(wb))


def _mem_kv_kernel(mem_ref, nw_ref, w_ref, knw_ref, k_ref, v_ref):
    x = mem_ref[...]
    ms = jnp.mean(x * x, axis=-1, keepdims=True)
    xn = x * lax.rsqrt(ms + EPS) * nw_ref[...]
    kv = _mm(xn, w_ref[...])
    v_ref[...] = kv[:, MEM_W:].astype(v_ref.dtype)
    for h in range(MEM_HEADS):
        hs = slice(h * MEM_D, (h + 1) * MEM_D)
        kh = kv[:, hs]
        msk = jnp.mean(kh * kh, axis=-1, keepdims=True)
        k_ref[:, hs] = (kh * lax.rsqrt(msk + EPS) * knw_ref[...]).astype(k_ref.dtype)


def _mem_kv(mem, mem_norm_w, w_kv, k_norm_w):
    b, m, d = mem.shape
    full = lambda shape: pl.BlockSpec(shape, lambda ib: (0,) * len(shape))
    blk = lambda w: pl.BlockSpec((None, m, w), lambda ib: (ib, 0, 0))
    return pl.pallas_call(
        _mem_kv_kernel,
        grid=(b,),
        in_specs=[blk(d), full((1, d)), full((d, 2 * MEM_W)), full((1, MEM_D))],
        out_specs=[blk(MEM_W), blk(MEM_W)],
        out_shape=[jax.ShapeDtypeStruct((b, m, MEM_W), BF16)] * 2,
        compiler_params=_cparams(("arbitrary",)),
        name="mem_kv",
    )(mem, mem_norm_w.astype(F32)[None, :], w_kv.astype(BF16), k_norm_w.astype(F32)[None, :])


def _mem_attn_kernel(q_ref, k_ref, v_ref, qnw_ref, o_ref):
    scale = MEM_D ** -0.5
    for h in range(MEM_HEADS):
        hs = slice(h * MEM_D, (h + 1) * MEM_D)
        qh = q_ref[:, hs]
        ms = jnp.mean(qh * qh, axis=-1, keepdims=True)
        qn = qh * lax.rsqrt(ms + EPS) * qnw_ref[...]
        s = _mm_nt(qn, k_ref[:, hs]) * scale
        m = jnp.max(s, axis=-1, keepdims=True)
        p = jnp.exp(s - m)
        l = jnp.sum(p, axis=-1, keepdims=True)
        o_ref[:, hs] = (_mm(p, v_ref[:, hs]) * (1.0 / l)).astype(o_ref.dtype)


def _mem_attn(mq, k, v, q_norm_w, ts):
    b, s, _ = mq.shape
    m = k.shape[1]
    return pl.pallas_call(
        _mem_attn_kernel,
        grid=(b, s // ts),
        in_specs=[
            pl.BlockSpec((None, ts, MEM_W), lambda ib, ij: (ib, ij, 0)),
            pl.BlockSpec((None, m, MEM_W), lambda ib, ij: (ib, 0, 0)),
            pl.BlockSpec((None, m, MEM_W), lambda ib, ij: (ib, 0, 0)),
            pl.BlockSpec((1, MEM_D), lambda ib, ij: (0, 0)),
        ],
        out_specs=pl.BlockSpec((None, ts, MEM_W), lambda ib, ij: (ib, ij, 0)),
        out_shape=jax.ShapeDtypeStruct((b, s, MEM_W), BF16),
        compiler_params=_cparams(("arbitrary", "arbitrary")),
        name="mem_attn",
    )(mq, k, v, q_norm_w.astype(F32)[None, :])


def _out_proj_kernel(x_hbm, oa_ref, ob_ref, oc_ref, w_ref, h_ref, xbuf, xsem):
    i = pl.program_id(0)
    n = pl.num_programs(0)
    tm = h_ref.shape[0]

    def x_copy(k):
        start = k * tm if isinstance(k, int) else pl.multiple_of(k * tm, tm)
        slot = k % OUT_X_SLOTS
        return pltpu.make_async_copy(x_hbm.at[pl.ds(start, tm), :], xbuf.at[slot], xsem.at[slot])

    @pl.when(i == 0)
    def _():
        for k in range(OUT_X_SLOTS - 1):
            x_copy(k).start()

    @pl.when(i + (OUT_X_SLOTS - 1) < n)
    def _():
        x_copy(i + (OUT_X_SLOTS - 1)).start()

    x_copy(i).wait()
    acc = xbuf[i % OUT_X_SLOTS]
    off = 0
    for o_ref in (oa_ref, ob_ref, oc_ref):
        wd = o_ref.shape[-1]
        acc = acc + jnp.dot(o_ref[...].astype(BF16), w_ref[off:off + wd, :], preferred_element_type=F32)
        off += wd
    h_ref[...] = acc


def _out_proj(x2, oa, ob, oc, w_out, tm):
    t, d = x2.shape
    assert t // tm >= OUT_X_SLOTS - 1
    row = lambda w: pl.BlockSpec((tm, w), lambda i: (i, 0))
    return pl.pallas_call(
        _out_proj_kernel,
        grid=(t // tm,),
        in_specs=[pl.BlockSpec(memory_space=pl.ANY), row(oa.shape[1]), row(ob.shape[1]), row(oc.shape[1]),
                  pl.BlockSpec(w_out.shape, lambda i: (0, 0))],
        out_specs=row(d),
        out_shape=jax.ShapeDtypeStruct((t, d), F32),
        scratch_shapes=[pltpu.VMEM((OUT_X_SLOTS, tm, d), F32), pltpu.SemaphoreType.DMA((OUT_X_SLOTS,))],
        compiler_params=_cparams(("arbitrary",)),
        name="out_proj",
    )(x2, oa, ob, oc, w_out)


def _ffn_kernel(h_ref, halo_ref, nw_ref, wup_ref, cw_ref, wdn_ref, o_ref, hn_s, u_s, act_s, *, fc):
    j = pl.program_id(1)
    ts = h_ref.shape[0]
    f = wdn_ref.shape[0]
    hl = halo_ref.shape[0]

    def norm(x):
        ms = jnp.mean(x * x, axis=-1, keepdims=True)
        return (x * lax.rsqrt(ms + EPS) * nw_ref[...]).astype(BF16)

    halo = jnp.where(j > 0, halo_ref[...], 0.0)
    hn_s[0:hl, :] = norm(halo)
    hn_s[hl:hl + ts, :] = norm(h_ref[...])
    for ic in range(f // fc):
        hn = hn_s[...]
        slot = ic % 2
        for part in range(2):
            cols = slice(part * f + ic * fc, part * f + (ic + 1) * fc)
            u_s[slot, part] = jnp.dot(hn, wup_ref[:, cols], preferred_element_type=F32)
        conv = []
        for part in range(2):
            cols = slice(part * f + ic * fc, part * f + (ic + 1) * fc)
            acc = cw_ref[FFN_CONV - 1:FFN_CONV, cols] * u_s[slot, part, hl:hl + ts, :]
            for jj in range(FFN_CONV - 1):
                acc = acc + cw_ref[jj:jj + 1, cols] * u_s[slot, part, pl.ds(hl - (FFN_CONV - 1) + jj, ts), :]
            conv.append(acc)
        act_s[:, ic * fc:(ic + 1) * fc] = (_silu(conv[0]) * conv[1]).astype(BF16)
    o_ref[...] = h_ref[...] + jnp.dot(act_s[...], wdn_ref[...], preferred_element_type=F32)


def _ffn(h, norm_w, w_up, conv_w, w_down, ts, fc):
    b, s, d = h.shape
    f = w_down.shape[0]
    full = lambda shape: pl.BlockSpec(shape, lambda ib, ij: (0,) * len(shape))
    return pl.pallas_call(
        functools.partial(_ffn_kernel, fc=fc),
        grid=(b, s // ts),
        in_specs=[
            pl.BlockSpec((None, ts, d), lambda ib, ij: (ib, ij, 0)),
            pl.BlockSpec((None, HALO_BF16, d), lambda ib, ij: (ib, jnp.maximum(ij * (ts // HALO_BF16) - 1, 0), 0)),
            full((1, d)), full((d, 2 * f)), full((FFN_CONV, 2 * f)), full((f, d)),
        ],
        out_specs=pl.BlockSpec((None, ts, d), lambda ib, ij: (ib, ij, 0)),
        out_shape=jax.ShapeDtypeStruct((b, s, d), F32),
        scratch_shapes=[
            pltpu.VMEM((ts + HALO_BF16, d), BF16),
            pltpu.VMEM((2, 2, ts + HALO_BF16, fc), F32),
            pltpu.VMEM((ts, f), BF16),
        ],
        compiler_params=_cparams(("arbitrary", "arbitrary")),
        name="ffn",
    )(h, h, norm_w.astype(F32)[None, :], w_up, conv_w.astype(F32), w_down)


def _split_w_in(w_in):
    sizes = (3 * GDN_W, GDN_HEADS, GDN_HEADS, GDN_W, NSA_W, NSA_KV_W, NSA_KV_W, NSA_KV_W, NSA_KV_W,
             NSA_KV_W, NSA_KV_W, 3 * NSA_HEADS, MEM_W)
    offs = np.concatenate([[0], np.cumsum(sizes)])
    (qkv, a, bb, gate, nq, kc, vc, ks, vs, kw, vw, ng, mq) = [w_in[:, offs[i]:offs[i + 1]] for i in range(len(sizes))]
    n_small = 2 * GDN_HEADS + 3 * NSA_HEADS
    small = jnp.concatenate([a, bb, ng, jnp.zeros((w_in.shape[0], LANES - n_small), w_in.dtype)], axis=1)
    widths = (3 * GDN_W, GDN_W, NSA_W, NSA_KV_W, NSA_KV_W, 4 * NSA_KV_W, MEM_W, LANES)
    w_cat = jnp.concatenate([qkv, gate, nq, kc, vc, ks, vs, kw, vw, mq, small], axis=1).astype(BF16)
    return w_cat, widths


def _layer(x, mem, attn_norm_w, mem_norm_w, w_in, gdn_conv_w, gdn_a_log, gdn_dt_bias, gdn_out_norm_w,
           nsa_q_norm_w, nsa_kc_norm_w, nsa_ks_norm_w, nsa_kw_norm_w, nsa_cmp_pos_k, nsa_cmp_pos_v,
           nsa_cmp_k_w1, nsa_cmp_k_w2, nsa_cmp_v_w1, nsa_cmp_v_w2, mem_w_kv, mem_q_norm_w, mem_k_norm_w,
           w_out, ffn_norm_w, ffn_w_up, ffn_conv_w, ffn_w_down):
    b, s, d = x.shape
    t = b * s
    ts = min(512, s)
    x2 = x.reshape(t, d)

    w_cat, widths = _split_w_in(w_in)
    qkv, gate, nq, kc, vc, kv4, mq, small = _in_proj(x2, attn_norm_w.astype(F32)[None, :], w_cat, gdn_conv_w,
                                                     widths, ts, s // ts)
    r3 = lambda a: a.reshape(b, s, a.shape[-1])

    o_a = _gdn(r3(qkv), r3(small), r3(gate), gdn_a_log, gdn_dt_bias, gdn_out_norm_w, ts)

    tabs = _rope_tables(s)
    q_r, ks3, vs2, kw2, vw2, gsig = _nsa_prep(r3(nq), r3(kv4), r3(small), tabs, nsa_q_norm_w, nsa_ks_norm_w,
                                              nsa_kw_norm_w, 2 * GDN_HEADS, ts)
    kcmp, vcmp = _nsa_compress(r3(kc), r3(vc), tabs, nsa_cmp_pos_k, nsa_cmp_pos_v, nsa_cmp_k_w1, nsa_cmp_k_w2,
                               nsa_cmp_v_w1, nsa_cmp_v_w2, nsa_kc_norm_w)
    o_b = _nsa_attn(q_r, gsig, kcmp, vcmp, ks3, vs2, kw2, vw2)

    mk, mv = _mem_kv(mem, mem_norm_w, mem_w_kv, mem_k_norm_w)
    o_c = _mem_attn(r3(mq), mk, mv, mem_q_norm_w, ts)

    h = _out_proj(x2, o_a.reshape(t, GDN_W), o_b.reshape(t, NSA_W), o_c.reshape(t, MEM_W), w_out.astype(BF16), ts)
    out = _ffn(h.reshape(b, s, d), ffn_norm_w, ffn_w_up.astype(BF16), ffn_conv_w, ffn_w_down.astype(BF16), ts, 256)
    return out


def kernel(x, mem, attn_norm_w, mem_norm_w, w_in, gdn_conv_w, gdn_a_log, gdn_dt_bias, gdn_out_norm_w, nsa_q_norm_w, nsa_kc_norm_w, nsa_ks_norm_w, nsa_kw_norm_w, nsa_cmp_pos_k, nsa_cmp_pos_v, nsa_cmp_k_w1, nsa_cmp_k_w2, nsa_cmp_v_w1, nsa_cmp_v_w2, mem_w_kv, mem_q_norm_w, mem_k_norm_w, w_out, ffn_norm_w, ffn_w_up, ffn_conv_w, ffn_w_down):
    h = x
    for l in range(w_in.shape[0]):
        h = _layer(h, mem, attn_norm_w[l], mem_norm_w[l], w_in[l], gdn_conv_w[l], gdn_a_log[l], gdn_dt_bias[l],
                   gdn_out_norm_w[l], nsa_q_norm_w[l], nsa_kc_norm_w[l], nsa_ks_norm_w[l], nsa_kw_norm_w[l],
                   nsa_cmp_pos_k[l], nsa_cmp_pos_v[l], nsa_cmp_k_w1[l], nsa_cmp_k_w2[l], nsa_cmp_v_w1[l],
                   nsa_cmp_v_w2[l], mem_w_kv[l], mem_q_norm_w[l], mem_k_norm_w[l], w_out[l], ffn_norm_w[l],
                   ffn_w_up[l], ffn_conv_w[l], ffn_w_down[l])
    return h
```
